```python
import numpy as np
import jax
import jax.numpy as jnp
from jax import lax

D_MODEL = 2048
BATCH = 16
SEQ = 256
DEPTH = 4
DEC_BATCH = 4
DEC_SEQ = 2048
PAST_LEN = 256

GRID_W = 64
HEAD_DIM = 128
ROPE_THETA = 10000.0
EPS = 1e-6
N_MOD = 6
N_BRANCH = 4

POOL_GROUPS = 4
POOL_GROUP_DIM = 128
POOL_WIDTH = POOL_GROUPS * POOL_GROUP_DIM
POOL_WINDOWS = (2, 4, 8, 16)

ATTN_Q_HEADS = 8
ATTN_KV_HEADS = 2
ATTN_GROUP = ATTN_Q_HEADS // ATTN_KV_HEADS
ATTN_WINDOW = 128
ATTN_BLOCK = 128
ATTN_Q_W = ATTN_Q_HEADS * HEAD_DIM
ATTN_KV_W = ATTN_KV_HEADS * HEAD_DIM

DN_HEADS = 4
DN_DK = 128
DN_DV = 128
DN_W = DN_HEADS * DN_DV
DN_CONV = 4
DN_CHUNK = 64

GLA_HEADS = 4
GLA_DK = 64
GLA_DV = 128
GLA_QK_W = GLA_HEADS * GLA_DK
GLA_V_W = GLA_HEADS * GLA_DV
GLA_RANK = 16
GLA_NORMALIZER = 16.0
GLA_CHUNK = 64

FFN_HIDDEN = -(-(8 * D_MODEL) // (3 * 256)) * 256

IN_SIZES = (POOL_WIDTH, ATTN_Q_W, ATTN_KV_W, ATTN_KV_W, 3 * DN_W, DN_W, 2 * DN_HEADS, 2 * DN_HEADS,
            GLA_QK_W, GLA_QK_W, GLA_V_W, GLA_V_W, 2 * GLA_RANK, N_BRANCH * D_MODEL)
IN_WIDTH = sum(IN_SIZES)

kernel_name = 'hybrid_flow_backbone_ctx_prefix_step'


def _rms(x, w):
    xf = x.astype(jnp.float32)
    y = xf * lax.rsqrt(jnp.mean(xf * xf, axis=-1, keepdims=True) + EPS)
    return (y * w.astype(jnp.float32)).astype(x.dtype)


def _l2norm(x):
    return x * lax.rsqrt(jnp.sum(x * x, axis=-1, keepdims=True) + EPS)


def _modulation(cond, w_mod, b_mod):
    m = jax.nn.silu(cond) @ w_mod + b_mod
    return jnp.split(m[..., None, :], N_MOD, axis=-1)


def _grid_positions(n_tok):
    rows = n_tok // GRID_W
    row = jnp.repeat(jnp.arange(rows, dtype=jnp.float32), GRID_W)
    col = jnp.tile(jnp.arange(GRID_W, dtype=jnp.float32), rows)
    return row, col


def _axial_rope(x, row, col):
    quarter = HEAD_DIM // 4
    half = HEAD_DIM // 2
    inv = ROPE_THETA ** (-jnp.arange(quarter, dtype=jnp.float32) / quarter)
    shape = (1, x.shape[1]) + (1,) * (x.ndim - 3) + (quarter,)

    def rot(xp, p):
        ang = (p[:, None] * inv[None, :]).reshape(shape)
        cs, sn = jnp.cos(ang), jnp.sin(ang)
        x1, x2 = xp[..., :quarter], xp[..., quarter:]
        return jnp.concatenate([x1 * cs - x2 * sn, x1 * sn + x2 * cs], axis=-1)

    return jnp.concatenate([rot(x[..., :half], row), rot(x[..., half:], col)], axis=-1)


def _pool_mixer(u, w_group, scale):
    B, T, _ = u.shape
    uf = u.astype(jnp.float32)
    cs = jnp.concatenate([jnp.zeros((B, 1, POOL_WIDTH), jnp.float32), jnp.cumsum(uf, axis=1)], axis=1)
    pos = jnp.arange(T)
    outs = []
    for gi, win in enumerate(POOL_WINDOWS):
        lo = jnp.clip(pos - win // 2, 0, T)
        hi = jnp.clip(pos + win // 2, 0, T)
        sl = slice(gi * POOL_GROUP_DIM, (gi + 1) * POOL_GROUP_DIM)
        csg = cs[..., sl]
        ssum = jnp.take(csg, hi, axis=1) - jnp.take(csg, lo, axis=1)
        cnt = (hi - lo).astype(jnp.float32)[None, :, None]
        outs.append(ssum / cnt - uf[..., sl])
    p = jnp.stack(outs, axis=2)
    y = jnp.einsum('btgc,gcd->btgd', p, w_group.astype(jnp.float32)).reshape(B, T, POOL_WIDTH)
    return y * scale.astype(jnp.float32)


def _attend_block(q, k_c, v_c, sink, k_l=None, v_l=None, mask_l=None):
    B, Q = q.shape[:2]
    s_c = jnp.einsum('bqngd,bpnd->bngqp', q, k_c)
    snk = jnp.broadcast_to(sink.astype(jnp.float32).reshape(1, ATTN_KV_HEADS, ATTN_GROUP, 1, 1),
                           (B, ATTN_KV_HEADS, ATTN_GROUP, Q, 1))
    logits = [snk, s_c]
    if k_l is not None:
        s_l = jnp.einsum('bqngd,blnd->bngql', q, k_l)
        logits.append(jnp.where(mask_l, s_l, -1e30))
    p = jax.nn.softmax(jnp.concatenate(logits, axis=-1), axis=-1)
    P = k_c.shape[1]
    o = jnp.einsum('bngqp,bpnd->bqngd', p[..., 1:1 + P], v_c)
    if k_l is not None:
        o = o + jnp.einsum('bngql,blnd->bqngd', p[..., 1 + P:], v_l)
    return o


def _context_attention(q, k, v, sink):
    B, S = q.shape[:2]
    nb = S // ATTN_BLOCK
    qb = q.reshape(B, nb, ATTN_BLOCK, ATTN_KV_HEADS, ATTN_GROUP, HEAD_DIM).transpose(1, 0, 2, 3, 4, 5)
    out = lax.map(lambda qi: _attend_block(qi, k, v, sink), qb)
    return out.transpose(1, 0, 2, 3, 4, 5).reshape(B, S, ATTN_KV_HEADS, ATTN_GROUP, HEAD_DIM)


def _latent_attention(q, k, v, k_c, v_c, sink):
    B, T = q.shape[:2]
    nb = T // ATTN_BLOCK
    pad = ((0, 0), (ATTN_BLOCK, ATTN_BLOCK), (0, 0), (0, 0))
    kp = jnp.pad(k, pad)
    vp = jnp.pad(v, pad)
    qb = q.reshape(B, nb, ATTN_BLOCK, ATTN_KV_HEADS, ATTN_GROUP, HEAD_DIM).transpose(1, 0, 2, 3, 4, 5)

    def blk(args):
        i, qi = args
        kl = lax.dynamic_slice_in_dim(kp, i * ATTN_BLOCK, 3 * ATTN_BLOCK, axis=1)
        vl = lax.dynamic_slice_in_dim(vp, i * ATTN_BLOCK, 3 * ATTN_BLOCK, axis=1)
        qpos = i * ATTN_BLOCK + jnp.arange(ATTN_BLOCK)
        kpos = (i - 1) * ATTN_BLOCK + jnp.arange(3 * ATTN_BLOCK)
        mask = (jnp.abs(qpos[:, None] - kpos[None, :]) <= ATTN_WINDOW) & (kpos[None, :] >= 0) & (kpos[None, :] < T)
        return _attend_block(qi, k_c, v_c, sink, kl, vl, mask)

    out = lax.map(blk, (jnp.arange(nb), qb))
    return out.transpose(1, 0, 2, 3, 4, 5).reshape(B, T, ATTN_KV_HEADS, ATTN_GROUP, HEAD_DIM)


def _short_conv(x, w):
    C = x.shape[-1]
    return lax.conv_general_dilated(x, w[:, None, :], window_strides=(1,),
                                    padding=[(DN_CONV // 2, DN_CONV - 1 - DN_CONV // 2)],
                                    dimension_numbers=('NWC', 'WIO', 'NWC'), feature_group_count=C)


def _gated_delta_chunked(q, k, v, g, beta, s0):
    B, H, T, DK = q.shape
    DV = v.shape[-1]
    C = DN_CHUNK
    n = T // C
    q = q.reshape(B, H, n, C, DK)
    k = k.reshape(B, H, n, C, DK)
    v = v.reshape(B, H, n, C, DV)
    beta = beta.reshape(B, H, n, C)
    gc = jnp.cumsum(g.reshape(B, H, n, C), axis=-1)
    incl = jnp.tril(jnp.ones((C, C), dtype=bool))
    strict = jnp.tril(jnp.ones((C, C), dtype=bool), -1)
    decay = jnp.exp(jnp.where(incl, gc[..., :, None] - gc[..., None, :], -jnp.inf))
    kb = k * beta[..., None]
    L = jnp.where(strict, jnp.einsum('bhnid,bhnjd->bhnij', kb, k) * decay, 0.0)
    eye = jnp.eye(C, dtype=jnp.float32)
    tinv = lax.linalg.triangular_solve(eye + L, jnp.broadcast_to(eye, L.shape), left_side=True, lower=True,
                                       unit_diagonal=True)
    u = jnp.einsum('bhnij,bhnje->bhnie', tinv, v * beta[..., None])
    w = jnp.einsum('bhnij,bhnjd->bhnid', tinv, kb * jnp.exp(gc)[..., None])
    a_intra = jnp.einsum('bhnid,bhnjd->bhnij', q, k) * decay
    qe = q * jnp.exp(gc)[..., None]
    kd = k * jnp.exp(gc[..., -1:] - gc)[..., None]
    dl = jnp.exp(gc[..., -1])
    mv = lambda t: jnp.moveaxis(t, 2, 0)

    def step(S, xs):
        qe_i, kd_i, u_i, w_i, a_i, dl_i = xs
        v_new = u_i - jnp.einsum('bhcd,bhde->bhce', w_i, S)
        o = jnp.einsum('bhcd,bhde->bhce', qe_i, S) + jnp.einsum('bhcj,bhje->bhce', a_i, v_new)
        S = dl_i[..., None, None] * S + jnp.einsum('bhcd,bhce->bhde', kd_i, v_new)
        return S, o

    S, o = lax.scan(step, s0, (mv(qe), mv(kd), mv(u), mv(w), mv(a_intra), mv(dl)))
    return jnp.moveaxis(o, 0, 2).reshape(B, H, T, DV), S


def _deltanet(qkv, z, a_raw, b_raw, conv_w, a_log, dt_bias, norm_w, s0):
    B, T, _ = qkv.shape
    qkv = jax.nn.silu(_short_conv(qkv.astype(jnp.float32), conv_w.astype(jnp.float32)))
    heads = lambda t: t.reshape(B, T, DN_HEADS, -1).transpose(0, 2, 1, 3)
    q, k, v = [heads(t) for t in jnp.split(qkv, 3, axis=-1)]
    q = _l2norm(q) * DN_DK ** -0.5
    k = _l2norm(k)
    a = a_raw.astype(jnp.float32).reshape(B, T, 2, DN_HEADS)
    g = -jnp.exp(a_log.astype(jnp.float32)) * jax.nn.softplus(a + dt_bias.astype(jnp.float32))
    beta = jax.nn.sigmoid(b_raw.astype(jnp.float32).reshape(B, T, 2, DN_HEADS))
    g = g.transpose(2, 0, 3, 1)
    beta = beta.transpose(2, 0, 3, 1)
    s0 = s0.astype(jnp.float32)
    flip = lambda t: jnp.flip(t, axis=2)
    o_f, s_f = _gated_delta_chunked(q, k, v, g[0], beta[0], s0[:, 0])
    o_b, s_b = _gated_delta_chunked(flip(q), flip(k), flip(v), flip(g[1]), flip(beta[1]), s0[:, 1])
    o = (o_f + flip(o_b)).transpose(0, 2, 1, 3)
    o = _rms(o, norm_w) * jax.nn.silu(z.astype(jnp.float32).reshape(B, T, DN_HEADS, DN_DV))
    return o.reshape(B, T, DN_W), jnp.stack([s_f, s_b], axis=1)


def _gla_chunked(q, k, v, g, s0):
    B, H, T, DK = q.shape
    DV = v.shape[-1]
    C = GLA_CHUNK
    n = T // C
    q = q.reshape(B, H, n, C, DK)
    k = k.reshape(B, H, n, C, DK)
    v = v.reshape(B, H, n, C, DV)
    gc = jnp.cumsum(g.reshape(B, H, n, C, DK), axis=3)
    incl = jnp.tril(jnp.ones((C, C), dtype=bool))
    qg = q * jnp.exp(gc)
    kg = k * jnp.exp(-gc)
    a_intra = jnp.where(incl, jnp.einsum('bhnid,bhnjd->bhnij', qg, kg), 0.0)
    kd = k * jnp.exp(gc[:, :, :, -1:, :] - gc)
    dl = jnp.exp(gc[:, :, :, -1, :])
    mv = lambda t: jnp.moveaxis(t, 2, 0)

    def step(S, xs):
        qg_i, a_i, v_i, kd_i, dl_i = xs
        o = jnp.einsum('bhcd,bhde->bhce', qg_i, S) + jnp.einsum('bhcj,bhje->bhce', a_i, v_i)
        S = dl_i[..., None] * S + jnp.einsum('bhcd,bhce->bhde', kd_i, v_i)
        return S, o

    S, o = lax.scan(step, s0, (mv(qg), mv(a_intra), mv(v), mv(kd), mv(dl)))
    return jnp.moveaxis(o, 0, 2).reshape(B, H, T, DV), S


def _gla(q, k, v, r, lr, w2, b2, norm_w, s0):
    B, T, _ = q.shape
    heads = lambda t, d: t.astype(jnp.float32).reshape(B, T, GLA_HEADS, d).transpose(0, 2, 1, 3)
    q = heads(q, GLA_DK) * GLA_DK ** -0.5
    k = heads(k, GLA_DK)
    v = heads(v, GLA_DV)
    lr = lr.astype(jnp.float32).reshape(B, T, 2, GLA_RANK)
    gk = jax.nn.log_sigmoid(jnp.einsum('btor,ork->obtk', lr, w2.astype(jnp.float32))
                            + b2.astype(jnp.float32)[:, None, None, :]) / GLA_NORMALIZER
    gk = gk.reshape(2, B, T, GLA_HEADS, GLA_DK).transpose(0, 1, 3, 2, 4)
    s0 = s0.astype(jnp.float32)
    flip = lambda t: jnp.flip(t, axis=2)
    o_f, s_f = _gla_chunked(q, k, v, gk[0], s0[:, 0])
    o_b, s_b = _gla_chunked(flip(q), flip(k), flip(v), flip(gk[1]), s0[:, 1])
    o = (o_f + flip(o_b)).transpose(0, 2, 1, 3)
    o = _rms(o, norm_w) * jax.nn.silu(r.astype(jnp.float32).reshape(B, T, GLA_HEADS, GLA_DV))
    return o.reshape(B, T, GLA_V_W), jnp.stack([s_f, s_b], axis=1)


def _layer(x, cond, lw, ctx):
    B, T, _ = x.shape
    dt = x.dtype
    sh1, sc1, gt1, sh2, sc2, gt2 = _modulation(cond, lw['w_mod'], lw['b_mod'])
    h = _rms(x, lw['norm1']) * (1 + sc1) + sh1
    parts = jnp.split(h @ lw['w_in'], np.cumsum(IN_SIZES)[:-1].tolist(), axis=-1)
    (u_pool, q_at, k_at, v_at, qkv_dn, z_dn, a_dn, b_dn,
     q_gl, k_gl, v_gl, r_gl, lr_gl, gate_logits) = parts

    y_a = _pool_mixer(u_pool, lw['pool_w'], lw['pool_scale'])

    q = q_at.astype(jnp.float32).reshape(B, T, ATTN_KV_HEADS, ATTN_GROUP, HEAD_DIM) * HEAD_DIM ** -0.5
    k = k_at.reshape(B, T, ATTN_KV_HEADS, HEAD_DIM)
    v = v_at.reshape(B, T, ATTN_KV_HEADS, HEAD_DIM)
    if ctx is None:
        y_b = _context_attention(q, k.astype(jnp.float32), v.astype(jnp.float32), lw['sink'])
        s_dn0 = jnp.zeros((B, 2, DN_HEADS, DN_DK, DN_DV), jnp.float32)
        s_gl0 = jnp.zeros((B, 2, GLA_HEADS, GLA_DK, GLA_DV), jnp.float32)
    else:
        k_c, v_c, s_dn0, s_gl0 = ctx
        row, col = _grid_positions(T)
        y_b = _latent_attention(_axial_rope(q, row, col), _axial_rope(k.astype(jnp.float32), row, col),
                                v.astype(jnp.float32), k_c.astype(jnp.float32), v_c.astype(jnp.float32),
                                lw['sink'])
    y_b = y_b.reshape(B, T, ATTN_Q_W)

    y_c, s_dn = _deltanet(qkv_dn, z_dn, a_dn, b_dn, lw['dn_conv'], lw['dn_a_log'], lw['dn_dt_bias'],
                          lw['dn_norm'], s_dn0)
    y_d, s_gl = _gla(q_gl, k_gl, v_gl, r_gl, lr_gl, lw['gla_w2'], lw['gla_b2'], lw['gla_norm'], s_gl0)

    g = jax.nn.sigmoid(gate_logits.astype(jnp.float32)).astype(dt).reshape(B, T, N_BRANCH, D_MODEL)
    merged = (g[:, :, 0] * (y_a.astype(dt) @ lw['w_br_pool'])
              + g[:, :, 1] * (y_b.astype(dt) @ lw['w_br_attn'])
              + g[:, :, 2] * (y_c.astype(dt) @ lw['w_br_delta'])
              + g[:, :, 3] * (y_d.astype(dt) @ lw['w_br_gla']))
    x = x + gt1 * (merged @ lw['w_out'])

    h2 = _rms(x, lw['norm2']) * (1 + sc2) + sh2
    x = x + gt2 * ((jax.nn.silu(h2 @ lw['w_gate']) * (h2 @ lw['w_up'])) @ lw['w_down'])
    return x, (k, v, s_dn, s_gl)


def setup_inputs(seed: int = 0) -> dict:
    key = jax.random.key(seed)
    ks = iter(jax.random.split(key, 48))
    nrm = lambda shape, scale: jax.random.normal(next(ks), shape, jnp.float32) * scale
    uni = lambda shape, lo, hi: jax.random.uniform(next(ks), shape, jnp.float32, lo, hi)
    L, D = DEPTH, D_MODEL
    dt_init = jnp.exp(uni((L, 2, DN_HEADS), float(np.log(1e-3)), float(np.log(1e-1))))
    return {
        'x_prompt': nrm((BATCH, SEQ, D), 1.0),
        'x_sample': nrm((DEC_BATCH, DEC_SEQ, D), 1.0),
        'cache_k': nrm((DEC_BATCH, L, PAST_LEN, ATTN_KV_HEADS, HEAD_DIM), 1.0),
        'cache_v': nrm((DEC_BATCH, L, PAST_LEN, ATTN_KV_HEADS, HEAD_DIM), 1.0),
        'state_delta': nrm((DEC_BATCH, L, 2, DN_HEADS, DN_DK, DN_DV), 0.5),
        'state_gla': nrm((DEC_BATCH, L, 2, GLA_HEADS, GLA_DK, GLA_DV), 0.5),
        'c': nrm((DEC_BATCH, D), 1.0),
        'c_ctx': nrm((D,), 1.0),
        'w_mod': nrm((L, D, N_MOD * D), 0.5 * D ** -0.5),
        'b_mod': nrm((L, N_MOD * D), 0.01),
        'norm1': 1.0 + nrm((L, D), 0.02),
        'w_in': nrm((L, D, IN_WIDTH), D ** -0.5),
        'pool_w': nrm((L, POOL_GROUPS, POOL_GROUP_DIM, POOL_GROUP_DIM), POOL_GROUP_DIM ** -0.5),
        'pool_scale': 1.0 + nrm((L, POOL_WIDTH), 0.02),
        'attn_sink': nrm((L, ATTN_Q_HEADS), 0.5),
        'dn_conv': nrm((L, DN_CONV, 3 * DN_W), DN_CONV ** -0.5),
        'dn_a_log': jnp.log(uni((L, 2, DN_HEADS), 1.0, 16.0)),
        'dn_dt_bias': dt_init + jnp.log(-jnp.expm1(-dt_init)),
        'dn_norm': 1.0 + nrm((L, DN_DV), 0.02),
        'gla_w2': nrm((L, 2, GLA_RANK, GLA_QK_W), GLA_RANK ** -0.5),
        'gla_b2': nrm((L, 2, GLA_QK_W), 0.1),
        'gla_norm': 1.0 + nrm((L, GLA_DV), 0.02),
        'w_br_pool': nrm((L, POOL_WIDTH, D), POOL_WIDTH ** -0.5),
        'w_br_attn': nrm((L, ATTN_Q_W, D), ATTN_Q_W ** -0.5),
        'w_br_delta': nrm((L, DN_W, D), DN_W ** -0.5),
        'w_br_gla': nrm((L, GLA_V_W, D), GLA_V_W ** -0.5),
        'w_out': nrm((L, D, D), D ** -0.5),
        'norm2': 1.0 + nrm((L, D), 0.02),
        'w_gate': nrm((L, D, FFN_HIDDEN), D ** -0.5),
        'w_up': nrm((L, D, FFN_HIDDEN), D ** -0.5),
        'w_down': nrm((L, FFN_HIDDEN, D), FFN_HIDDEN ** -0.5),
        'norm_f': 1.0 + nrm((D,), 0.02),
    }


def reference(x_prompt, x_sample, cache_k, cache_v, state_delta, state_gla, c, c_ctx,
              w_mod, b_mod, norm1, w_in, pool_w, pool_scale, attn_sink, dn_conv, dn_a_log, dn_dt_bias,
              dn_norm, gla_w2, gla_b2, gla_norm, w_br_pool, w_br_attn, w_br_delta, w_br_gla, w_out,
              norm2, w_gate, w_up, w_down, norm_f):
    xp, xs = x_prompt, x_sample
    ks, vs, sds, sgs = [], [], [], []
    for l in range(DEPTH):
        lw = {'w_mod': w_mod[l], 'b_mod': b_mod[l], 'norm1': norm1[l], 'w_in': w_in[l],
              'pool_w': pool_w[l], 'pool_scale': pool_scale[l], 'sink': attn_sink[l],
              'dn_conv': dn_conv[l], 'dn_a_log': dn_a_log[l], 'dn_dt_bias': dn_dt_bias[l], 'dn_norm': dn_norm[l],
              'gla_w2': gla_w2[l], 'gla_b2': gla_b2[l], 'gla_norm': gla_norm[l],
              'w_br_pool': w_br_pool[l], 'w_br_attn': w_br_attn[l], 'w_br_delta': w_br_delta[l],
              'w_br_gla': w_br_gla[l], 'w_out': w_out[l], 'norm2': norm2[l],
              'w_gate': w_gate[l], 'w_up': w_up[l], 'w_down': w_down[l]}
        xp, (k_l, v_l, sd_l, sg_l) = _layer(xp, c_ctx, lw, None)
        ks.append(k_l)
        vs.append(v_l)
        sds.append(sd_l)
        sgs.append(sg_l)
        xs, _ = _layer(xs, c, lw, (cache_k[:, l], cache_v[:, l], state_delta[:, l], state_gla[:, l]))
    y_prompt = _rms(xp, norm_f)
    y_sample = _rms(xs, norm_f)
    new_cache_k = jnp.stack(ks, axis=1)
    new_cache_v = jnp.stack(vs, axis=1)
    new_state_delta = jnp.stack(sds, axis=1)
    new_state_gla = jnp.stack(sgs, axis=1)
    return (y_prompt, y_sample, new_cache_k, new_cache_v, new_state_delta, new_state_gla)
```

```python
import functools

import numpy as np
import jax
import jax.numpy as jnp
from jax import lax
from jax.experimental import pallas as pl
from jax.experimental.pallas import tpu as pltpu

F32 = jnp.float32
BF16 = jnp.bfloat16

D = 2048
BATCH = 16
SEQ = 256
DEPTH = 4
DEC_BATCH = 4
DEC_SEQ = 2048
PAST = 256
GRID_W = 64
HD = 128
ROPE_THETA = 10000.0
EPS = 1e-6
N_MOD = 6
POOL_WINDOWS = (2, 4, 8, 16)
POOL_W = 512
Q_HEADS = 8
KV_HEADS = 2
GROUP = 4
ATTN_WINDOW = 128
DN_HEADS = 4
DN_DK = 128
DN_W = 512
DN_CONV = 4
GLA_HEADS = 4
GLA_DK = 64
GLA_DV = 128
GLA_RANK = 16
GLA_NORMALIZER = 16.0
FFN = 5632
CHUNK = 64

MP = BATCH * SEQ
MS = DEC_BATCH * DEC_SEQ
M = MP + MS
TS = 256
NT_P = MP // TS
TPS = DEC_SEQ // TS
NT_S = MS // TS
NT = NT_P + NT_S
HALO = 8

C_DNQKV, C_POOL, C_Q, C_DNZ, C_GV, C_GR = 0, 1536, 2048, 3072, 3584, 4096
C_K, C_V, C_GQ, C_GK, C_SM = 4608, 4864, 5120, 5376, 5632
MIXW = 5760
SM_A, SM_B, SM_LR = 0, 8, 16

VMEM_LIMIT = 56 * 1024 * 1024


def _params(sem):
    return pltpu.CompilerParams(dimension_semantics=sem, vmem_limit_bytes=VMEM_LIMIT)


def _cond_row(row0):
    return jnp.where(row0 < MP, 0, 1 + (row0 - MP) // DEC_SEQ)


def _tile_info(t):
    is_p = t < NT_P
    s = jnp.maximum(t - NT_P, 0)
    j = s % TPS
    first = jnp.logical_or(is_p, j == 0)
    last = jnp.logical_or(is_p, j == TPS - 1)
    base = jnp.where(is_p, 0, j * TS)
    seq_len = jnp.where(is_p, SEQ, DEC_SEQ)
    return is_p, first, last, base, seq_len


def _silu(x):
    return x * jax.nn.sigmoid(x)


def _softplus(x):
    return jnp.maximum(x, 0.0) + jnp.log(1.0 + jnp.exp(-jnp.abs(x)))


def _dot(a, b):
    return jnp.dot(a, b, preferred_element_type=F32)


def _dot_tb(a, b):
    return lax.dot_general(a, b, (((1,), (1,)), ((), ())), preferred_element_type=F32)


def _dot_ta(a, b):
    return lax.dot_general(a, b, (((0,), (0,)), ((), ())), preferred_element_type=F32)


def _split2(x):
    hi = x.astype(BF16)
    lo = (x - hi.astype(F32)).astype(BF16)
    return hi, lo


def _split3(x):
    p1 = x.astype(BF16)
    r = x - p1.astype(F32)
    p2 = r.astype(BF16)
    p3 = (r - p2.astype(F32)).astype(BF16)
    return p1, p2, p3


def _dot_hp(a, b):
    ah, al = _split2(a)
    bh, bl = _split2(b)
    return _dot(ah, bh) + (_dot(ah, bl) + _dot(al, bh))


def _mod_kernel(c_ref, w_ref, b_ref, o_ref):
    a = _silu(c_ref[...]).astype(BF16)
    o_ref[...] = _dot(a, w_ref[...].astype(BF16)) + b_ref[...]


def _modulation(cond8, w_mod, b_mod):
    tn = 1024
    n = N_MOD * D
    return pl.pallas_call(
        _mod_kernel,
        grid=(DEPTH, n // tn),
        in_specs=[pl.BlockSpec((8, D), lambda l, j: (0, 0)),
                  pl.BlockSpec((None, D, tn), lambda l, j: (l, 0, j)),
                  pl.BlockSpec((None, 1, tn), lambda l, j: (l, 0, j))],
        out_specs=pl.BlockSpec((None, 8, tn), lambda l, j: (l, 0, j)),
        out_shape=jax.ShapeDtypeStruct((DEPTH, 8, n), F32),
        compiler_params=_params(("parallel", "parallel")),
        name="modulation",
    )(cond8, w_mod, b_mod.reshape(DEPTH, 1, n))


def _rms_rows(x, w):
    ms = jnp.mean(x * x, axis=-1, keepdims=True)
    return x * lax.rsqrt(ms + EPS) * w


def _prenorm_kernel(x_ref, nw_ref, sc_ref, sh_ref, h_ref):
    y = _rms_rows(x_ref[...], nw_ref[...])
    h_ref[...] = (y * (1.0 + sc_ref[...]) + sh_ref[...]).astype(h_ref.dtype)


def _prenorm(x, nw, sc, sh):
    tm = 512
    cond = pl.BlockSpec((None, 1, D), lambda i: (_cond_row(i * tm), 0, 0))
    return pl.pallas_call(
        _prenorm_kernel,
        grid=(M // tm,),
        in_specs=[pl.BlockSpec((tm, D), lambda i: (i, 0)),
                  pl.BlockSpec((1, D), lambda i: (0, 0)), cond, cond],
        out_specs=pl.BlockSpec((tm, D), lambda i: (i, 0)),
        out_shape=jax.ShapeDtypeStruct((M, D), BF16),
        compiler_params=_params(("parallel",)),
        name="prenorm",
    )(x, nw.reshape(1, D), sc, sh)


def _finalnorm_kernel(x_ref, nw_ref, o_ref):
    o_ref[...] = _rms_rows(x_ref[...], nw_ref[...])


def _finalnorm(x, nw):
    tm = 512
    return pl.pallas_call(
        _finalnorm_kernel,
        grid=(M // tm,),
        in_specs=[pl.BlockSpec((tm, D), lambda i: (i, 0)),
                  pl.BlockSpec((1, D), lambda i: (0, 0))],
        out_specs=pl.BlockSpec((tm, D), lambda i: (i, 0)),
        out_shape=jax.ShapeDtypeStruct((M, D), F32),
        compiler_params=_params(("parallel",)),
        name="finalnorm",
    )(x, nw.reshape(1, D))


def _mm_kernel(a_ref, w_ref, o_ref):
    o_ref[...] = _dot(a_ref[...], w_ref[...]).astype(o_ref.dtype)


def _mm(a, w, tm, tn, out_dtype, name):
    m, k = a.shape
    n = w.shape[1]
    return pl.pallas_call(
        _mm_kernel,
        grid=(m // tm, n // tn),
        in_specs=[pl.BlockSpec((tm, k), lambda i, j: (i, 0)),
                  pl.BlockSpec((k, tn), lambda i, j: (0, j))],
        out_specs=pl.BlockSpec((tm, tn), lambda i, j: (i, j)),
        out_shape=jax.ShapeDtypeStruct((m, n), out_dtype),
        compiler_params=_params(("parallel", "arbitrary")),
        name=name,
    )(a, w)


def _mm_res_kernel(a_ref, w_ref, x_ref, gt_ref, o_ref):
    o_ref[...] = x_ref[...] + gt_ref[...] * _dot(a_ref[...], w_ref[...])


def _mm_res(a, w, x, gt, tm, tn, name):
    m, k = a.shape
    n = w.shape[1]
    return pl.pallas_call(
        _mm_res_kernel,
        grid=(m // tm, n // tn),
        in_specs=[pl.BlockSpec((tm, k), lambda i, j: (i, 0)),
                  pl.BlockSpec((k, tn), lambda i, j: (0, j)),
                  pl.BlockSpec((tm, tn), lambda i, j: (i, j)),
                  pl.BlockSpec((None, 1, tn), lambda i, j: (_cond_row(i * tm), 0, j))],
        out_specs=pl.BlockSpec((tm, tn), lambda i, j: (i, j)),
        out_shape=jax.ShapeDtypeStruct((m, n), F32),
        compiler_params=_params(("parallel", "arbitrary")),
        name=name,
    )(a, w, x, gt)


def _ffn_up_kernel(h_ref, wg_ref, wu_ref, o_ref):
    h = h_ref[...]
    o_ref[...] = (_silu(_dot(h, wg_ref[...])) * _dot(h, wu_ref[...])).astype(o_ref.dtype)


def _ffn_up(h, wg, wu):
    tm, tn = 1024, 512
    return pl.pallas_call(
        _ffn_up_kernel,
        grid=(M // tm, FFN // tn),
        in_specs=[pl.BlockSpec((tm, D), lambda i, j: (i, 0)),
                  pl.BlockSpec((D, tn), lambda i, j: (0, j)),
                  pl.BlockSpec((D, tn), lambda i, j: (0, j))],
        out_specs=pl.BlockSpec((tm, tn), lambda i, j: (i, j)),
        out_shape=jax.ShapeDtypeStruct((M, FFN), BF16),
        compiler_params=_params(("parallel", "arbitrary")),
        name="ffn_up",
    )(h, wg, wu)


def _merge_kernel(h_ref, g0_ref, g1_ref, g2_ref, g3_ref, ya_ref, yb_ref, yc_ref, yd_ref,
                  wa_ref, wb_ref, wc_ref, wd_ref, o_ref):
    h = h_ref[...]
    acc = None
    for g_ref, y_ref, w_ref in ((g0_ref, ya_ref, wa_ref), (g1_ref, yb_ref, wb_ref),
                                (g2_ref, yc_ref, wc_ref), (g3_ref, yd_ref, wd_ref)):
        term = jax.nn.sigmoid(_dot(h, g_ref[...])) * _dot(y_ref[...], w_ref[...])
        acc = term if acc is None else acc + term
    o_ref[...] = acc.astype(o_ref.dtype)


def _merge(h, wgate, ya, yb, yc, yd, wa, wb, wc, wd):
    tm, tn = 1024, 512
    nj = D // tn
    gate_specs = [pl.BlockSpec((D, tn), functools.partial(lambda i, j, b: (0, b * nj + j), b=b)) for b in range(4)]
    row = lambda w: pl.BlockSpec((tm, w), lambda i, j: (i, 0))
    col = lambda k: pl.BlockSpec((k, tn), lambda i, j: (0, j))
    return pl.pallas_call(
        _merge_kernel,
        grid=(M // tm, nj),
        in_specs=[row(D)] + gate_specs + [row(POOL_W), row(Q_HEADS * HD), row(DN_W), row(GLA_HEADS * GLA_DV),
                                          col(POOL_W), col(Q_HEADS * HD), col(DN_W), col(GLA_HEADS * GLA_DV)],
        out_specs=pl.BlockSpec((tm, tn), lambda i, j: (i, j)),
        out_shape=jax.ShapeDtypeStruct((M, D), BF16),
        compiler_params=_params(("parallel", "arbitrary")),
        name="merge",
    )(h, wgate, wgate, wgate, wgate, ya, yb, yc, yd, wa, wb, wc, wd)


def _pool_kernel(u_ref, up_ref, un_ref, w_ref, sc_ref, o_ref):
    t = pl.program_id(0)
    _, first, last, base, seq_len = _tile_info(t)
    u = u_ref[...]
    prev = jnp.where(first, 0.0, up_ref[...])
    nxt = jnp.where(last, 0.0, un_ref[...])
    ext = jnp.concatenate([prev, u, nxt], axis=0)
    pos = base + lax.broadcasted_iota(jnp.int32, (TS, 1), 0)
    outs = []
    for gi, win in enumerate(POOL_WINDOWS):
        lanes = slice(gi * 128, (gi + 1) * 128)
        p = ext[:, lanes]
        s = 1
        while s < win:
            p = p + pltpu.roll(p, s, 0)
            s *= 2
        d = HALO + win // 2 - 1
        ssum = p[d:d + TS]
        cnt = (jnp.minimum(pos + win // 2, seq_len) - jnp.maximum(pos - win // 2, 0)).astype(F32)
        pg = ssum / cnt - u[:, lanes]
        outs.append(_dot(pg.astype(BF16), w_ref[gi].astype(BF16)))
    o_ref[...] = (jnp.concatenate(outs, axis=1) * sc_ref[...]).astype(o_ref.dtype)


def _halo_specs(width, colblk):
    r = TS // HALO
    nblk = M // HALO
    prev = pl.BlockSpec((HALO, width), lambda t: (jnp.maximum(t * r - 1, 0), colblk))
    nxt = pl.BlockSpec((HALO, width), lambda t: (jnp.minimum((t + 1) * r, nblk - 1), colblk))
    return prev, nxt


def _pool(pm, pool_w, pool_scale):
    cb = C_POOL // POOL_W
    prev, nxt = _halo_specs(POOL_W, cb)
    return pl.pallas_call(
        _pool_kernel,
        grid=(NT,),
        in_specs=[pl.BlockSpec((TS, POOL_W), lambda t: (t, cb)), prev, nxt,
                  pl.BlockSpec((4, 128, 128), lambda t: (0, 0, 0)),
                  pl.BlockSpec((1, POOL_W), lambda t: (0, 0))],
        out_specs=pl.BlockSpec((TS, POOL_W), lambda t: (t, 0)),
        out_shape=jax.ShapeDtypeStruct((M, POOL_W), BF16),
        compiler_params=_params(("parallel",)),
        name="pool",
    )(pm, pm, pm, pool_w, pool_scale.reshape(1, POOL_W))


def _ctx_attn_kernel(sink_ref, q_ref, k_ref, v_ref, o_ref):
    q = q_ref[...] * (HD ** -0.5)
    k = k_ref[...].astype(BF16)
    v = v_ref[...].astype(BF16)
    for n in range(KV_HEADS):
        kn = k[:, n * HD:(n + 1) * HD]
        vn = v[:, n * HD:(n + 1) * HD]
        for g in range(GROUP):
            h = n * GROUP + g
            qh = q[:, h * HD:(h + 1) * HD].astype(BF16)
            s = _dot_tb(qh, kn)
            snk = sink_ref[h]
            m = jnp.maximum(jnp.max(s, axis=-1, keepdims=True), snk)
            e = jnp.exp(s - m)
            den = jnp.sum(e, axis=-1, keepdims=True) + jnp.exp(snk - m)
            o = _dot(e.astype(BF16), vn) / den
            o_ref[:, h * HD:(h + 1) * HD] = o.astype(o_ref.dtype)


def _ctx_attn(pm, sink):
    qw, kw = Q_HEADS * HD, KV_HEADS * HD
    return pl.pallas_call(
        _ctx_attn_kernel,
        grid=(NT_P,),
        in_specs=[pl.BlockSpec(memory_space=pltpu.SMEM),
                  pl.BlockSpec((TS, qw), lambda t: (t, C_Q // qw)),
                  pl.BlockSpec((TS, kw), lambda t: (t, C_K // kw)),
                  pl.BlockSpec((TS, kw), lambda t: (t, C_V // kw))],
        out_specs=pl.BlockSpec((TS, qw), lambda t: (t, 0)),
        out_shape=jax.ShapeDtypeStruct((MP, qw), BF16),
        compiler_params=_params(("parallel",)),
        name="ctx_attn",
    )(sink, pm, pm, pm)


def _rope(x, cs, sn):
    lane = lax.broadcasted_iota(jnp.int32, x.shape, 1)
    swapped = jnp.where((lane % 64) < 32, pltpu.roll(x, 96, 1), pltpu.roll(x, 32, 1))
    return x * cs + swapped * sn


def _lat_attn_kernel(sink_ref, q_ref, km_ref, kp_ref, kn_ref, vm_ref, vp_ref, vn_ref,
                     cm_ref, cp_ref, cn_ref, sm_ref, sp_ref, sn_ref, kc_ref, vc_ref, o_ref):
    ts = pl.program_id(0)
    base = (ts % TPS) * TS
    half = TS // 2
    nk = TS + 2 * half
    q = q_ref[...] * (HD ** -0.5)
    cq, sq = cm_ref[...], sm_ref[...]
    kl = jnp.concatenate([kp_ref[...], km_ref[...], kn_ref[...]], axis=0)
    vl = jnp.concatenate([vp_ref[...], vm_ref[...], vn_ref[...]], axis=0).astype(BF16)
    ck = jnp.concatenate([cp_ref[...], cq, cn_ref[...]], axis=0)
    sk = jnp.concatenate([sp_ref[...], sq, sn_ref[...]], axis=0)
    kc = kc_ref[...].astype(BF16)
    vc = vc_ref[...].astype(BF16)
    qpos = base + lax.broadcasted_iota(jnp.int32, (TS, 1), 0)
    kpos = base - half + lax.broadcasted_iota(jnp.int32, (1, nk), 1)
    dist = qpos - kpos
    ok = (jnp.abs(dist) <= ATTN_WINDOW).astype(jnp.int32) * (kpos >= 0).astype(jnp.int32) * (kpos < DEC_SEQ).astype(jnp.int32)
    bias = jnp.where(ok > 0, 0.0, -1e30)
    for n in range(KV_HEADS):
        lanes = slice(n * HD, (n + 1) * HD)
        kr = _rope(kl[:, lanes], ck, sk).astype(BF16)
        vln, kcn, vcn = vl[:, lanes], kc[:, lanes], vc[:, lanes]
        for g in range(GROUP):
            h = n * GROUP + g
            qh = _rope(q[:, h * HD:(h + 1) * HD], cq, sq).astype(BF16)
            s_c = _dot_tb(qh, kcn)
            s_l = _dot_tb(qh, kr) + bias
            snk = sink_ref[h]
            m = jnp.maximum(jnp.maximum(jnp.max(s_c, axis=-1, keepdims=True),
                                        jnp.max(s_l, axis=-1, keepdims=True)), snk)
            e_c = jnp.exp(s_c - m)
            e_l = jnp.exp(s_l - m)
            den = jnp.sum(e_c, axis=-1, keepdims=True) + jnp.sum(e_l, axis=-1, keepdims=True) + jnp.exp(snk - m)
            o = (_dot(e_c.astype(BF16), vcn) + _dot(e_l.astype(BF16), vln)) / den
            o_ref[:, h * HD:(h + 1) * HD] = o.astype(o_ref.dtype)


def _lat_attn(pm, sink, cos_t, sin_t, cache_k, cache_v, layer):
    qw, kw = Q_HEADS * HD, KV_HEADS * HD
    half = TS // 2
    nhb = M // half
    nhs = DEC_SEQ // half

    def main(w, cb):
        return pl.BlockSpec((TS, w), lambda ts: (NT_P + ts, cb))

    def prev(cb):
        return pl.BlockSpec((half, kw), lambda ts: (2 * (NT_P + ts) - 1, cb))

    def nxt(cb):
        return pl.BlockSpec((half, kw), lambda ts: (jnp.minimum(2 * (NT_P + ts) + 2, nhb - 1), cb))

    tab_m = pl.BlockSpec((TS, HD), lambda ts: (ts % TPS, 0))
    tab_p = pl.BlockSpec((half, HD), lambda ts: (jnp.maximum(2 * (ts % TPS) - 1, 0), 0))
    tab_n = pl.BlockSpec((half, HD), lambda ts: (jnp.minimum(2 * (ts % TPS) + 2, nhs - 1), 0))
    ctx = pl.BlockSpec((None, None, PAST, kw), lambda ts: (ts // TPS, layer, 0, 0))
    return pl.pallas_call(
        _lat_attn_kernel,
        grid=(NT_S,),
        in_specs=[pl.BlockSpec(memory_space=pltpu.SMEM),
                  main(qw, C_Q // qw),
                  main(kw, C_K // kw), prev(C_K // kw), nxt(C_K // kw),
                  main(kw, C_V // kw), prev(C_V // kw), nxt(C_V // kw),
                  tab_m, tab_p, tab_n, tab_m, tab_p, tab_n, ctx, ctx],
        out_specs=pl.BlockSpec((TS, qw), lambda ts: (ts, 0)),
        out_shape=jax.ShapeDtypeStruct((MS, qw), BF16),
        compiler_params=_params(("parallel",)),
        name="lat_attn",
    )(sink, pm, pm, pm, pm, pm, pm, pm, cos_t, cos_t, cos_t, sin_t, sin_t, sin_t,
      cache_k.reshape(DEC_BATCH, DEPTH, PAST, kw), cache_v.reshape(DEC_BATCH, DEPTH, PAST, kw))


def _rope_tables():
    quarter = HD // 4
    inv = ROPE_THETA ** (-jnp.arange(quarter, dtype=F32) / quarter)
    t = jnp.arange(DEC_SEQ)
    row = (t // GRID_W).astype(F32)
    col = (t % GRID_W).astype(F32)
    ar = row[:, None] * inv[None, :]
    ac = col[:, None] * inv[None, :]
    cos_t = jnp.concatenate([jnp.cos(ar), jnp.cos(ar), jnp.cos(ac), jnp.cos(ac)], axis=-1)
    sin_t = jnp.concatenate([-jnp.sin(ar), jnp.sin(ar), -jnp.sin(ac), jnp.sin(ac)], axis=-1)
    return cos_t, sin_t


def _scan_tile(reverse):
    i = pl.program_id(0)
    return (NT - 1 - i) if reverse else i


def _tri_masks(reverse):
    ii = lax.broadcasted_iota(jnp.int32, (CHUNK, CHUNK), 0)
    jj = lax.broadcasted_iota(jnp.int32, (CHUNK, CHUNK), 1)
    incl = (jj >= ii) if reverse else (jj <= ii)
    strict = (jj > ii) if reverse else (jj < ii)
    return incl, strict


def _gated_rms_out(o, gate, nw):
    outs = []
    for h in range(4):
        lanes = slice(h * 128, (h + 1) * 128)
        outs.append(_rms_rows(o[:, lanes], nw) * _silu(gate[:, lanes]))
    return jnp.concatenate(outs, axis=1)


def _inv_unit_tri(lm):
    ii = lax.broadcasted_iota(jnp.int32, (CHUNK, CHUNK), 0)
    jj = lax.broadcasted_iota(jnp.int32, (CHUNK, CHUNK), 1)
    eye = jnp.where(ii == jj, 1.0, 0.0)
    p = eye - lm
    mk = _dot_hp(lm, lm)
    p = p + _dot_hp(p, mk)
    span = 4
    while span < CHUNK:
        mk = _dot_hp(mk, mk)
        p = p + _dot_hp(p, mk)
        span *= 2
    return p


def _dn_kernel(reverse, has_prev, x_ref, xp_ref, xn_ref, sm_ref, cw_ref, alog_ref, dtb_ref, s0_ref, *rest):
    if has_prev:
        oprev_ref, z_ref, nw_ref, o_ref, st_ref, s_scr, q_scr, k_scr, v_scr, g_scr, b_scr, o_scr = rest
    else:
        o_ref, st_ref, s_scr, q_scr, k_scr, v_scr, g_scr, b_scr, o_scr = rest
    t = _scan_tile(reverse)
    is_p, first, last, _, _ = _tile_info(t)
    d = 1 if reverse else 0

    @pl.when(last if reverse else first)
    def _():
        s_scr[...] = jnp.where(is_p, 0.0, s0_ref[...])

    x = x_ref[...]
    ext = jnp.concatenate([jnp.where(first, 0.0, xp_ref[...]), x, jnp.where(last, 0.0, xn_ref[...])], axis=0)
    cw = cw_ref[...]
    conv = None
    for j in range(DN_CONV):
        r0 = HALO + j - DN_CONV // 2
        term = ext[r0:r0 + TS] * cw[j:j + 1, :]
        conv = term if conv is None else conv + term
    act = _silu(conv)
    for h in range(DN_HEADS):
        lanes = slice(h * 128, (h + 1) * 128)
        qh = act[:, h * 128:(h + 1) * 128]
        kh = act[:, DN_W + h * 128:DN_W + (h + 1) * 128]
        q_scr[:, lanes] = qh * lax.rsqrt(jnp.sum(qh * qh, axis=-1, keepdims=True) + EPS) * (DN_DK ** -0.5)
        k_scr[:, lanes] = kh * lax.rsqrt(jnp.sum(kh * kh, axis=-1, keepdims=True) + EPS)
    v_scr[...] = act[:, 2 * DN_W:3 * DN_W]
    sm = sm_ref[...]
    g_scr[...] = -jnp.exp(alog_ref[...]) * _softplus(sm + dtb_ref[...])
    b_scr[...] = jax.nn.sigmoid(sm)

    incl, strict = _tri_masks(reverse)
    tri = jnp.where(incl, 1.0, 0.0).astype(BF16)
    tri_t = jnp.where(_tri_masks(not reverse)[0], 1.0, 0.0).astype(BF16)

    def chunk_body(ci, carry):
        c = (TS // CHUNK - 1 - ci) if reverse else ci
        rows = pl.ds(pl.multiple_of(c * CHUNK, CHUNK), CHUNK)
        g1, g2, g3 = _split3(g_scr[rows, :])
        gcc = _dot(tri, g1) + _dot(tri, g2) + _dot(tri, g3)
        gcr = _dot_ta(g1, tri_t) + _dot_ta(g2, tri_t) + _dot_ta(g3, tri_t)
        bt = b_scr[rows, :]
        e_last = CHUNK - 1 if not reverse else 0
        for h in range(DN_HEADS):
            lanes = slice(h * 128, (h + 1) * 128)
            r = SM_A + d * DN_HEADS + h
            gcol = gcc[:, r:r + 1]
            grow = gcr[r:r + 1, :]
            glast = gcc[e_last:e_last + 1, r:r + 1]
            beta = bt[:, SM_B + d * DN_HEADS + h:SM_B + d * DN_HEADS + h + 1]
            qc, kc, vc = q_scr[rows, lanes], k_scr[rows, lanes], v_scr[rows, lanes]
            decay = jnp.exp(jnp.where(incl, gcol - grow, -1e30))
            kb = kc * beta
            kcb = kc.astype(BF16)
            lm = jnp.where(strict, _dot_tb(kb.astype(BF16), kcb) * decay, 0.0)
            tinv = _inv_unit_tri(lm).astype(BF16)
            eg = jnp.exp(gcol)
            u = _dot(tinv, (vc * beta).astype(BF16))
            w = _dot(tinv, (kb * eg).astype(BF16))
            a = _dot_tb(qc.astype(BF16), kcb) * decay
            qe = qc * eg
            kd = kc * jnp.exp(glast - gcol)
            s_old = s_scr[h]
            sb = s_old.astype(BF16)
            v_new = u - _dot(w.astype(BF16), sb)
            vnb = v_new.astype(BF16)
            o_scr[rows, lanes] = _dot(qe.astype(BF16), sb) + _dot(a.astype(BF16), vnb)
            s_scr[h] = jnp.exp(glast) * s_old + _dot_ta(kd.astype(BF16), vnb)
        return carry

    lax.fori_loop(0, TS // CHUNK, chunk_body, 0)

    if has_prev:
        o_ref[...] = _gated_rms_out(o_scr[...] + oprev_ref[...], z_ref[...], nw_ref[...]).astype(o_ref.dtype)
    else:
        o_ref[...] = o_scr[...]

    @pl.when(is_p)
    def _():
        st_ref[...] = s_scr[...]


def _seq_of_tile(t):
    return jnp.maximum(t - NT_P, 0) // TPS


def _dn_call(pm, reverse, layer, conv_w, alog, dtb, state_delta, prev=None, norm_w=None):
    d = 1 if reverse else 0
    tmap = (lambda i: NT - 1 - i) if reverse else (lambda i: i)
    width = 3 * DN_W
    r = TS // HALO
    nblk = M // HALO
    in_specs = [
        pl.BlockSpec((TS, width), lambda i: (tmap(i), C_DNQKV // width)),
        pl.BlockSpec((HALO, width), lambda i: (jnp.maximum(tmap(i) * r - 1, 0), 0)),
        pl.BlockSpec((HALO, width), lambda i: (jnp.minimum((tmap(i) + 1) * r, nblk - 1), 0)),
        pl.BlockSpec((TS, 128), lambda i: (tmap(i), C_SM // 128)),
        pl.BlockSpec((DN_CONV, width), lambda i: (0, 0)),
        pl.BlockSpec((1, 128), lambda i: (0, 0)),
        pl.BlockSpec((1, 128), lambda i: (0, 0)),
        pl.BlockSpec((None, None, None, DN_HEADS, DN_DK, 128), lambda i: (_seq_of_tile(tmap(i)), layer, d, 0, 0, 0)),
    ]
    args = [pm, pm, pm, pm, conv_w, alog, dtb, state_delta]
    has_prev = prev is not None
    if has_prev:
        in_specs += [pl.BlockSpec((TS, DN_W), lambda i: (tmap(i), 0)),
                     pl.BlockSpec((TS, DN_W), lambda i: (tmap(i), C_DNZ // DN_W)),
                     pl.BlockSpec((1, 128), lambda i: (0, 0))]
        args += [prev, pm, norm_w.reshape(1, 128)]
    out_dtype = BF16 if has_prev else F32
    return pl.pallas_call(
        functools.partial(_dn_kernel, reverse, has_prev),
        grid=(NT,),
        in_specs=in_specs,
        out_specs=[pl.BlockSpec((TS, DN_W), lambda i: (tmap(i), 0)),
                   pl.BlockSpec((None, DN_HEADS, DN_DK, 128), lambda i: (jnp.minimum(tmap(i), NT_P - 1), 0, 0, 0))],
        out_shape=[jax.ShapeDtypeStruct((M, DN_W), out_dtype),
                   jax.ShapeDtypeStruct((BATCH, DN_HEADS, DN_DK, 128), F32)],
        scratch_shapes=[pltpu.VMEM((DN_HEADS, DN_DK, 128), F32),
                        pltpu.VMEM((TS, DN_W), F32), pltpu.VMEM((TS, DN_W), F32), pltpu.VMEM((TS, DN_W), F32),
                        pltpu.VMEM((TS, 128), F32), pltpu.VMEM((TS, 128), F32),
                        pltpu.VMEM((TS, DN_W), F32)],
        compiler_params=_params(("arbitrary",)),
        name="deltanet_bwd" if reverse else "deltanet_fwd",
    )(*args)


def _gla_kernel(reverse, has_prev, q_ref, k_ref, v_ref, sm_ref, w2_ref, b2_ref, s0_ref, *rest):
    if has_prev:
        oprev_ref, r_ref, nw_ref, o_ref, st_ref, s_scr, g_scr, o_scr = rest
    else:
        o_ref, st_ref, s_scr, g_scr, o_scr = rest
    t = _scan_tile(reverse)
    is_p, first, last, _, _ = _tile_info(t)

    @pl.when(last if reverse else first)
    def _():
        s_scr[...] = jnp.where(is_p, 0.0, s0_ref[...])

    logits = _dot(sm_ref[...].astype(BF16), w2_ref[...]) + b2_ref[...]
    g_scr[...] = -_softplus(-logits) / GLA_NORMALIZER

    incl, _ = _tri_masks(reverse)
    tri = jnp.where(incl, 1.0, 0.0).astype(BF16)

    def chunk_body(ci, carry):
        c = (TS // CHUNK - 1 - ci) if reverse else ci
        rows = pl.ds(pl.multiple_of(c * CHUNK, CHUNK), CHUNK)
        g1, g2, g3 = _split3(g_scr[rows, :])
        gc = _dot(tri, g1) + _dot(tri, g2) + _dot(tri, g3)
        e_last = CHUNK - 1 if not reverse else 0
        qt, kt, vt = q_ref[rows, :], k_ref[rows, :], v_ref[rows, :]
        for h in range(GLA_HEADS):
            kl = slice(h * GLA_DK, (h + 1) * GLA_DK)
            vl = slice(h * GLA_DV, (h + 1) * GLA_DV)
            gch = gc[:, kl]
            glast = gch[e_last:e_last + 1, :]
            qc = qt[:, kl] * (GLA_DK ** -0.5)
            kc = kt[:, kl]
            vcb = vt[:, vl].astype(BF16)
            qg = (qc * jnp.exp(gch)).astype(BF16)
            kg = (kc * jnp.exp(-gch)).astype(BF16)
            a = jnp.where(incl, _dot_tb(qg, kg), 0.0)
            kd = (kc * jnp.exp(glast - gch)).astype(BF16)
            s_old = s_scr[h]
            o_scr[rows, vl] = _dot_tb(qg, s_old.astype(BF16)) + _dot(a.astype(BF16), vcb)
            s_scr[h] = jnp.exp(glast) * s_old + _dot_ta(vcb, kd)
        return carry

    lax.fori_loop(0, TS // CHUNK, chunk_body, 0)

    if has_prev:
        o_ref[...] = _gated_rms_out(o_scr[...] + oprev_ref[...], r_ref[...], nw_ref[...]).astype(o_ref.dtype)
    else:
        o_ref[...] = o_scr[...]

    @pl.when(is_p)
    def _():
        st_ref[...] = s_scr[...]


def _gla_call(pm, reverse, w2pad, b2, s0t, prev=None, norm_w=None):
    tmap = (lambda i: NT - 1 - i) if reverse else (lambda i: i)
    qkw = GLA_HEADS * GLA_DK
    vw = GLA_HEADS * GLA_DV
    in_specs = [
        pl.BlockSpec((TS, qkw), lambda i: (tmap(i), C_GQ // qkw)),
        pl.BlockSpec((TS, qkw), lambda i: (tmap(i), C_GK // qkw)),
        pl.BlockSpec((TS, vw), lambda i: (tmap(i), C_GV // vw)),
        pl.BlockSpec((TS, 128), lambda i: (tmap(i), C_SM // 128)),
        pl.BlockSpec((128, qkw), lambda i: (0, 0)),
        pl.BlockSpec((1, qkw), lambda i: (0, 0)),
        pl.BlockSpec((None, GLA_HEADS, GLA_DV, GLA_DK), lambda i: (_seq_of_tile(tmap(i)), 0, 0, 0)),
    ]
    args = [pm, pm, pm, pm, w2pad, b2, s0t]
    has_prev = prev is not None
    if has_prev:
        in_specs += [pl.BlockSpec((TS, vw), lambda i: (tmap(i), 0)),
                     pl.BlockSpec((TS, vw), lambda i: (tmap(i), C_GR // vw)),
                     pl.BlockSpec((1, 128), lambda i: (0, 0))]
        args += [prev, pm, norm_w.reshape(1, 128)]
    out_dtype = BF16 if has_prev else F32
    return pl.pallas_call(
        functools.partial(_gla_kernel, reverse, has_prev),
        grid=(NT,),
        in_specs=in_specs,
        out_specs=[pl.BlockSpec((TS, vw), lambda i: (tmap(i), 0)),
                   pl.BlockSpec((None, GLA_HEADS, GLA_DV, GLA_DK), lambda i: (jnp.minimum(tmap(i), NT_P - 1), 0, 0, 0))],
        out_shape=[jax.ShapeDtypeStruct((M, vw), out_dtype),
                   jax.ShapeDtypeStruct((BATCH, GLA_HEADS, GLA_DV, GLA_DK), F32)],
        scratch_shapes=[pltpu.VMEM((GLA_HEADS, GLA_DV, GLA_DK), F32),
                        pltpu.VMEM((TS, qkw), F32),
                        pltpu.VMEM((TS, vw), F32)],
        compiler_params=_params(("arbitrary",)),
        name="gla_bwd" if reverse else "gla_fwd",
    )(*args)


def _pack_mix_weights(w_in_l):
    o = np.cumsum((0, 512, 1024, 256, 256, 1536, 512, 8, 8, 256, 256, 512, 512, 32)).tolist()
    sl = lambda i: w_in_l[:, o[i]:o[i + 1]]
    small = jnp.concatenate([sl(6), sl(7), sl(12), jnp.zeros((D, 128 - 48), w_in_l.dtype)], axis=1)
    mix = jnp.concatenate([sl(4), sl(0), sl(1), sl(5), sl(10), sl(11), sl(2), sl(3), sl(8), sl(9), small], axis=1)
    return mix.astype(BF16), w_in_l[:, o[13]:].astype(BF16)


def kernel(x_prompt, x_sample, cache_k, cache_v, state_delta, state_gla, c, c_ctx, w_mod, b_mod, norm1, w_in, pool_w, pool_scale, attn_sink, dn_conv, dn_a_log, dn_dt_bias, dn_norm, gla_w2, gla_b2, gla_norm, w_br_pool, w_br_attn, w_br_delta, w_br_gla, w_out, norm2, w_gate, w_up, w_down, norm_f):
    x = jnp.concatenate([x_prompt.reshape(MP, D), x_sample.reshape(MS, D)], axis=0)
    cond8 = jnp.concatenate([c_ctx[None, :], c, jnp.zeros((8 - 1 - DEC_BATCH, D), F32)], axis=0)
    mod = _modulation(cond8, w_mod, b_mod)
    cos_t, sin_t = _rope_tables()
    s_gla_t = jnp.swapaxes(state_gla, -1, -2)

    ks, vs, sds, sgs = [], [], [], []
    for l in range(DEPTH):
        sh1, sc1, gt1, sh2, sc2, gt2 = [mod[l, :, i * D:(i + 1) * D].reshape(8, 1, D) for i in range(N_MOD)]
        w_mix, w_gates = _pack_mix_weights(w_in[l])
        bf = lambda w: w.astype(BF16)

        h = _prenorm(x, norm1[l], sc1, sh1)
        pm = _mm(h, w_mix, 1024, 1152, F32, "mix_proj")

        ya = _pool(pm, pool_w[l], pool_scale[l])
        yb = jnp.concatenate([_ctx_attn(pm, attn_sink[l]),
                              _lat_attn(pm, attn_sink[l], cos_t, sin_t, cache_k, cache_v, l)], axis=0)

        alog = jnp.zeros((1, 128), F32).at[0, SM_A:SM_A + 8].set(dn_a_log[l].reshape(8))
        dtb = jnp.zeros((1, 128), F32).at[0, SM_A:SM_A + 8].set(dn_dt_bias[l].reshape(8))
        o_b, sd_b = _dn_call(pm, True, l, dn_conv[l], alog, dtb, state_delta)
        yc, sd_f = _dn_call(pm, False, l, dn_conv[l], alog, dtb, state_delta, prev=o_b, norm_w=dn_norm[l])

        w2pad = [jnp.zeros((128, GLA_HEADS * GLA_DK), F32).at[SM_LR + o * GLA_RANK:SM_LR + (o + 1) * GLA_RANK].set(gla_w2[l, o]).astype(BF16)
                 for o in range(2)]
        g_b, sg_b = _gla_call(pm, True, w2pad[1], gla_b2[l, 1].reshape(1, -1), s_gla_t[:, l, 1])
        yd, sg_f = _gla_call(pm, False, w2pad[0], gla_b2[l, 0].reshape(1, -1), s_gla_t[:, l, 0], prev=g_b, norm_w=gla_norm[l])

        merged = _merge(h, w_gates, ya, yb, yc, yd, bf(w_br_pool[l]), bf(w_br_attn[l]), bf(w_br_delta[l]), bf(w_br_gla[l]))
        x = _mm_res(merged, bf(w_out[l]), x, gt1, 1024, 512, "out_proj")

        h2 = _prenorm(x, norm2[l], sc2, sh2)
        act = _ffn_up(h2, bf(w_gate[l]), bf(w_up[l]))
        x = _mm_res(act, bf(w_down[l]), x, gt2, 512, 512, "ffn_down")

        ks.append(pm[:MP, C_K:C_K + KV_HEADS * HD].reshape(BATCH, SEQ, KV_HEADS, HD))
        vs.append(pm[:MP, C_V:C_V + KV_HEADS * HD].reshape(BATCH, SEQ, KV_HEADS, HD))
        sds.append(jnp.stack([sd_f, sd_b], axis=1))
        sgs.append(jnp.swapaxes(jnp.stack([sg_f, sg_b], axis=1), -1, -2))

    y = _finalnorm(x, norm_f)
    return (y[:MP].reshape(BATCH, SEQ, D), y[MP:].reshape(DEC_BATCH, DEC_SEQ, D),
            jnp.stack(ks, axis=1), jnp.stack(vs, axis=1), jnp.stack(sds, axis=1), jnp.stack(sgs, axis=1))
```

```python
import functools

import numpy as np
import jax
import jax.numpy as jnp
from jax import lax
from jax.experimental import pallas as pl
from jax.experimental.pallas import tpu as pltpu

F32 = jnp.float32
BF16 = jnp.bfloat16

D = 2048
BATCH = 16
SEQ = 256
DEPTH = 4
DEC_BATCH = 4
DEC_SEQ = 2048
PAST = 256
GRID_W = 64
HD = 128
ROPE_THETA = 10000.0
EPS = 1e-6
N_MOD = 6
POOL_WINDOWS = (2, 4, 8, 16)
POOL_W = 512
Q_HEADS = 8
KV_HEADS = 2
GROUP = 4
ATTN_WINDOW = 128
DN_HEADS = 4
DN_DK = 128
DN_W = 512
DN_CONV = 4
GLA_HEADS = 4
GLA_DK = 64
GLA_DV = 128
GLA_RANK = 16
GLA_NORMALIZER = 16.0
FFN = 5632
CHUNK = 64

MP = BATCH * SEQ
MS = DEC_BATCH * DEC_SEQ
M = MP + MS
TS = 256
NT_P = MP // TS
TPS = DEC_SEQ // TS
NT_S = MS // TS
NT = NT_P + NT_S
HALO = 8

C_DNQKV, C_POOL, C_Q, C_DNZ, C_GV, C_GR = 0, 1536, 2048, 3072, 3584, 4096
C_K, C_V, C_GQ, C_GK, C_SM = 4608, 4864, 5120, 5376, 5632
MIXW = 5760
SM_A, SM_B, SM_LR = 0, 8, 16

VMEM_LIMIT = 56 * 1024 * 1024


def _params(sem):
    return pltpu.CompilerParams(dimension_semantics=sem, vmem_limit_bytes=VMEM_LIMIT)


def _cond_row(row0):
    return jnp.where(row0 < MP, 0, 1 + (row0 - MP) // DEC_SEQ)


def _tile_info(t):
    is_p = t < NT_P
    s = jnp.maximum(t - NT_P, 0)
    j = s % TPS
    first = jnp.logical_or(is_p, j == 0)
    last = jnp.logical_or(is_p, j == TPS - 1)
    base = jnp.where(is_p, 0, j * TS)
    seq_len = jnp.where(is_p, SEQ, DEC_SEQ)
    return is_p, first, last, base, seq_len


def _silu(x):
    return x * jax.nn.sigmoid(x)


def _softplus(x):
    return jnp.maximum(x, 0.0) + jnp.log(1.0 + jnp.exp(-jnp.abs(x)))


def _dot(a, b):
    return jnp.dot(a, b, preferred_element_type=F32)


def _dot_tb(a, b):
    return lax.dot_general(a, b, (((1,), (1,)), ((), ())), preferred_element_type=F32)


def _dot_ta(a, b):
    return lax.dot_general(a, b, (((0,), (0,)), ((), ())), preferred_element_type=F32)


def _split2(x):
    hi = x.astype(BF16)
    lo = (x - hi.astype(F32)).astype(BF16)
    return hi, lo


def _split3(x):
    p1 = x.astype(BF16)
    r = x - p1.astype(F32)
    p2 = r.astype(BF16)
    p3 = (r - p2.astype(F32)).astype(BF16)
    return p1, p2, p3


def _dot_hp(a, b):
    ah, al = _split2(a)
    bh, bl = _split2(b)
    return _dot(ah, bh) + (_dot(ah, bl) + _dot(al, bh))


def _mod_kernel(c_ref, w_ref, b_ref, o_ref):
    a = _silu(c_ref[...]).astype(BF16)
    o_ref[...] = _dot(a, w_ref[...].astype(BF16)) + b_ref[...]


def _modulation(cond8, w_mod, b_mod):
    tn = 1024
    n = N_MOD * D
    return pl.pallas_call(
        _mod_kernel,
        grid=(DEPTH, n // tn),
        in_specs=[pl.BlockSpec((8, D), lambda l, j: (0, 0)),
                  pl.BlockSpec((None, D, tn), lambda l, j: (l, 0, j)),
                  pl.BlockSpec((None, 1, tn), lambda l, j: (l, 0, j))],
        out_specs=pl.BlockSpec((None, 8, tn), lambda l, j: (l, 0, j)),
        out_shape=jax.ShapeDtypeStruct((DEPTH, 8, n), F32),
        compiler_params=_params(("parallel", "parallel")),
        name="modulation",
    )(cond8, w_mod, b_mod.reshape(DEPTH, 1, n))


def _rms_rows(x, w):
    ms = jnp.mean(x * x, axis=-1, keepdims=True)
    return x * lax.rsqrt(ms + EPS) * w


def _prenorm_kernel(x_ref, nw_ref, sc_ref, sh_ref, h_ref):
    y = _rms_rows(x_ref[...], nw_ref[...])
    h_ref[...] = (y * (1.0 + sc_ref[...]) + sh_ref[...]).astype(h_ref.dtype)


def _prenorm(x, nw, sc, sh):
    tm = 512
    cond = pl.BlockSpec((None, 1, D), lambda i: (_cond_row(i * tm), 0, 0))
    return pl.pallas_call(
        _prenorm_kernel,
        grid=(M // tm,),
        in_specs=[pl.BlockSpec((tm, D), lambda i: (i, 0)),
                  pl.BlockSpec((1, D), lambda i: (0, 0)), cond, cond],
        out_specs=pl.BlockSpec((tm, D), lambda i: (i, 0)),
        out_shape=jax.ShapeDtypeStruct((M, D), BF16),
        compiler_params=_params(("parallel",)),
        name="prenorm",
    )(x, nw.reshape(1, D), sc, sh)


def _finalnorm_kernel(x_ref, nw_ref, o_ref):
    o_ref[...] = _rms_rows(x_ref[...], nw_ref[...])


def _finalnorm(x, nw):
    tm = 512
    return pl.pallas_call(
        _finalnorm_kernel,
        grid=(M // tm,),
        in_specs=[pl.BlockSpec((tm, D), lambda i: (i, 0)),
                  pl.BlockSpec((1, D), lambda i: (0, 0))],
        out_specs=pl.BlockSpec((tm, D), lambda i: (i, 0)),
        out_shape=jax.ShapeDtypeStruct((M, D), F32),
        compiler_params=_params(("parallel",)),
        name="finalnorm",
    )(x, nw.reshape(1, D))


def _mm_kernel(a_ref, w_ref, o_ref):
    o_ref[...] = _dot(a_ref[...], w_ref[...]).astype(o_ref.dtype)


def _mm(a, w, tm, tn, out_dtype, name):
    m, k = a.shape
    n = w.shape[1]
    return pl.pallas_call(
        _mm_kernel,
        grid=(m // tm, n // tn),
        in_specs=[pl.BlockSpec((tm, k), lambda i, j: (i, 0)),
                  pl.BlockSpec((k, tn), lambda i, j: (0, j))],
        out_specs=pl.BlockSpec((tm, tn), lambda i, j: (i, j)),
        out_shape=jax.ShapeDtypeStruct((m, n), out_dtype),
        compiler_params=_params(("parallel", "arbitrary")),
        name=name,
    )(a, w)


def _mm_res_kernel(a_ref, w_ref, x_ref, gt_ref, o_ref):
    o_ref[...] = x_ref[...] + gt_ref[...] * _dot(a_ref[...], w_ref[...])


def _mm_res(a, w, x, gt, tm, tn, name):
    m, k = a.shape
    n = w.shape[1]
    return pl.pallas_call(
        _mm_res_kernel,
        grid=(m // tm, n // tn),
        in_specs=[pl.BlockSpec((tm, k), lambda i, j: (i, 0)),
                  pl.BlockSpec((k, tn), lambda i, j: (0, j)),
                  pl.BlockSpec((tm, tn), lambda i, j: (i, j)),
                  pl.BlockSpec((None, 1, tn), lambda i, j: (_cond_row(i * tm), 0, j))],
        out_specs=pl.BlockSpec((tm, tn), lambda i, j: (i, j)),
        out_shape=jax.ShapeDtypeStruct((m, n), F32),
        compiler_params=_params(("parallel", "arbitrary")),
        name=name,
    )(a, w, x, gt)


def _ffn_up_kernel(h_ref, wg_ref, wu_ref, o_ref):
    h = h_ref[...]
    o_ref[...] = (_silu(_dot(h, wg_ref[...])) * _dot(h, wu_ref[...])).astype(o_ref.dtype)


def _ffn_up(h, wg, wu):
    tm, tn = 1024, 512
    return pl.pallas_call(
        _ffn_up_kernel,
        grid=(M // tm, FFN // tn),
        in_specs=[pl.BlockSpec((tm, D), lambda i, j: (i, 0)),
                  pl.BlockSpec((D, tn), lambda i, j: (0, j)),
                  pl.BlockSpec((D, tn), lambda i, j: (0, j))],
        out_specs=pl.BlockSpec((tm, tn), lambda i, j: (i, j)),
        out_shape=jax.ShapeDtypeStruct((M, FFN), BF16),
        compiler_params=_params(("parallel", "arbitrary")),
        name="ffn_up",
    )(h, wg, wu)


def _merge_kernel(h_ref, g0_ref, g1_ref, g2_ref, g3_ref, ya_ref, yb_ref, yc_ref, yd_ref,
                  wa_ref, wb_ref, wc_ref, wd_ref, o_ref):
    h = h_ref[...]
    acc = None
    for g_ref, y_ref, w_ref in ((g0_ref, ya_ref, wa_ref), (g1_ref, yb_ref, wb_ref),
                                (g2_ref, yc_ref, wc_ref), (g3_ref, yd_ref, wd_ref)):
        term = jax.nn.sigmoid(_dot(h, g_ref[...])) * _dot(y_ref[...], w_ref[...])
        acc = term if acc is None else acc + term
    o_ref[...] = acc.astype(o_ref.dtype)


def _merge(h, wgate, ya, yb, yc, yd, wa, wb, wc, wd):
    tm, tn = 1024, 512
    nj = D // tn
    gate_specs = [pl.BlockSpec((D, tn), functools.partial(lambda i, j, b: (0, b * nj + j), b=b)) for b in range(4)]
    row = lambda w: pl.BlockSpec((tm, w), lambda i, j: (i, 0))
    col = lambda k: pl.BlockSpec((k, tn), lambda i, j: (0, j))
    return pl.pallas_call(
        _merge_kernel,
        grid=(M // tm, nj),
        in_specs=[row(D)] + gate_specs + [row(POOL_W), row(Q_HEADS * HD), row(DN_W), row(GLA_HEADS * GLA_DV),
                                          col(POOL_W), col(Q_HEADS * HD), col(DN_W), col(GLA_HEADS * GLA_DV)],
        out_specs=pl.BlockSpec((tm, tn), lambda i, j: (i, j)),
        out_shape=jax.ShapeDtypeStruct((M, D), BF16),
        compiler_params=_params(("parallel", "arbitrary")),
        name="merge",
    )(h, wgate, wgate, wgate, wgate, ya, yb, yc, yd, wa, wb, wc, wd)


def _pool_kernel(u_ref, up_ref, un_ref, w_ref, sc_ref, o_ref):
    t = pl.program_id(0)
    _, first, last, base, seq_len = _tile_info(t)
    u = u_ref[...]
    prev = jnp.where(first, 0.0, up_ref[...])
    nxt = jnp.where(last, 0.0, un_ref[...])
    ext = jnp.concatenate([prev, u, nxt], axis=0)
    pos = base + lax.broadcasted_iota(jnp.int32, (TS, 1), 0)
    outs = []
    for gi, win in enumerate(POOL_WINDOWS):
        lanes = slice(gi * 128, (gi + 1) * 128)
        p = ext[:, lanes]
        s = 1
        while s < win:
            p = p + pltpu.roll(p, s, 0)
            s *= 2
        d = HALO + win // 2 - 1
        ssum = p[d:d + TS]
        cnt = (jnp.minimum(pos + win // 2, seq_len) - jnp.maximum(pos - win // 2, 0)).astype(F32)
        pg = ssum / cnt - u[:, lanes]
        outs.append(_dot(pg.astype(BF16), w_ref[gi].astype(BF16)))
    o_ref[...] = (jnp.concatenate(outs, axis=1) * sc_ref[...]).astype(o_ref.dtype)


def _halo_specs(width, colblk):
    r = TS // HALO
    nblk = M // HALO
    prev = pl.BlockSpec((HALO, width), lambda t: (jnp.maximum(t * r - 1, 0), colblk))
    nxt = pl.BlockSpec((HALO, width), lambda t: (jnp.minimum((t + 1) * r, nblk - 1), colblk))
    return prev, nxt


def _pool(pm, pool_w, pool_scale):
    cb = C_POOL // POOL_W
    prev, nxt = _halo_specs(POOL_W, cb)
    return pl.pallas_call(
        _pool_kernel,
        grid=(NT,),
        in_specs=[pl.BlockSpec((TS, POOL_W), lambda t: (t, cb)), prev, nxt,
                  pl.BlockSpec((4, 128, 128), lambda t: (0, 0, 0)),
                  pl.BlockSpec((1, POOL_W), lambda t: (0, 0))],
        out_specs=pl.BlockSpec((TS, POOL_W), lambda t: (t, 0)),
        out_shape=jax.ShapeDtypeStruct((M, POOL_W), BF16),
        compiler_params=_params(("parallel",)),
        name="pool",
    )(pm, pm, pm, pool_w, pool_scale.reshape(1, POOL_W))


def _ctx_attn_kernel(sink_ref, q_ref, k_ref, v_ref, o_ref):
    q = q_ref[...] * (HD ** -0.5)
    k = k_ref[...].astype(BF16)
    v = v_ref[...].astype(BF16)
    for n in range(KV_HEADS):
        kn = k[:, n * HD:(n + 1) * HD]
        vn = v[:, n * HD:(n + 1) * HD]
        for g in range(GROUP):
            h = n * GROUP + g
            qh = q[:, h * HD:(h + 1) * HD].astype(BF16)
            s = _dot_tb(qh, kn)
            snk = sink_ref[h]
            m = jnp.maximum(jnp.max(s, axis=-1, keepdims=True), snk)
            e = jnp.exp(s - m)
            den = jnp.sum(e, axis=-1, keepdims=True) + jnp.exp(snk - m)
            o = _dot(e.astype(BF16), vn) / den
            o_ref[:, h * HD:(h + 1) * HD] = o.astype(o_ref.dtype)


def _ctx_attn(pm, sink):
    qw, kw = Q_HEADS * HD, KV_HEADS * HD
    return pl.pallas_call(
        _ctx_attn_kernel,
        grid=(NT_P,),
        in_specs=[pl.BlockSpec(memory_space=pltpu.SMEM),
                  pl.BlockSpec((TS, qw), lambda t: (t, C_Q // qw)),
                  pl.BlockSpec((TS, kw), lambda t: (t, C_K // kw)),
                  pl.BlockSpec((TS, kw), lambda t: (t, C_V // kw))],
        out_specs=pl.BlockSpec((TS, qw), lambda t: (t, 0)),
        out_shape=jax.ShapeDtypeStruct((MP, qw), BF16),
        compiler_params=_params(("parallel",)),
        name="ctx_attn",
    )(sink, pm, pm, pm)


def _rope(x, cs, sn):
    lane = lax.broadcasted_iota(jnp.int32, x.shape, 1)
    swapped = jnp.where((lane % 64) < 32, pltpu.roll(x, 96, 1), pltpu.roll(x, 32, 1))
    return x * cs + swapped * sn


def _lat_attn_kernel(sink_ref, q_ref, km_ref, kp_ref, kn_ref, vm_ref, vp_ref, vn_ref,
                     cm_ref, cp_ref, cn_ref, sm_ref, sp_ref, sn_ref, kc_ref, vc_ref, o_ref):
    ts = pl.program_id(0)
    base = (ts % TPS) * TS
    half = TS // 2
    nk = TS + 2 * half
    q = q_ref[...] * (HD ** -0.5)
    cq, sq = cm_ref[...], sm_ref[...]
    kl = jnp.concatenate([kp_ref[...], km_ref[...], kn_ref[...]], axis=0)
    vl = jnp.concatenate([vp_ref[...], vm_ref[...], vn_ref[...]], axis=0).astype(BF16)
    ck = jnp.concatenate([cp_ref[...], cq, cn_ref[...]], axis=0)
    sk = jnp.concatenate([sp_ref[...], sq, sn_ref[...]], axis=0)
    kc = kc_ref[...].astype(BF16)
    vc = vc_ref[...].astype(BF16)
    qpos = base + lax.broadcasted_iota(jnp.int32, (TS, 1), 0)
    kpos = base - half + lax.broadcasted_iota(jnp.int32, (1, nk), 1)
    dist = qpos - kpos
    ok = (jnp.abs(dist) <= ATTN_WINDOW).astype(jnp.int32) * (kpos >= 0).astype(jnp.int32) * (kpos < DEC_SEQ).astype(jnp.int32)
    bias = jnp.where(ok > 0, 0.0, -1e30)
    for n in range(KV_HEADS):
        lanes = slice(n * HD, (n + 1) * HD)
        kr = _rope(kl[:, lanes], ck, sk).astype(BF16)
        vln, kcn, vcn = vl[:, lanes], kc[:, lanes], vc[:, lanes]
        for g in range(GROUP):
            h = n * GROUP + g
            qh = _rope(q[:, h * HD:(h + 1) * HD], cq, sq).astype(BF16)
            s_c = _dot_tb(qh, kcn)
            s_l = _dot_tb(qh, kr) + bias
            snk = sink_ref[h]
            m = jnp.maximum(jnp.maximum(jnp.max(s_c, axis=-1, keepdims=True),
                                        jnp.max(s_l, axis=-1, keepdims=True)), snk)
            e_c = jnp.exp(s_c - m)
            e_l = jnp.exp(s_l - m)
            den = jnp.sum(e_c, axis=-1, keepdims=True) + jnp.sum(e_l, axis=-1, keepdims=True) + jnp.exp(snk - m)
            o = (_dot(e_c.astype(BF16), vcn) + _dot(e_l.astype(BF16), vln)) / den
            o_ref[:, h * HD:(h + 1) * HD] = o.astype(o_ref.dtype)


def _lat_attn(pm, sink, cos_t, sin_t, cache_k, cache_v, layer):
    qw, kw = Q_HEADS * HD, KV_HEADS * HD
    half = TS // 2
    nhb = M // half
    nhs = DEC_SEQ // half

    def main(w, cb):
        return pl.BlockSpec((TS, w), lambda ts: (NT_P + ts, cb))

    def prev(cb):
        return pl.BlockSpec((half, kw), lambda ts: (2 * (NT_P + ts) - 1, cb))

    def nxt(cb):
        return pl.BlockSpec((half, kw), lambda ts: (jnp.minimum(2 * (NT_P + ts) + 2, nhb - 1), cb))

    tab_m = pl.BlockSpec((TS, HD), lambda ts: (ts % TPS, 0))
    tab_p = pl.BlockSpec((half, HD), lambda ts: (jnp.maximum(2 * (ts % TPS) - 1, 0), 0))
    tab_n = pl.BlockSpec((half, HD), lambda ts: (jnp.minimum(2 * (ts % TPS) + 2, nhs - 1), 0))
    ctx = pl.BlockSpec((None, None, PAST, kw), lambda ts: (ts // TPS, layer, 0, 0))
    return pl.pallas_call(
        _lat_attn_kernel,
        grid=(NT_S,),
        in_specs=[pl.BlockSpec(memory_space=pltpu.SMEM),
                  main(qw, C_Q // qw),
                  main(kw, C_K // kw), prev(C_K // kw), nxt(C_K // kw),
                  main(kw, C_V // kw), prev(C_V // kw), nxt(C_V // kw),
                  tab_m, tab_p, tab_n, tab_m, tab_p, tab_n, ctx, ctx],
        out_specs=pl.BlockSpec((TS, qw), lambda ts: (ts, 0)),
        out_shape=jax.ShapeDtypeStruct((MS, qw), BF16),
        compiler_params=_params(("parallel",)),
        name="lat_attn",
    )(sink, pm, pm, pm, pm, pm, pm, pm, cos_t, cos_t, cos_t, sin_t, sin_t, sin_t,
      cache_k.reshape(DEC_BATCH, DEPTH, PAST, kw), cache_v.reshape(DEC_BATCH, DEPTH, PAST, kw))


def _rope_tables():
    quarter = HD // 4
    inv = ROPE_THETA ** (-jnp.arange(quarter, dtype=F32) / quarter)
    t = jnp.arange(DEC_SEQ)
    row = (t // GRID_W).astype(F32)
    col = (t % GRID_W).astype(F32)
    ar = row[:, None] * inv[None, :]
    ac = col[:, None] * inv[None, :]
    cos_t = jnp.concatenate([jnp.cos(ar), jnp.cos(ar), jnp.cos(ac), jnp.cos(ac)], axis=-1)
    sin_t = jnp.concatenate([-jnp.sin(ar), jnp.sin(ar), -jnp.sin(ac), jnp.sin(ac)], axis=-1)
    return cos_t, sin_t


def _scan_tile(reverse):
    i = pl.program_id(0)
    return (NT - 1 - i) if reverse else i


def _tri_masks(reverse):
    ii = lax.broadcasted_iota(jnp.int32, (CHUNK, CHUNK), 0)
    jj = lax.broadcasted_iota(jnp.int32, (CHUNK, CHUNK), 1)
    incl = (jj >= ii) if reverse else (jj <= ii)
    strict = (jj > ii) if reverse else (jj < ii)
    return incl, strict


def _gated_rms_out(o, gate, nw):
    outs = []
    for h in range(4):
        lanes = slice(h * 128, (h + 1) * 128)
        outs.append(_rms_rows(o[:, lanes], nw) * _silu(gate[:, lanes]))
    return jnp.concatenate(outs, axis=1)


def _dot_hp_all(xs, ys):
    hh = [_dot(x[0], y[0]) for x, y in zip(xs, ys)]
    hl = [_dot(x[0], y[1]) for x, y in zip(xs, ys)]
    lh = [_dot(x[1], y[0]) for x, y in zip(xs, ys)]
    return [a + (b + c) for a, b, c in zip(hh, hl, lh)]


def _inv_unit_tri_all(lms):
    ii = lax.broadcasted_iota(jnp.int32, (CHUNK, CHUNK), 0)
    jj = lax.broadcasted_iota(jnp.int32, (CHUNK, CHUNK), 1)
    eye = jnp.where(ii == jj, 1.0, 0.0)
    n = len(lms)
    ps = [eye - lm for lm in lms]
    ls = [_split2(lm) for lm in lms]
    mks = _dot_hp_all(ls, ls)
    span = 4
    while span < CHUNK:
        ms = [_split2(m) for m in mks]
        both = _dot_hp_all([_split2(p) for p in ps] + ms, ms + ms)
        ps = [p + pm for p, pm in zip(ps, both[:n])]
        mks = both[n:]
        span *= 2
    last = _dot_hp_all([_split2(p) for p in ps], [_split2(m) for m in mks])
    return [p + pm for p, pm in zip(ps, last)]


def _dn_kernel(reverse, has_prev, x_ref, xp_ref, xn_ref, sm_ref, cw_ref, alog_ref, dtb_ref, s0_ref, *rest):
    if has_prev:
        oprev_ref, z_ref, nw_ref = rest[:3]
        rest = rest[3:]
    (o_ref, st_ref, s_scr, q_scr, k_scr, v_scr, b_scr, gcc_scr, gcr_scr,
     u_scr, w_scr, qe_scr, kd_scr, a_scr, o_scr) = rest
    t = _scan_tile(reverse)
    is_p, first, last, _, _ = _tile_info(t)
    d = 1 if reverse else 0
    nchunk = TS // CHUNK
    e_last = 0 if reverse else CHUNK - 1

    @pl.when(last if reverse else first)
    def _():
        s_scr[...] = jnp.where(is_p, 0.0, s0_ref[...])

    x = x_ref[...]
    ext = jnp.concatenate([jnp.where(first, 0.0, xp_ref[...]), x, jnp.where(last, 0.0, xn_ref[...])], axis=0)
    cw = cw_ref[...]
    conv = None
    for j in range(DN_CONV):
        r0 = HALO + j - DN_CONV // 2
        term = ext[r0:r0 + TS] * cw[j:j + 1, :]
        conv = term if conv is None else conv + term
    act = _silu(conv)
    for h in range(DN_HEADS):
        lanes = slice(h * 128, (h + 1) * 128)
        qh = act[:, h * 128:(h + 1) * 128]
        kh = act[:, DN_W + h * 128:DN_W + (h + 1) * 128]
        q_scr[:, lanes] = qh * lax.rsqrt(jnp.sum(qh * qh, axis=-1, keepdims=True) + EPS) * (DN_DK ** -0.5)
        k_scr[:, lanes] = kh * lax.rsqrt(jnp.sum(kh * kh, axis=-1, keepdims=True) + EPS)
    v_scr[...] = act[:, 2 * DN_W:3 * DN_W]
    sm = sm_ref[...]
    b_scr[...] = jax.nn.sigmoid(sm)
    g = -jnp.exp(alog_ref[...]) * _softplus(sm + dtb_ref[...])

    ti = lax.broadcasted_iota(jnp.int32, (TS, TS), 0)
    tj = lax.broadcasted_iota(jnp.int32, (TS, TS), 1)
    same = (ti // CHUNK) == (tj // CHUNK)
    lower = jnp.where(jnp.logical_and(same, tj <= ti), 1.0, 0.0).astype(BF16)
    upper = jnp.where(jnp.logical_and(same, tj >= ti), 1.0, 0.0).astype(BF16)
    tri, tri_t = (upper, lower) if reverse else (lower, upper)
    g1, g2, g3 = _split3(g)
    gcc_scr[...] = _dot(tri, g1) + _dot(tri, g2) + _dot(tri, g3)
    gcr = _dot_ta(g1, tri_t) + _dot_ta(g2, tri_t) + _dot_ta(g3, tri_t)
    for c in range(nchunk):
        gcr_scr[c] = gcr[:, c * CHUNK:(c + 1) * CHUNK]

    incl, strict = _tri_masks(reverse)

    def prep_body(pi, carry):
        rows, gblk, gcr_c, bt = {}, {}, {}, {}
        for cc in range(2):
            c = pi * 2 + cc
            rows[cc] = pl.ds(pl.multiple_of(c * CHUNK, CHUNK), CHUNK)
            gblk[cc] = gcc_scr[rows[cc], :]
            gcr_c[cc] = gcr_scr[c]
            bt[cc] = b_scr[rows[cc], :]
        info = []
        for cc in range(2):
            for h in range(DN_HEADS):
                lanes = slice(h * 128, (h + 1) * 128)
                r = SM_A + d * DN_HEADS + h
                gcol = gblk[cc][:, r:r + 1]
                grow = gcr_c[cc][r:r + 1, :]
                glast = gblk[cc][e_last:e_last + 1, r:r + 1]
                beta = bt[cc][:, SM_B + d * DN_HEADS + h:SM_B + d * DN_HEADS + h + 1]
                qc, kc, vc = q_scr[rows[cc], lanes], k_scr[rows[cc], lanes], v_scr[rows[cc], lanes]
                decay = jnp.exp(jnp.where(incl, gcol - grow, -1e30))
                kb = kc * beta
                eg = jnp.exp(gcol)
                qe_scr[rows[cc], lanes] = (qc * eg).astype(BF16)
                kd_scr[rows[cc], lanes] = (kc * jnp.exp(glast - gcol)).astype(BF16)
                info.append(dict(cc=cc, h=h, lanes=lanes, decay=decay, kbb=kb.astype(BF16), kcb=kc.astype(BF16),
                                 qcb=qc.astype(BF16), vbb=(vc * beta).astype(BF16), kgb=(kb * eg).astype(BF16)))
        kk = [_dot_tb(p["kbb"], p["kcb"]) for p in info]
        qk = [_dot_tb(p["qcb"], p["kcb"]) for p in info]
        for p, m in zip(info, qk):
            a_scr[rows[p["cc"]], p["h"] * CHUNK:(p["h"] + 1) * CHUNK] = (m * p["decay"]).astype(BF16)
        lms = [jnp.where(strict, m * p["decay"], 0.0) for p, m in zip(info, kk)]
        tinvs = [tv.astype(BF16) for tv in _inv_unit_tri_all(lms)]
        us = [_dot(tv, p["vbb"]) for tv, p in zip(tinvs, info)]
        ws = [_dot(tv, p["kgb"]) for tv, p in zip(tinvs, info)]
        for p, u, w in zip(info, us, ws):
            u_scr[rows[p["cc"]], p["lanes"]] = u
            w_scr[rows[p["cc"]], p["lanes"]] = w.astype(BF16)
        return carry

    lax.fori_loop(0, nchunk // 2, prep_body, 0)

    def chunk_body(ci, carry):
        c = (nchunk - 1 - ci) if reverse else ci
        rows = pl.ds(pl.multiple_of(c * CHUNK, CHUNK), CHUNK)
        glast_row = gcc_scr[pl.ds(c * CHUNK + e_last, 1), :]
        heads = range(DN_HEADS)
        lanes = [slice(h * 128, (h + 1) * 128) for h in heads]
        s_old = [s_scr[h] for h in heads]
        sb = [s.astype(BF16) for s in s_old]
        ws = [_dot(w_scr[rows, lanes[h]], sb[h]) for h in heads]
        qs = [_dot(qe_scr[rows, lanes[h]], sb[h]) for h in heads]
        vnb = [(u_scr[rows, lanes[h]] - ws[h]).astype(BF16) for h in heads]
        av = [_dot(a_scr[rows, h * CHUNK:(h + 1) * CHUNK], vnb[h]) for h in heads]
        up = [_dot_ta(kd_scr[rows, lanes[h]], vnb[h]) for h in heads]
        for h in heads:
            r = SM_A + d * DN_HEADS + h
            o_scr[rows, lanes[h]] = qs[h] + av[h]
            s_scr[h] = jnp.exp(glast_row[:, r:r + 1]) * s_old[h] + up[h]
        return carry

    lax.fori_loop(0, nchunk, chunk_body, 0)

    if has_prev:
        o_ref[...] = _gated_rms_out(o_scr[...] + oprev_ref[...], z_ref[...], nw_ref[...]).astype(o_ref.dtype)
    else:
        o_ref[...] = o_scr[...]

    @pl.when(is_p)
    def _():
        st_ref[...] = s_scr[...]


def _seq_of_tile(t):
    return jnp.maximum(t - NT_P, 0) // TPS


def _dn_call(pm, reverse, layer, conv_w, alog, dtb, state_delta, prev=None, norm_w=None):
    d = 1 if reverse else 0
    tmap = (lambda i: NT - 1 - i) if reverse else (lambda i: i)
    width = 3 * DN_W
    r = TS // HALO
    nblk = M // HALO
    in_specs = [
        pl.BlockSpec((TS, width), lambda i: (tmap(i), C_DNQKV // width)),
        pl.BlockSpec((HALO, width), lambda i: (jnp.maximum(tmap(i) * r - 1, 0), 0)),
        pl.BlockSpec((HALO, width), lambda i: (jnp.minimum((tmap(i) + 1) * r, nblk - 1), 0)),
        pl.BlockSpec((TS, 128), lambda i: (tmap(i), C_SM // 128)),
        pl.BlockSpec((DN_CONV, width), lambda i: (0, 0)),
        pl.BlockSpec((1, 128), lambda i: (0, 0)),
        pl.BlockSpec((1, 128), lambda i: (0, 0)),
        pl.BlockSpec((None, None, None, DN_HEADS, DN_DK, 128), lambda i: (_seq_of_tile(tmap(i)), layer, d, 0, 0, 0)),
    ]
    args = [pm, pm, pm, pm, conv_w, alog, dtb, state_delta]
    has_prev = prev is not None
    if has_prev:
        in_specs += [pl.BlockSpec((TS, DN_W), lambda i: (tmap(i), 0)),
                     pl.BlockSpec((TS, DN_W), lambda i: (tmap(i), C_DNZ // DN_W)),
                     pl.BlockSpec((1, 128), lambda i: (0, 0))]
        args += [prev, pm, norm_w.reshape(1, 128)]
    out_dtype = BF16 if has_prev else F32
    return pl.pallas_call(
        functools.partial(_dn_kernel, reverse, has_prev),
        grid=(NT,),
        in_specs=in_specs,
        out_specs=[pl.BlockSpec((TS, DN_W), lambda i: (tmap(i), 0)),
                   pl.BlockSpec((None, DN_HEADS, DN_DK, 128), lambda i: (jnp.minimum(tmap(i), NT_P - 1), 0, 0, 0))],
        out_shape=[jax.ShapeDtypeStruct((M, DN_W), out_dtype),
                   jax.ShapeDtypeStruct((BATCH, DN_HEADS, DN_DK, 128), F32)],
        scratch_shapes=[pltpu.VMEM((DN_HEADS, DN_DK, 128), F32),
                        pltpu.VMEM((TS, DN_W), F32), pltpu.VMEM((TS, DN_W), F32), pltpu.VMEM((TS, DN_W), F32),
                        pltpu.VMEM((TS, 128), F32), pltpu.VMEM((TS, 128), F32),
                        pltpu.VMEM((TS // CHUNK, 128, CHUNK), F32),
                        pltpu.VMEM((TS, DN_W), F32), pltpu.VMEM((TS, DN_W), BF16),
                        pltpu.VMEM((TS, DN_W), BF16), pltpu.VMEM((TS, DN_W), BF16),
                        pltpu.VMEM((TS, DN_HEADS * CHUNK), BF16),
                        pltpu.VMEM((TS, DN_W), F32)],
        compiler_params=_params(("arbitrary",)),
        name="deltanet_bwd" if reverse else "deltanet_fwd",
    )(*args)


def _gla_kernel(reverse, has_prev, q_ref, k_ref, v_ref, sm_ref, w2_ref, b2_ref, s0_ref, *rest):
    if has_prev:
        oprev_ref, r_ref, nw_ref, o_ref, st_ref, s_scr, o_scr = rest
    else:
        o_ref, st_ref, s_scr, o_scr = rest
    t = _scan_tile(reverse)
    is_p, first, last, _, _ = _tile_info(t)

    @pl.when(last if reverse else first)
    def _():
        s_scr[...] = jnp.where(is_p, 0.0, s0_ref[...])

    logits = _dot(sm_ref[...].astype(BF16), w2_ref[...]) + b2_ref[...]
    gk = -_softplus(-logits) / GLA_NORMALIZER

    ti = lax.broadcasted_iota(jnp.int32, (TS, TS), 0)
    tj = lax.broadcasted_iota(jnp.int32, (TS, TS), 1)
    same = (ti // CHUNK) == (tj // CHUNK)
    tri = jnp.where(jnp.logical_and(same, (tj >= ti) if reverse else (tj <= ti)), 1.0, 0.0).astype(BF16)
    g1, g2, g3 = _split3(gk)
    gc = _dot(tri, g1) + _dot(tri, g2) + _dot(tri, g3)

    incl, _ = _tri_masks(reverse)
    nchunk = TS // CHUNK
    e_last = 0 if reverse else CHUNK - 1
    eg = jnp.exp(gc)
    q_all = q_ref[...] * (GLA_DK ** -0.5)
    k_all = k_ref[...]
    qg_all = (q_all * eg).astype(BF16)
    kg_all = (k_all * jnp.exp(-gc)).astype(BF16)
    v_all = v_ref[...].astype(BF16)

    probs = [(c, h) for c in range(nchunk) for h in range(GLA_HEADS)]
    rws = {c: slice(c * CHUNK, (c + 1) * CHUNK) for c in range(nchunk)}
    kls = {h: slice(h * GLA_DK, (h + 1) * GLA_DK) for h in range(GLA_HEADS)}
    vls = {h: slice(h * GLA_DV, (h + 1) * GLA_DV) for h in range(GLA_HEADS)}
    glast = {c: gc[c * CHUNK + e_last:c * CHUNK + e_last + 1, :] for c in range(nchunk)}
    kd = {c: (k_all[rws[c], :] * jnp.exp(glast[c] - gc[rws[c], :])).astype(BF16) for c in range(nchunk)}
    a = {p: jnp.where(incl, _dot_tb(qg_all[rws[p[0]], kls[p[1]]], kg_all[rws[p[0]], kls[p[1]]]), 0.0).astype(BF16)
         for p in probs}
    av = {p: _dot(a[p], v_all[rws[p[0]], vls[p[1]]]) for p in probs}
    up = {p: _dot_ta(v_all[rws[p[0]], vls[p[1]]], kd[p[0]][:, kls[p[1]]]) for p in probs}
    order = list(reversed(range(nchunk))) if reverse else list(range(nchunk))
    s_in = {}
    for h in range(GLA_HEADS):
        s = s_scr[h]
        for c in order:
            s_in[(c, h)] = s.astype(BF16)
            s = jnp.exp(glast[c][:, kls[h]]) * s + up[(c, h)]
        s_scr[h] = s
    for p in probs:
        c, h = p
        o_scr[rws[c], vls[h]] = _dot_tb(qg_all[rws[c], kls[h]], s_in[p]) + av[p]

    if has_prev:
        o_ref[...] = _gated_rms_out(o_scr[...] + oprev_ref[...], r_ref[...], nw_ref[...]).astype(o_ref.dtype)
    else:
        o_ref[...] = o_scr[...]

    @pl.when(is_p)
    def _():
        st_ref[...] = s_scr[...]


def _gla_call(pm, reverse, w2pad, b2, s0t, prev=None, norm_w=None):
    tmap = (lambda i: NT - 1 - i) if reverse else (lambda i: i)
    qkw = GLA_HEADS * GLA_DK
    vw = GLA_HEADS * GLA_DV
    in_specs = [
        pl.BlockSpec((TS, qkw), lambda i: (tmap(i), C_GQ // qkw)),
        pl.BlockSpec((TS, qkw), lambda i: (tmap(i), C_GK // qkw)),
        pl.BlockSpec((TS, vw), lambda i: (tmap(i), C_GV // vw)),
        pl.BlockSpec((TS, 128), lambda i: (tmap(i), C_SM // 128)),
        pl.BlockSpec((128, qkw), lambda i: (0, 0)),
        pl.BlockSpec((1, qkw), lambda i: (0, 0)),
        pl.BlockSpec((None, GLA_HEADS, GLA_DV, GLA_DK), lambda i: (_seq_of_tile(tmap(i)), 0, 0, 0)),
    ]
    args = [pm, pm, pm, pm, w2pad, b2, s0t]
    has_prev = prev is not None
    if has_prev:
        in_specs += [pl.BlockSpec((TS, vw), lambda i: (tmap(i), 0)),
                     pl.BlockSpec((TS, vw), lambda i: (tmap(i), C_GR // vw)),
                     pl.BlockSpec((1, 128), lambda i: (0, 0))]
        args += [prev, pm, norm_w.reshape(1, 128)]
    out_dtype = BF16 if has_prev else F32
    return pl.pallas_call(
        functools.partial(_gla_kernel, reverse, has_prev),
        grid=(NT,),
        in_specs=in_specs,
        out_specs=[pl.BlockSpec((TS, vw), lambda i: (tmap(i), 0)),
                   pl.BlockSpec((None, GLA_HEADS, GLA_DV, GLA_DK), lambda i: (jnp.minimum(tmap(i), NT_P - 1), 0, 0, 0))],
        out_shape=[jax.ShapeDtypeStruct((M, vw), out_dtype),
                   jax.ShapeDtypeStruct((BATCH, GLA_HEADS, GLA_DV, GLA_DK), F32)],
        scratch_shapes=[pltpu.VMEM((GLA_HEADS, GLA_DV, GLA_DK), F32),
                        pltpu.VMEM((TS, vw), F32)],
        compiler_params=_params(("arbitrary",)),
        name="gla_bwd" if reverse else "gla_fwd",
    )(*args)


def _pack_mix_weights(w_in_l):
    o = np.cumsum((0, 512, 1024, 256, 256, 1536, 512, 8, 8, 256, 256, 512, 512, 32)).tolist()
    sl = lambda i: w_in_l[:, o[i]:o[i + 1]]
    small = jnp.concatenate([sl(6), sl(7), sl(12), jnp.zeros((D, 128 - 48), w_in_l.dtype)], axis=1)
    mix = jnp.concatenate([sl(4), sl(0), sl(1), sl(5), sl(10), sl(11), sl(2), sl(3), sl(8), sl(9), small], axis=1)
    return mix.astype(BF16), w_in_l[:, o[13]:].astype(BF16)


def kernel(x_prompt, x_sample, cache_k, cache_v, state_delta, state_gla, c, c_ctx, w_mod, b_mod, norm1, w_in, pool_w, pool_scale, attn_sink, dn_conv, dn_a_log, dn_dt_bias, dn_norm, gla_w2, gla_b2, gla_norm, w_br_pool, w_br_attn, w_br_delta, w_br_gla, w_out, norm2, w_gate, w_up, w_down, norm_f):
    x = jnp.concatenate([x_prompt.reshape(MP, D), x_sample.reshape(MS, D)], axis=0)
    cond8 = jnp.concatenate([c_ctx[None, :], c, jnp.zeros((8 - 1 - DEC_BATCH, D), F32)], axis=0)
    mod = _modulation(cond8, w_mod, b_mod)
    cos_t, sin_t = _rope_tables()
    s_gla_t = jnp.swapaxes(state_gla, -1, -2)

    ks, vs, sds, sgs = [], [], [], []
    for l in range(DEPTH):
        sh1, sc1, gt1, sh2, sc2, gt2 = [mod[l, :, i * D:(i + 1) * D].reshape(8, 1, D) for i in range(N_MOD)]
        w_mix, w_gates = _pack_mix_weights(w_in[l])
        bf = lambda w: w.astype(BF16)

        h = _prenorm(x, norm1[l], sc1, sh1)
        pm = _mm(h, w_mix, 1024, 1152, F32, "mix_proj")

        ya = _pool(pm, pool_w[l], pool_scale[l])
        yb = jnp.concatenate([_ctx_attn(pm, attn_sink[l]),
                              _lat_attn(pm, attn_sink[l], cos_t, sin_t, cache_k, cache_v, l)], axis=0)

        alog = jnp.zeros((1, 128), F32).at[0, SM_A:SM_A + 8].set(dn_a_log[l].reshape(8))
        dtb = jnp.zeros((1, 128), F32).at[0, SM_A:SM_A + 8].set(dn_dt_bias[l].reshape(8))
        o_b, sd_b = _dn_call(pm, True, l, dn_conv[l], alog, dtb, state_delta)
        yc, sd_f = _dn_call(pm, False, l, dn_conv[l], alog, dtb, state_delta, prev=o_b, norm_w=dn_norm[l])

        w2pad = [jnp.zeros((128, GLA_HEADS * GLA_DK), F32).at[SM_LR + o * GLA_RANK:SM_LR + (o + 1) * GLA_RANK].set(gla_w2[l, o]).astype(BF16)
                 for o in range(2)]
        g_b, sg_b = _gla_call(pm, True, w2pad[1], gla_b2[l, 1].reshape(1, -1), s_gla_t[:, l, 1])
        yd, sg_f = _gla_call(pm, False, w2pad[0], gla_b2[l, 0].reshape(1, -1), s_gla_t[:, l, 0], prev=g_b, norm_w=gla_norm[l])

        merged = _merge(h, w_gates, ya, yb, yc, yd, bf(w_br_pool[l]), bf(w_br_attn[l]), bf(w_br_delta[l]), bf(w_br_gla[l]))
        x = _mm_res(merged, bf(w_out[l]), x, gt1, 1024, 512, "out_proj")

        h2 = _prenorm(x, norm2[l], sc2, sh2)
        act = _ffn_up(h2, bf(w_gate[l]), bf(w_up[l]))
        x = _mm_res(act, bf(w_down[l]), x, gt2, 512, 512, "ffn_down")

        ks.append(pm[:MP, C_K:C_K + KV_HEADS * HD].reshape(BATCH, SEQ, KV_HEADS, HD))
        vs.append(pm[:MP, C_V:C_V + KV_HEADS * HD].reshape(BATCH, SEQ, KV_HEADS, HD))
        sds.append(jnp.stack([sd_f, sd_b], axis=1))
        sgs.append(jnp.swapaxes(jnp.stack([sg_f, sg_b], axis=1), -1, -2))

    y = _finalnorm(x, norm_f)
    return (y[:MP].reshape(BATCH, SEQ, D), y[MP:].reshape(DEC_BATCH, DEC_SEQ, D),
            jnp.stack(ks, axis=1), jnp.stack(vs, axis=1), jnp.stack(sds, axis=1), jnp.stack(sgs, axis=1))
```

```python
import functools

import numpy as np
import jax
import jax.numpy as jnp
from jax import lax
from jax.experimental import pallas as pl
from jax.experimental.pallas import tpu as pltpu

F32 = jnp.float32
BF16 = jnp.bfloat16

D = 2048
BATCH = 16
SEQ = 256
DEPTH = 4
DEC_BATCH = 4
DEC_SEQ = 2048
PAST = 256
GRID_W = 64
HD = 128
ROPE_THETA = 10000.0
EPS = 1e-6
N_MOD = 6
POOL_WINDOWS = (2, 4, 8, 16)
POOL_W = 512
Q_HEADS = 8
KV_HEADS = 2
GROUP = 4
ATTN_WINDOW = 128
DN_HEADS = 4
DN_DK = 128
DN_W = 512
DN_CONV = 4
GLA_HEADS = 4
GLA_DK = 64
GLA_DV = 128
GLA_RANK = 16
GLA_NORMALIZER = 16.0
FFN = 5632
CHUNK = 64

MP = BATCH * SEQ
MS = DEC_BATCH * DEC_SEQ
M = MP + MS
TS = 256
NT_P = MP // TS
TPS = DEC_SEQ // TS
NT_S = MS // TS
NT = NT_P + NT_S
HALO = 8

C_Q, C_POOL, C_DNQKV, C_K, C_V, C_DNZ = 0, 1024, 1536, 3072, 3328, 3584
C_GQ, C_GK, C_GV, C_GR = 4096, 4352, 4608, 5120
MIXW = 5632
MIX_TN = 512
MIX_SRC = (1, 2, 0, 4, 5, 6, 3, 7, 8, 9, 10)
MIX_ALIGNED = 8
MIX_SHIFT = 16
SMALL_BLK_A = 4096 // 128
SMALL_BLK_LR = 5632 // 128
GATE_COL = 5680
GATE_SHIFT = GATE_COL % 128
IN_WIDTH = GATE_COL + 4 * D
SM_A, SM_B, SM_LR = 0, 8, 16

VMEM_LIMIT = 56 * 1024 * 1024


def _params(sem):
    return pltpu.CompilerParams(dimension_semantics=sem, vmem_limit_bytes=VMEM_LIMIT)


def _cond_row(row0):
    return jnp.where(row0 < MP, 0, 1 + (row0 - MP) // DEC_SEQ)


def _tile_info(t):
    is_p = t < NT_P
    s = jnp.maximum(t - NT_P, 0)
    j = s % TPS
    first = jnp.logical_or(is_p, j == 0)
    last = jnp.logical_or(is_p, j == TPS - 1)
    base = jnp.where(is_p, 0, j * TS)
    seq_len = jnp.where(is_p, SEQ, DEC_SEQ)
    return is_p, first, last, base, seq_len


def _silu(x):
    return x * jax.nn.sigmoid(x)


def _softplus(x):
    return jnp.maximum(x, 0.0) + jnp.log(1.0 + jnp.exp(-jnp.abs(x)))


def _dot(a, b):
    return jnp.dot(a, b, preferred_element_type=F32)


def _dot_tb(a, b):
    return lax.dot_general(a, b, (((1,), (1,)), ((), ())), preferred_element_type=F32)


def _dot_ta(a, b):
    return lax.dot_general(a, b, (((0,), (0,)), ((), ())), preferred_element_type=F32)


def _split2(x):
    hi = x.astype(BF16)
    lo = (x - hi.astype(F32)).astype(BF16)
    return hi, lo


def _split3(x):
    p1 = x.astype(BF16)
    r = x - p1.astype(F32)
    p2 = r.astype(BF16)
    p3 = (r - p2.astype(F32)).astype(BF16)
    return p1, p2, p3


def _dot_hp(a, b):
    ah, al = _split2(a)
    bh, bl = _split2(b)
    return _dot(ah, bh) + (_dot(ah, bl) + _dot(al, bh))


def _mod_kernel(c_ref, w_ref, b_ref, o_ref):
    a = _silu(c_ref[...]).astype(BF16)
    o_ref[...] = _dot(a, w_ref[...].astype(BF16)) + b_ref[...]


def _modulation(cond8, w_mod, b_mod):
    tn = 1024
    n = N_MOD * D
    return pl.pallas_call(
        _mod_kernel,
        grid=(DEPTH, n // tn),
        in_specs=[pl.BlockSpec((8, D), lambda l, j: (0, 0)),
                  pl.BlockSpec((None, D, tn), lambda l, j: (l, 0, j)),
                  pl.BlockSpec((None, 1, tn), lambda l, j: (l, 0, j))],
        out_specs=pl.BlockSpec((None, 8, tn), lambda l, j: (l, 0, j)),
        out_shape=jax.ShapeDtypeStruct((DEPTH, 8, n), F32),
        compiler_params=_params(("parallel", "parallel")),
        name="modulation",
    )(cond8, w_mod, b_mod.reshape(DEPTH, 1, n))


def _rms_rows(x, w):
    ms = jnp.mean(x * x, axis=-1, keepdims=True)
    return x * lax.rsqrt(ms + EPS) * w


def _prenorm_kernel(x_ref, nw_ref, sc_ref, sh_ref, h_ref):
    y = _rms_rows(x_ref[...], nw_ref[...])
    h_ref[...] = (y * (1.0 + sc_ref[...]) + sh_ref[...]).astype(h_ref.dtype)


def _mod_spec(layer, chunk, tm, tn=D, row_axis=0, col_axis=None):
    def index(*g):
        col = 0 if col_axis is None else g[col_axis]
        return (layer, _cond_row(g[row_axis] * tm), chunk, 0, col)
    return pl.BlockSpec((None, None, None, 1, tn), index)


def _prenorm(x, norm_w, mod, layer, sc_chunk, sh_chunk):
    tm = 512
    return pl.pallas_call(
        _prenorm_kernel,
        grid=(M // tm,),
        in_specs=[pl.BlockSpec((tm, D), lambda i: (i, 0)),
                  pl.BlockSpec((None, 1, D), lambda i: (layer, 0, 0)),
                  _mod_spec(layer, sc_chunk, tm), _mod_spec(layer, sh_chunk, tm)],
        out_specs=pl.BlockSpec((tm, D), lambda i: (i, 0)),
        out_shape=jax.ShapeDtypeStruct((M, D), BF16),
        compiler_params=_params(("parallel",)),
        name="prenorm",
    )(x, norm_w.reshape(DEPTH, 1, D), mod, mod)


def _finalnorm_kernel(x_ref, nw_ref, o_ref):
    o_ref[...] = _rms_rows(x_ref[...], nw_ref[...])


def _finalnorm(x, nw):
    tm = 512
    return pl.pallas_call(
        _finalnorm_kernel,
        grid=(M // tm,),
        in_specs=[pl.BlockSpec((tm, D), lambda i: (i, 0)),
                  pl.BlockSpec((1, D), lambda i: (0, 0))],
        out_specs=pl.BlockSpec((tm, D), lambda i: (i, 0)),
        out_shape=jax.ShapeDtypeStruct((M, D), F32),
        compiler_params=_params(("parallel",)),
        name="finalnorm",
    )(x, nw.reshape(1, D))


def _first_row_block():
    return pl.program_id(1) == 0


def _mix_kernel(h_ref, w_ref, wx_ref, o_ref, wb_scr):
    j = pl.program_id(0)

    @pl.when(jnp.logical_and(_first_row_block(), j < MIX_ALIGNED))
    def _():
        wb_scr[...] = w_ref[...].astype(BF16)

    @pl.when(jnp.logical_and(_first_row_block(), j >= MIX_ALIGNED))
    def _():
        wide = jnp.concatenate([w_ref[...], wx_ref[...]], axis=1)
        wb_scr[...] = wide[:, MIX_SHIFT:MIX_SHIFT + MIX_TN].astype(BF16)

    o_ref[...] = _dot(h_ref[...], wb_scr[...])


def _mix_src_tile(j):
    src = j
    for out_tile, src_tile in enumerate(MIX_SRC):
        src = jnp.where(j == out_tile, src_tile, src)
    return src


def _mix_proj(h, w_in, layer):
    tm, tn = 1024, MIX_TN
    sub = tn // 128
    return pl.pallas_call(
        _mix_kernel,
        grid=(MIXW // tn, M // tm),
        in_specs=[pl.BlockSpec((tm, D), lambda j, i: (i, 0)),
                  pl.BlockSpec((None, D, tn), lambda j, i: (layer, 0, _mix_src_tile(j))),
                  pl.BlockSpec((None, D, 128), lambda j, i: (layer, 0, (_mix_src_tile(j) + 1) * sub))],
        out_specs=pl.BlockSpec((tm, tn), lambda j, i: (i, j)),
        out_shape=jax.ShapeDtypeStruct((M, MIXW), F32),
        scratch_shapes=[pltpu.VMEM((D, tn), BF16)],
        compiler_params=_params(("arbitrary", "arbitrary")),
        name="mix_proj",
    )(h, w_in, w_in)


def _small_kernel(h_ref, wa_ref, wl_ref, o_ref, wb_scr):
    @pl.when(pl.program_id(0) == 0)
    def _():
        lane = lax.broadcasted_iota(jnp.int32, (D, 128), 1)
        w = jnp.where(lane < SM_LR, wa_ref[...], jnp.where(lane < SM_LR + 2 * GLA_RANK, wl_ref[...], 0.0))
        wb_scr[...] = w.astype(BF16)

    o_ref[...] = _dot(h_ref[...], wb_scr[...])


def _small_proj(h, w_in, layer):
    tm = 1024
    return pl.pallas_call(
        _small_kernel,
        grid=(M // tm,),
        in_specs=[pl.BlockSpec((tm, D), lambda i: (i, 0)),
                  pl.BlockSpec((None, D, 128), lambda i: (layer, 0, SMALL_BLK_A)),
                  pl.BlockSpec((None, D, 128), lambda i: (layer, 0, SMALL_BLK_LR))],
        out_specs=pl.BlockSpec((tm, 128), lambda i: (i, 0)),
        out_shape=jax.ShapeDtypeStruct((M, 128), F32),
        scratch_shapes=[pltpu.VMEM((D, 128), BF16)],
        compiler_params=_params(("arbitrary",)),
        name="small_proj",
    )(h, w_in, w_in)


def _ffn_up_kernel(h_ref, wg_ref, wu_ref, o_ref, wg_scr, wu_scr):
    @pl.when(_first_row_block())
    def _():
        wg_scr[...] = wg_ref[...].astype(BF16)
        wu_scr[...] = wu_ref[...].astype(BF16)

    h = h_ref[...]
    o_ref[...] = (_silu(_dot(h, wg_scr[...])) * _dot(h, wu_scr[...])).astype(o_ref.dtype)


def _ffn_up(h, w_gate, w_up, layer):
    tm, tn = 1024, 512
    wspec = pl.BlockSpec((None, D, tn), lambda j, i: (layer, 0, j))
    return pl.pallas_call(
        _ffn_up_kernel,
        grid=(FFN // tn, M // tm),
        in_specs=[pl.BlockSpec((tm, D), lambda j, i: (i, 0)), wspec, wspec],
        out_specs=pl.BlockSpec((tm, tn), lambda j, i: (i, j)),
        out_shape=jax.ShapeDtypeStruct((M, FFN), BF16),
        scratch_shapes=[pltpu.VMEM((D, tn), BF16), pltpu.VMEM((D, tn), BF16)],
        compiler_params=_params(("arbitrary", "arbitrary")),
        name="ffn_up",
    )(h, w_gate, w_up)


def _ffn_down_kernel(a_ref, w_ref, x_ref, gt_ref, o_ref, wb_scr):
    @pl.when(_first_row_block())
    def _():
        wb_scr[...] = w_ref[...].astype(BF16)

    o_ref[...] = x_ref[...] + gt_ref[...] * _dot(a_ref[...], wb_scr[...])


def _ffn_down(act, w_down, x, mod, layer, gt_chunk):
    tm, tn = 512, 512
    return pl.pallas_call(
        _ffn_down_kernel,
        grid=(D // tn, M // tm),
        in_specs=[pl.BlockSpec((tm, FFN), lambda j, i: (i, 0)),
                  pl.BlockSpec((None, FFN, tn), lambda j, i: (layer, 0, j)),
                  pl.BlockSpec((tm, tn), lambda j, i: (i, j)),
                  _mod_spec(layer, gt_chunk, tm, tn, row_axis=1, col_axis=0)],
        out_specs=pl.BlockSpec((tm, tn), lambda j, i: (i, j)),
        out_shape=jax.ShapeDtypeStruct((M, D), F32),
        scratch_shapes=[pltpu.VMEM((FFN, tn), BF16)],
        compiler_params=_params(("arbitrary", "arbitrary")),
        name="ffn_down",
    )(act, w_down, x, mod)


BR_WIDTHS = (POOL_W, Q_HEADS * HD, DN_W, GLA_HEADS * GLA_DV)
MERGE_TN = 256


def _merge_kernel(h_ref, *refs):
    g_refs, gx_refs = refs[0:8:2], refs[1:8:2]
    y_refs, w_refs = refs[8:12], refs[12:16]
    o_ref, g_scr, w_scr = refs[16:]

    @pl.when(_first_row_block())
    def _():
        row = 0
        for b in range(4):
            wide = jnp.concatenate([g_refs[b][...], gx_refs[b][...]], axis=1)
            g_scr[b] = wide[:, GATE_SHIFT:GATE_SHIFT + MERGE_TN].astype(BF16)
            w_scr[row:row + BR_WIDTHS[b], :] = w_refs[b][...].astype(BF16)
            row += BR_WIDTHS[b]

    h = h_ref[...]
    acc = None
    row = 0
    for b in range(4):
        term = jax.nn.sigmoid(_dot(h, g_scr[b])) * _dot(y_refs[b][...], w_scr[row:row + BR_WIDTHS[b], :])
        acc = term if acc is None else acc + term
        row += BR_WIDTHS[b]
    o_ref[...] = acc.astype(o_ref.dtype)


def _merge(h, w_in, ys, w_brs, layer):
    tm, tn = 1024, MERGE_TN
    nj = D // tn
    first = (GATE_COL - GATE_SHIFT) // tn
    sub = tn // 128
    once = dict(pipeline_mode=pl.Buffered(1))
    gate_specs = []
    for b in range(4):
        gate_specs.append(pl.BlockSpec((None, D, tn), functools.partial(
            lambda j, i, b: (layer, 0, first + b * nj + j), b=b), **once))
        gate_specs.append(pl.BlockSpec((None, D, 128), functools.partial(
            lambda j, i, b: (layer, 0, (first + b * nj + j + 1) * sub), b=b), **once))
    y_specs = [pl.BlockSpec((tm, w), lambda j, i: (i, 0)) for w in BR_WIDTHS]
    w_specs = [pl.BlockSpec((None, w, tn), lambda j, i: (layer, 0, j), **once) for w in BR_WIDTHS]
    return pl.pallas_call(
        _merge_kernel,
        grid=(nj, M // tm),
        in_specs=[pl.BlockSpec((tm, D), lambda j, i: (i, 0))] + gate_specs + y_specs + w_specs,
        out_specs=pl.BlockSpec((tm, tn), lambda j, i: (i, j)),
        out_shape=jax.ShapeDtypeStruct((M, D), BF16),
        scratch_shapes=[pltpu.VMEM((4, D, tn), BF16), pltpu.VMEM((sum(BR_WIDTHS), tn), BF16)],
        compiler_params=_params(("arbitrary", "arbitrary")),
        name="merge",
    )(h, *([w_in] * 8), *ys, *w_brs)


def _out_proj_kernel(m_ref, w_ref, x_ref, gt_ref, nw_ref, sc_ref, sh_ref, xo_ref, h_ref):
    xn = x_ref[...] + gt_ref[...] * _dot(m_ref[...], w_ref[...])
    xo_ref[...] = xn
    h_ref[...] = (_rms_rows(xn, nw_ref[...]) * (1.0 + sc_ref[...]) + sh_ref[...]).astype(h_ref.dtype)


def _out_proj(merged, w_out_bf16, x, mod, norm_w, layer, gt_chunk, sc_chunk, sh_chunk):
    tm = 512
    row = pl.BlockSpec((tm, D), lambda i: (i, 0))
    return pl.pallas_call(
        _out_proj_kernel,
        grid=(M // tm,),
        in_specs=[row,
                  pl.BlockSpec((None, D, D), lambda i: (layer, 0, 0), pipeline_mode=pl.Buffered(1)),
                  row, _mod_spec(layer, gt_chunk, tm),
                  pl.BlockSpec((None, 1, D), lambda i: (layer, 0, 0)),
                  _mod_spec(layer, sc_chunk, tm), _mod_spec(layer, sh_chunk, tm)],
        out_specs=[row, row],
        out_shape=[jax.ShapeDtypeStruct((M, D), F32), jax.ShapeDtypeStruct((M, D), BF16)],
        compiler_params=_params(("parallel",)),
        name="out_proj",
    )(merged, w_out_bf16, x, mod, norm_w.reshape(DEPTH, 1, D), mod, mod)


def _pool_kernel(u_ref, up_ref, un_ref, w_ref, sc_ref, o_ref):
    t = pl.program_id(0)
    _, first, last, base, seq_len = _tile_info(t)
    u = u_ref[...]
    prev = jnp.where(first, 0.0, up_ref[...])
    nxt = jnp.where(last, 0.0, un_ref[...])
    ext = jnp.concatenate([prev, u, nxt], axis=0)
    pos = base + lax.broadcasted_iota(jnp.int32, (TS, 1), 0)
    outs = []
    for gi, win in enumerate(POOL_WINDOWS):
        lanes = slice(gi * 128, (gi + 1) * 128)
        p = ext[:, lanes]
        s = 1
        while s < win:
            p = p + pltpu.roll(p, s, 0)
            s *= 2
        d = HALO + win // 2 - 1
        ssum = p[d:d + TS]
        cnt = (jnp.minimum(pos + win // 2, seq_len) - jnp.maximum(pos - win // 2, 0)).astype(F32)
        pg = ssum / cnt - u[:, lanes]
        outs.append(_dot(pg.astype(BF16), w_ref[gi].astype(BF16)))
    o_ref[...] = (jnp.concatenate(outs, axis=1) * sc_ref[...]).astype(o_ref.dtype)


def _halo_specs(width, colblk):
    r = TS // HALO
    nblk = M // HALO
    prev = pl.BlockSpec((HALO, width), lambda t: (jnp.maximum(t * r - 1, 0), colblk))
    nxt = pl.BlockSpec((HALO, width), lambda t: (jnp.minimum((t + 1) * r, nblk - 1), colblk))
    return prev, nxt


def _pool(pm, pool_w, pool_scale):
    cb = C_POOL // POOL_W
    prev, nxt = _halo_specs(POOL_W, cb)
    return pl.pallas_call(
        _pool_kernel,
        grid=(NT,),
        in_specs=[pl.BlockSpec((TS, POOL_W), lambda t: (t, cb)), prev, nxt,
                  pl.BlockSpec((4, 128, 128), lambda t: (0, 0, 0)),
                  pl.BlockSpec((1, POOL_W), lambda t: (0, 0))],
        out_specs=pl.BlockSpec((TS, POOL_W), lambda t: (t, 0)),
        out_shape=jax.ShapeDtypeStruct((M, POOL_W), BF16),
        compiler_params=_params(("parallel",)),
        name="pool",
    )(pm, pm, pm, pool_w, pool_scale.reshape(1, POOL_W))


def _ctx_attn_kernel(sink_ref, q_ref, k_ref, v_ref, o_ref):
    q = q_ref[...] * (HD ** -0.5)
    k = k_ref[...].astype(BF16)
    v = v_ref[...].astype(BF16)
    for n in range(KV_HEADS):
        kn = k[:, n * HD:(n + 1) * HD]
        vn = v[:, n * HD:(n + 1) * HD]
        for g in range(GROUP):
            h = n * GROUP + g
            qh = q[:, h * HD:(h + 1) * HD].astype(BF16)
            s = _dot_tb(qh, kn)
            snk = sink_ref[h]
            m = jnp.maximum(jnp.max(s, axis=-1, keepdims=True), snk)
            e = jnp.exp(s - m)
            den = jnp.sum(e, axis=-1, keepdims=True) + jnp.exp(snk - m)
            o = _dot(e.astype(BF16), vn) / den
            o_ref[:, h * HD:(h + 1) * HD] = o.astype(o_ref.dtype)


def _ctx_attn(pm, sink):
    qw, kw = Q_HEADS * HD, KV_HEADS * HD
    return pl.pallas_call(
        _ctx_attn_kernel,
        grid=(NT_P,),
        in_specs=[pl.BlockSpec(memory_space=pltpu.SMEM),
                  pl.BlockSpec((TS, qw), lambda t: (t, C_Q // qw)),
                  pl.BlockSpec((TS, kw), lambda t: (t, C_K // kw)),
                  pl.BlockSpec((TS, kw), lambda t: (t, C_V // kw))],
        out_specs=pl.BlockSpec((TS, qw), lambda t: (t, 0)),
        out_shape=jax.ShapeDtypeStruct((MP, qw), BF16),
        compiler_params=_params(("parallel",)),
        name="ctx_attn",
    )(sink, pm, pm, pm)


def _rope(x, cs, sn):
    lane = lax.broadcasted_iota(jnp.int32, x.shape, 1)
    swapped = jnp.where((lane % 64) < 32, pltpu.roll(x, 96, 1), pltpu.roll(x, 32, 1))
    return x * cs + swapped * sn


def _lat_attn_kernel(sink_ref, q_ref, km_ref, kp_ref, kn_ref, vm_ref, vp_ref, vn_ref,
                     cm_ref, cp_ref, cn_ref, sm_ref, sp_ref, sn_ref, kc_ref, vc_ref, o_ref):
    ts = pl.program_id(0)
    base = (ts % TPS) * TS
    half = TS // 2
    nk = TS + 2 * half
    q = q_ref[...] * (HD ** -0.5)
    cq, sq = cm_ref[...], sm_ref[...]
    kl = jnp.concatenate([kp_ref[...], km_ref[...], kn_ref[...]], axis=0)
    vl = jnp.concatenate([vp_ref[...], vm_ref[...], vn_ref[...]], axis=0).astype(BF16)
    ck = jnp.concatenate([cp_ref[...], cq, cn_ref[...]], axis=0)
    sk = jnp.concatenate([sp_ref[...], sq, sn_ref[...]], axis=0)
    kc = kc_ref[...].astype(BF16)
    vc = vc_ref[...].astype(BF16)
    qpos = base + lax.broadcasted_iota(jnp.int32, (TS, 1), 0)
    kpos = base - half + lax.broadcasted_iota(jnp.int32, (1, nk), 1)
    dist = qpos - kpos
    ok = (jnp.abs(dist) <= ATTN_WINDOW).astype(jnp.int32) * (kpos >= 0).astype(jnp.int32) * (kpos < DEC_SEQ).astype(jnp.int32)
    bias = jnp.where(ok > 0, 0.0, -1e30)
    for n in range(KV_HEADS):
        lanes = slice(n * HD, (n + 1) * HD)
        kr = _rope(kl[:, lanes], ck, sk).astype(BF16)
        vln, kcn, vcn = vl[:, lanes], kc[:, lanes], vc[:, lanes]
        for g in range(GROUP):
            h = n * GROUP + g
            qh = _rope(q[:, h * HD:(h + 1) * HD], cq, sq).astype(BF16)
            s_c = _dot_tb(qh, kcn)
            s_l = _dot_tb(qh, kr) + bias
            snk = sink_ref[h]
            m = jnp.maximum(jnp.maximum(jnp.max(s_c, axis=-1, keepdims=True),
                                        jnp.max(s_l, axis=-1, keepdims=True)), snk)
            e_c = jnp.exp(s_c - m)
            e_l = jnp.exp(s_l - m)
            den = jnp.sum(e_c, axis=-1, keepdims=True) + jnp.sum(e_l, axis=-1, keepdims=True) + jnp.exp(snk - m)
            o = (_dot(e_c.astype(BF16), vcn) + _dot(e_l.astype(BF16), vln)) / den
            o_ref[:, h * HD:(h + 1) * HD] = o.astype(o_ref.dtype)


def _lat_attn(pm, sink, cos_t, sin_t, cache_k, cache_v, layer):
    qw, kw = Q_HEADS * HD, KV_HEADS * HD
    half = TS // 2
    nhb = M // half
    nhs = DEC_SEQ // half

    def main(w, cb):
        return pl.BlockSpec((TS, w), lambda ts: (NT_P + ts, cb))

    def prev(cb):
        return pl.BlockSpec((half, kw), lambda ts: (2 * (NT_P + ts) - 1, cb))

    def nxt(cb):
        return pl.BlockSpec((half, kw), lambda ts: (jnp.minimum(2 * (NT_P + ts) + 2, nhb - 1), cb))

    tab_m = pl.BlockSpec((TS, HD), lambda ts: (ts % TPS, 0))
    tab_p = pl.BlockSpec((half, HD), lambda ts: (jnp.maximum(2 * (ts % TPS) - 1, 0), 0))
    tab_n = pl.BlockSpec((half, HD), lambda ts: (jnp.minimum(2 * (ts % TPS) + 2, nhs - 1), 0))
    ctx = pl.BlockSpec((None, None, PAST, kw), lambda ts: (ts // TPS, layer, 0, 0))
    return pl.pallas_call(
        _lat_attn_kernel,
        grid=(NT_S,),
        in_specs=[pl.BlockSpec(memory_space=pltpu.SMEM),
                  main(qw, C_Q // qw),
                  main(kw, C_K // kw), prev(C_K // kw), nxt(C_K // kw),
                  main(kw, C_V // kw), prev(C_V // kw), nxt(C_V // kw),
                  tab_m, tab_p, tab_n, tab_m, tab_p, tab_n, ctx, ctx],
        out_specs=pl.BlockSpec((TS, qw), lambda ts: (ts, 0)),
        out_shape=jax.ShapeDtypeStruct((MS, qw), BF16),
        compiler_params=_params(("parallel",)),
        name="lat_attn",
    )(sink, pm, pm, pm, pm, pm, pm, pm, cos_t, cos_t, cos_t, sin_t, sin_t, sin_t,
      cache_k.reshape(DEC_BATCH, DEPTH, PAST, kw), cache_v.reshape(DEC_BATCH, DEPTH, PAST, kw))


def _rope_tables():
    quarter = HD // 4
    inv = ROPE_THETA ** (-jnp.arange(quarter, dtype=F32) / quarter)
    t = jnp.arange(DEC_SEQ)
    row = (t // GRID_W).astype(F32)
    col = (t % GRID_W).astype(F32)
    ar = row[:, None] * inv[None, :]
    ac = col[:, None] * inv[None, :]
    cos_t = jnp.concatenate([jnp.cos(ar), jnp.cos(ar), jnp.cos(ac), jnp.cos(ac)], axis=-1)
    sin_t = jnp.concatenate([-jnp.sin(ar), jnp.sin(ar), -jnp.sin(ac), jnp.sin(ac)], axis=-1)
    return cos_t, sin_t


def _scan_tile(reverse):
    i = pl.program_id(0)
    return (NT - 1 - i) if reverse else i


def _tri_masks(reverse):
    ii = lax.broadcasted_iota(jnp.int32, (CHUNK, CHUNK), 0)
    jj = lax.broadcasted_iota(jnp.int32, (CHUNK, CHUNK), 1)
    incl = (jj >= ii) if reverse else (jj <= ii)
    strict = (jj > ii) if reverse else (jj < ii)
    return incl, strict


def _gated_rms_out(o, gate, nw):
    outs = []
    for h in range(4):
        lanes = slice(h * 128, (h + 1) * 128)
        outs.append(_rms_rows(o[:, lanes], nw) * _silu(gate[:, lanes]))
    return jnp.concatenate(outs, axis=1)


def _dot_hp_all(xs, ys):
    hh = [_dot(x[0], y[0]) for x, y in zip(xs, ys)]
    hl = [_dot(x[0], y[1]) for x, y in zip(xs, ys)]
    lh = [_dot(x[1], y[0]) for x, y in zip(xs, ys)]
    return [a + (b + c) for a, b, c in zip(hh, hl, lh)]


def _inv_unit_tri_all(lms):
    ii = lax.broadcasted_iota(jnp.int32, (CHUNK, CHUNK), 0)
    jj = lax.broadcasted_iota(jnp.int32, (CHUNK, CHUNK), 1)
    eye = jnp.where(ii == jj, 1.0, 0.0)
    ps = [eye - lm for lm in lms]
    mbs = [lm.astype(BF16) for lm in lms]
    mks = [_dot(m, m) for m in mbs]
    span = 4
    while span <= CHUNK:
        mbs = [m.astype(BF16) for m in mks]
        incs = [_dot(p.astype(BF16), m) for p, m in zip(ps, mbs)]
        if span < CHUNK:
            mks = [_dot(m, m) for m in mbs]
        ps = [p + inc for p, inc in zip(ps, incs)]
        span *= 2
    lts = _dot_hp_all([_split2(lm) for lm in lms], [_split2(t) for t in ps])
    rs = [(eye - t) - lt for t, lt in zip(ps, lts)]
    return [t + _dot(t.astype(BF16), r.astype(BF16)) for t, r in zip(ps, rs)]


def _dn_kernel(reverse, has_prev, x_ref, xp_ref, xn_ref, sm_ref, cw_ref, alog_ref, dtb_ref, s0_ref, *rest):
    if has_prev:
        oprev_ref, z_ref, nw_ref = rest[:3]
        rest = rest[3:]
    (o_ref, st_ref, s_scr, q_scr, k_scr, v_scr, b_scr, gcc_scr, gcr_scr,
     u_scr, w_scr, qe_scr, kd_scr, a_scr, o_scr) = rest
    t = _scan_tile(reverse)
    is_p, first, last, _, _ = _tile_info(t)
    d = 1 if reverse else 0
    nchunk = TS // CHUNK
    e_last = 0 if reverse else CHUNK - 1

    @pl.when(last if reverse else first)
    def _():
        s_scr[...] = jnp.where(is_p, 0.0, s0_ref[...])

    x = x_ref[...]
    ext = jnp.concatenate([jnp.where(first, 0.0, xp_ref[...]), x, jnp.where(last, 0.0, xn_ref[...])], axis=0)
    cw = cw_ref[...]
    conv = None
    for j in range(DN_CONV):
        r0 = HALO + j - DN_CONV // 2
        term = ext[r0:r0 + TS] * cw[j:j + 1, :]
        conv = term if conv is None else conv + term
    act = _silu(conv)
    for h in range(DN_HEADS):
        lanes = slice(h * 128, (h + 1) * 128)
        qh = act[:, h * 128:(h + 1) * 128]
        kh = act[:, DN_W + h * 128:DN_W + (h + 1) * 128]
        q_scr[:, lanes] = qh * lax.rsqrt(jnp.sum(qh * qh, axis=-1, keepdims=True) + EPS) * (DN_DK ** -0.5)
        k_scr[:, lanes] = kh * lax.rsqrt(jnp.sum(kh * kh, axis=-1, keepdims=True) + EPS)
    v_scr[...] = act[:, 2 * DN_W:3 * DN_W]
    sm = sm_ref[...]
    b_scr[...] = jax.nn.sigmoid(sm)
    g = -jnp.exp(alog_ref[...]) * _softplus(sm + dtb_ref[...])

    ti = lax.broadcasted_iota(jnp.int32, (TS, TS), 0)
    tj = lax.broadcasted_iota(jnp.int32, (TS, TS), 1)
    same = (ti // CHUNK) == (tj // CHUNK)
    lower = jnp.where(jnp.logical_and(same, tj <= ti), 1.0, 0.0).astype(BF16)
    upper = jnp.where(jnp.logical_and(same, tj >= ti), 1.0, 0.0).astype(BF16)
    tri, tri_t = (upper, lower) if reverse else (lower, upper)
    g1, g2, g3 = _split3(g)
    gcc_scr[...] = _dot(tri, g1) + _dot(tri, g2) + _dot(tri, g3)
    gcr = _dot_ta(g1, tri_t) + _dot_ta(g2, tri_t) + _dot_ta(g3, tri_t)
    for c in range(nchunk):
        gcr_scr[c] = gcr[:, c * CHUNK:(c + 1) * CHUNK]

    incl, strict = _tri_masks(reverse)

    def prep_body(pi, carry):
        rows, gblk, gcr_c, bt = {}, {}, {}, {}
        for cc in range(2):
            c = pi * 2 + cc
            rows[cc] = pl.ds(pl.multiple_of(c * CHUNK, CHUNK), CHUNK)
            gblk[cc] = gcc_scr[rows[cc], :]
            gcr_c[cc] = gcr_scr[c]
            bt[cc] = b_scr[rows[cc], :]
        info = []
        for cc in range(2):
            for h in range(DN_HEADS):
                lanes = slice(h * 128, (h + 1) * 128)
                r = SM_A + d * DN_HEADS + h
                gcol = gblk[cc][:, r:r + 1]
                grow = gcr_c[cc][r:r + 1, :]
                glast = gblk[cc][e_last:e_last + 1, r:r + 1]
                beta = bt[cc][:, SM_B + d * DN_HEADS + h:SM_B + d * DN_HEADS + h + 1]
                qc, kc, vc = q_scr[rows[cc], lanes], k_scr[rows[cc], lanes], v_scr[rows[cc], lanes]
                decay = jnp.exp(jnp.where(incl, gcol - grow, -1e30))
                kb = kc * beta
                eg = jnp.exp(gcol)
                qe_scr[rows[cc], lanes] = (qc * eg).astype(BF16)
                kd_scr[rows[cc], lanes] = (kc * jnp.exp(glast - gcol)).astype(BF16)
                info.append(dict(cc=cc, h=h, lanes=lanes, decay=decay, kbb=kb.astype(BF16), kcb=kc.astype(BF16),
                                 qcb=qc.astype(BF16), vbb=(vc * beta).astype(BF16), kgb=(kb * eg).astype(BF16)))
        kk = [_dot_tb(p["kbb"], p["kcb"]) for p in info]
        qk = [_dot_tb(p["qcb"], p["kcb"]) for p in info]
        for p, m in zip(info, qk):
            a_scr[rows[p["cc"]], p["h"] * CHUNK:(p["h"] + 1) * CHUNK] = (m * p["decay"]).astype(BF16)
        lms = [jnp.where(strict, m * p["decay"], 0.0) for p, m in zip(info, kk)]
        tinvs = [tv.astype(BF16) for tv in _inv_unit_tri_all(lms)]
        us = [_dot(tv, p["vbb"]) for tv, p in zip(tinvs, info)]
        ws = [_dot(tv, p["kgb"]) for tv, p in zip(tinvs, info)]
        for p, u, w in zip(info, us, ws):
            u_scr[rows[p["cc"]], p["lanes"]] = u
            w_scr[rows[p["cc"]], p["lanes"]] = w.astype(BF16)
        return carry

    lax.fori_loop(0, nchunk // 2, prep_body, 0)

    def chunk_body(ci, carry):
        c = (nchunk - 1 - ci) if reverse else ci
        rows = pl.ds(pl.multiple_of(c * CHUNK, CHUNK), CHUNK)
        glast_row = gcc_scr[pl.ds(c * CHUNK + e_last, 1), :]
        heads = range(DN_HEADS)
        lanes = [slice(h * 128, (h + 1) * 128) for h in heads]
        s_old = [s_scr[h] for h in heads]
        sb = [s.astype(BF16) for s in s_old]
        ws = [_dot(w_scr[rows, lanes[h]], sb[h]) for h in heads]
        qs = [_dot(qe_scr[rows, lanes[h]], sb[h]) for h in heads]
        vnb = [(u_scr[rows, lanes[h]] - ws[h]).astype(BF16) for h in heads]
        av = [_dot(a_scr[rows, h * CHUNK:(h + 1) * CHUNK], vnb[h]) for h in heads]
        up = [_dot_ta(kd_scr[rows, lanes[h]], vnb[h]) for h in heads]
        for h in heads:
            r = SM_A + d * DN_HEADS + h
            o_scr[rows, lanes[h]] = qs[h] + av[h]
            s_scr[h] = jnp.exp(glast_row[:, r:r + 1]) * s_old[h] + up[h]
        return carry

    lax.fori_loop(0, nchunk, chunk_body, 0)

    if has_prev:
        o_ref[...] = _gated_rms_out(o_scr[...] + oprev_ref[...], z_ref[...], nw_ref[...]).astype(o_ref.dtype)
    else:
        o_ref[...] = o_scr[...]

    @pl.when(is_p)
    def _():
        st_ref[...] = s_scr[...]


def _seq_of_tile(t):
    return jnp.maximum(t - NT_P, 0) // TPS


def _dn_call(pm, ps, reverse, layer, conv_w, alog, dtb, state_delta, prev=None, norm_w=None):
    d = 1 if reverse else 0
    tmap = (lambda i: NT - 1 - i) if reverse else (lambda i: i)
    width = 3 * DN_W
    cb = C_DNQKV // width
    r = TS // HALO
    nblk = M // HALO
    in_specs = [
        pl.BlockSpec((TS, width), lambda i: (tmap(i), cb)),
        pl.BlockSpec((HALO, width), lambda i: (jnp.maximum(tmap(i) * r - 1, 0), cb)),
        pl.BlockSpec((HALO, width), lambda i: (jnp.minimum((tmap(i) + 1) * r, nblk - 1), cb)),
        pl.BlockSpec((TS, 128), lambda i: (tmap(i), 0)),
        pl.BlockSpec((DN_CONV, width), lambda i: (0, 0)),
        pl.BlockSpec((1, 128), lambda i: (0, 0)),
        pl.BlockSpec((1, 128), lambda i: (0, 0)),
        pl.BlockSpec((None, None, None, DN_HEADS, DN_DK, 128), lambda i: (_seq_of_tile(tmap(i)), layer, d, 0, 0, 0)),
    ]
    args = [pm, pm, pm, ps, conv_w, alog, dtb, state_delta]
    has_prev = prev is not None
    if has_prev:
        in_specs += [pl.BlockSpec((TS, DN_W), lambda i: (tmap(i), 0)),
                     pl.BlockSpec((TS, DN_W), lambda i: (tmap(i), C_DNZ // DN_W)),
                     pl.BlockSpec((1, 128), lambda i: (0, 0))]
        args += [prev, pm, norm_w.reshape(1, 128)]
    out_dtype = BF16 if has_prev else F32
    return pl.pallas_call(
        functools.partial(_dn_kernel, reverse, has_prev),
        grid=(NT,),
        in_specs=in_specs,
        out_specs=[pl.BlockSpec((TS, DN_W), lambda i: (tmap(i), 0)),
                   pl.BlockSpec((None, DN_HEADS, DN_DK, 128), lambda i: (jnp.minimum(tmap(i), NT_P - 1), 0, 0, 0))],
        out_shape=[jax.ShapeDtypeStruct((M, DN_W), out_dtype),
                   jax.ShapeDtypeStruct((BATCH, DN_HEADS, DN_DK, 128), F32)],
        scratch_shapes=[pltpu.VMEM((DN_HEADS, DN_DK, 128), F32),
                        pltpu.VMEM((TS, DN_W), F32), pltpu.VMEM((TS, DN_W), F32), pltpu.VMEM((TS, DN_W), F32),
                        pltpu.VMEM((TS, 128), F32), pltpu.VMEM((TS, 128), F32),
                        pltpu.VMEM((TS // CHUNK, 128, CHUNK), F32),
                        pltpu.VMEM((TS, DN_W), F32), pltpu.VMEM((TS, DN_W), BF16),
                        pltpu.VMEM((TS, DN_W), BF16), pltpu.VMEM((TS, DN_W), BF16),
                        pltpu.VMEM((TS, DN_HEADS * CHUNK), BF16),
                        pltpu.VMEM((TS, DN_W), F32)],
        compiler_params=_params(("arbitrary",)),
        name="deltanet_bwd" if reverse else "deltanet_fwd",
    )(*args)


def _gla_kernel(reverse, has_prev, q_ref, k_ref, v_ref, sm_ref, w2_ref, b2_ref, s0_ref, *rest):
    if has_prev:
        oprev_ref, r_ref, nw_ref, o_ref, st_ref, s_scr, o_scr = rest
    else:
        o_ref, st_ref, s_scr, o_scr = rest
    t = _scan_tile(reverse)
    is_p, first, last, _, _ = _tile_info(t)

    @pl.when(last if reverse else first)
    def _():
        s_scr[...] = jnp.where(is_p, 0.0, s0_ref[...])

    logits = _dot(sm_ref[...].astype(BF16), w2_ref[...]) + b2_ref[...]
    gk = -_softplus(-logits) / GLA_NORMALIZER

    ti = lax.broadcasted_iota(jnp.int32, (TS, TS), 0)
    tj = lax.broadcasted_iota(jnp.int32, (TS, TS), 1)
    same = (ti // CHUNK) == (tj // CHUNK)
    tri = jnp.where(jnp.logical_and(same, (tj >= ti) if reverse else (tj <= ti)), 1.0, 0.0).astype(BF16)
    g1, g2, g3 = _split3(gk)
    gc = _dot(tri, g1) + _dot(tri, g2) + _dot(tri, g3)

    incl, _ = _tri_masks(reverse)
    nchunk = TS // CHUNK
    e_last = 0 if reverse else CHUNK - 1
    eg = jnp.exp(gc)
    q_all = q_ref[...] * (GLA_DK ** -0.5)
    k_all = k_ref[...]
    qg_all = (q_all * eg).astype(BF16)
    kg_all = (k_all * jnp.exp(-gc)).astype(BF16)
    v_all = v_ref[...].astype(BF16)

    probs = [(c, h) for c in range(nchunk) for h in range(GLA_HEADS)]
    rws = {c: slice(c * CHUNK, (c + 1) * CHUNK) for c in range(nchunk)}
    kls = {h: slice(h * GLA_DK, (h + 1) * GLA_DK) for h in range(GLA_HEADS)}
    vls = {h: slice(h * GLA_DV, (h + 1) * GLA_DV) for h in range(GLA_HEADS)}
    glast = {c: gc[c * CHUNK + e_last:c * CHUNK + e_last + 1, :] for c in range(nchunk)}
    kd = {c: (k_all[rws[c], :] * jnp.exp(glast[c] - gc[rws[c], :])).astype(BF16) for c in range(nchunk)}
    a = {p: jnp.where(incl, _dot_tb(qg_all[rws[p[0]], kls[p[1]]], kg_all[rws[p[0]], kls[p[1]]]), 0.0).astype(BF16)
         for p in probs}
    av = {p: _dot(a[p], v_all[rws[p[0]], vls[p[1]]]) for p in probs}
    up = {p: _dot_ta(v_all[rws[p[0]], vls[p[1]]], kd[p[0]][:, kls[p[1]]]) for p in probs}
    order = list(reversed(range(nchunk))) if reverse else list(range(nchunk))
    s_in = {}
    for h in range(GLA_HEADS):
        s = s_scr[h]
        for c in order:
            s_in[(c, h)] = s.astype(BF16)
            s = jnp.exp(glast[c][:, kls[h]]) * s + up[(c, h)]
        s_scr[h] = s
    for p in probs:
        c, h = p
        o_scr[rws[c], vls[h]] = _dot_tb(qg_all[rws[c], kls[h]], s_in[p]) + av[p]

    if has_prev:
        o_ref[...] = _gated_rms_out(o_scr[...] + oprev_ref[...], r_ref[...], nw_ref[...]).astype(o_ref.dtype)
    else:
        o_ref[...] = o_scr[...]

    @pl.when(is_p)
    def _():
        st_ref[...] = s_scr[...]


def _gla_call(pm, ps, reverse, w2pad, b2, s0t, prev=None, norm_w=None):
    tmap = (lambda i: NT - 1 - i) if reverse else (lambda i: i)
    qkw = GLA_HEADS * GLA_DK
    vw = GLA_HEADS * GLA_DV
    in_specs = [
        pl.BlockSpec((TS, qkw), lambda i: (tmap(i), C_GQ // qkw)),
        pl.BlockSpec((TS, qkw), lambda i: (tmap(i), C_GK // qkw)),
        pl.BlockSpec((TS, vw), lambda i: (tmap(i), C_GV // vw)),
        pl.BlockSpec((TS, 128), lambda i: (tmap(i), 0)),
        pl.BlockSpec((128, qkw), lambda i: (0, 0)),
        pl.BlockSpec((1, qkw), lambda i: (0, 0)),
        pl.BlockSpec((None, GLA_HEADS, GLA_DV, GLA_DK), lambda i: (_seq_of_tile(tmap(i)), 0, 0, 0)),
    ]
    args = [pm, pm, pm, ps, w2pad, b2, s0t]
    has_prev = prev is not None
    if has_prev:
        in_specs += [pl.BlockSpec((TS, vw), lambda i: (tmap(i), 0)),
                     pl.BlockSpec((TS, vw), lambda i: (tmap(i), C_GR // vw)),
                     pl.BlockSpec((1, 128), lambda i: (0, 0))]
        args += [prev, pm, norm_w.reshape(1, 128)]
    out_dtype = BF16 if has_prev else F32
    return pl.pallas_call(
        functools.partial(_gla_kernel, reverse, has_prev),
        grid=(NT,),
        in_specs=in_specs,
        out_specs=[pl.BlockSpec((TS, vw), lambda i: (tmap(i), 0)),
                   pl.BlockSpec((None, GLA_HEADS, GLA_DV, GLA_DK), lambda i: (jnp.minimum(tmap(i), NT_P - 1), 0, 0, 0))],
        out_shape=[jax.ShapeDtypeStruct((M, vw), out_dtype),
                   jax.ShapeDtypeStruct((BATCH, GLA_HEADS, GLA_DV, GLA_DK), F32)],
        scratch_shapes=[pltpu.VMEM((GLA_HEADS, GLA_DV, GLA_DK), F32),
                        pltpu.VMEM((TS, vw), F32)],
        compiler_params=_params(("arbitrary",)),
        name="gla_bwd" if reverse else "gla_fwd",
    )(*args)


MOD_SH1, MOD_SC1, MOD_GT1, MOD_SH2, MOD_SC2, MOD_GT2 = range(N_MOD)


def kernel(x_prompt, x_sample, cache_k, cache_v, state_delta, state_gla, c, c_ctx, w_mod, b_mod, norm1, w_in, pool_w, pool_scale, attn_sink, dn_conv, dn_a_log, dn_dt_bias, dn_norm, gla_w2, gla_b2, gla_norm, w_br_pool, w_br_attn, w_br_delta, w_br_gla, w_out, norm2, w_gate, w_up, w_down, norm_f):
    x = jnp.concatenate([x_prompt.reshape(MP, D), x_sample.reshape(MS, D)], axis=0)
    cond8 = jnp.concatenate([c_ctx[None, :], c, jnp.zeros((8 - 1 - DEC_BATCH, D), F32)], axis=0)
    mod = _modulation(cond8, w_mod, b_mod).reshape(DEPTH, 8, N_MOD, 1, D)
    cos_t, sin_t = _rope_tables()
    s_gla_t = jnp.swapaxes(state_gla, -1, -2)
    w_out_bf16 = w_out.astype(BF16)
    w_brs = (w_br_pool, w_br_attn, w_br_delta, w_br_gla)

    ks, vs, sds, sgs = [], [], [], []
    for l in range(DEPTH):
        h = _prenorm(x, norm1, mod, l, MOD_SC1, MOD_SH1)
        pm = _mix_proj(h, w_in, l)
        ps = _small_proj(h, w_in, l)

        ya = _pool(pm, pool_w[l], pool_scale[l])
        yb = jnp.concatenate([_ctx_attn(pm, attn_sink[l]),
                              _lat_attn(pm, attn_sink[l], cos_t, sin_t, cache_k, cache_v, l)], axis=0)

        alog = jnp.zeros((1, 128), F32).at[0, SM_A:SM_A + 8].set(dn_a_log[l].reshape(8))
        dtb = jnp.zeros((1, 128), F32).at[0, SM_A:SM_A + 8].set(dn_dt_bias[l].reshape(8))
        o_b, sd_b = _dn_call(pm, ps, True, l, dn_conv[l], alog, dtb, state_delta)
        yc, sd_f = _dn_call(pm, ps, False, l, dn_conv[l], alog, dtb, state_delta, prev=o_b, norm_w=dn_norm[l])

        w2pad = [jnp.zeros((128, GLA_HEADS * GLA_DK), F32).at[SM_LR + o * GLA_RANK:SM_LR + (o + 1) * GLA_RANK].set(gla_w2[l, o]).astype(BF16)
                 for o in range(2)]
        g_b, sg_b = _gla_call(pm, ps, True, w2pad[1], gla_b2[l, 1].reshape(1, -1), s_gla_t[:, l, 1])
        yd, sg_f = _gla_call(pm, ps, False, w2pad[0], gla_b2[l, 0].reshape(1, -1), s_gla_t[:, l, 0], prev=g_b, norm_w=gla_norm[l])

        merged = _merge(h, w_in, (ya, yb, yc, yd), w_brs, l)
        x, h2 = _out_proj(merged, w_out_bf16, x, mod, norm2, l, MOD_GT1, MOD_SC2, MOD_SH2)
        act = _ffn_up(h2, w_gate, w_up, l)
        x = _ffn_down(act, w_down, x, mod, l, MOD_GT2)

        ks.append(pm[:MP, C_K:C_K + KV_HEADS * HD].reshape(BATCH, SEQ, KV_HEADS, HD))
        vs.append(pm[:MP, C_V:C_V + KV_HEADS * HD].reshape(BATCH, SEQ, KV_HEADS, HD))
        sds.append(jnp.stack([sd_f, sd_b], axis=1))
        sgs.append(jnp.swapaxes(jnp.stack([sg_f, sg_b], axis=1), -1, -2))

    y = _finalnorm(x, norm_f)
    return (y[:MP].reshape(BATCH, SEQ, D), y[MP:].reshape(DEC_BATCH, DEC_SEQ, D),
            jnp.stack(ks, axis=1), jnp.stack(vs, axis=1), jnp.stack(sds, axis=1), jnp.stack(sgs, axis=1))
```

```python
import functools

import numpy as np
import jax
import jax.numpy as jnp
from jax import lax
from jax.experimental import pallas as pl
from jax.experimental.pallas import tpu as pltpu

F32 = jnp.float32
BF16 = jnp.bfloat16

D = 2048
BATCH = 16
SEQ = 256
DEPTH = 4
DEC_BATCH = 4
DEC_SEQ = 2048
PAST = 256
GRID_W = 64
HD = 128
ROPE_THETA = 10000.0
EPS = 1e-6
N_MOD = 6
POOL_WINDOWS = (2, 4, 8, 16)
POOL_W = 512
Q_HEADS = 8
KV_HEADS = 2
GROUP = 4
ATTN_WINDOW = 128
DN_HEADS = 4
DN_DK = 128
DN_W = 512
DN_CONV = 4
GLA_HEADS = 4
GLA_DK = 64
GLA_DV = 128
GLA_RANK = 16
GLA_NORMALIZER = 16.0
FFN = 5632
CHUNK = 64
PREP_CHUNKS = 4

MP = BATCH * SEQ
MS = DEC_BATCH * DEC_SEQ
M = MP + MS
TS = 256
NT_P = MP // TS
TPS = DEC_SEQ // TS
NT_S = MS // TS
NT = NT_P + NT_S
HALO = 8

C_Q, C_POOL, C_DNQKV, C_K, C_V, C_DNZ = 0, 1024, 1536, 3072, 3328, 3584
C_GQ, C_GK, C_GV, C_GR = 4096, 4352, 4608, 5120
MIXW = 5632
MIX_TN = 512
MIX_SRC = (1, 2, 0, 4, 5, 6, 3, 7, 8, 9, 10)
MIX_ALIGNED = 8
MIX_SHIFT = 16
SMALL_COL_A = 4096
SMALL_COL_LR = 5648
GATE_COL = 5680
IN_WIDTH = GATE_COL + 4 * D
SM_A, SM_B, SM_LR = 0, 8, 16

VMEM_LIMIT = 56 * 1024 * 1024


def _params(sem):
    return pltpu.CompilerParams(dimension_semantics=sem, vmem_limit_bytes=VMEM_LIMIT)


def _cond_row(row0):
    return jnp.where(row0 < MP, 0, 1 + (row0 - MP) // DEC_SEQ)


def _tile_info(t):
    is_p = t < NT_P
    s = jnp.maximum(t - NT_P, 0)
    j = s % TPS
    first = jnp.logical_or(is_p, j == 0)
    last = jnp.logical_or(is_p, j == TPS - 1)
    base = jnp.where(is_p, 0, j * TS)
    seq_len = jnp.where(is_p, SEQ, DEC_SEQ)
    return is_p, first, last, base, seq_len


def _silu(x):
    return x * jax.nn.sigmoid(x)


def _softplus(x):
    return jnp.maximum(x, 0.0) + jnp.log(1.0 + jnp.exp(-jnp.abs(x)))


def _dot(a, b):
    return jnp.dot(a, b, preferred_element_type=F32)


def _dot_tb(a, b):
    return lax.dot_general(a, b, (((1,), (1,)), ((), ())), preferred_element_type=F32)


def _dot_ta(a, b):
    return lax.dot_general(a, b, (((0,), (0,)), ((), ())), preferred_element_type=F32)


def _split2(x):
    hi = x.astype(BF16)
    lo = (x - hi.astype(F32)).astype(BF16)
    return hi, lo


def _split3(x):
    p1 = x.astype(BF16)
    r = x - p1.astype(F32)
    p2 = r.astype(BF16)
    p3 = (r - p2.astype(F32)).astype(BF16)
    return p1, p2, p3


def _dot_hp(a, b):
    ah, al = _split2(a)
    bh, bl = _split2(b)
    return _dot(ah, bh) + (_dot(ah, bl) + _dot(al, bh))


def _mod_kernel(c_ref, w_ref, b_ref, o_ref):
    a = _silu(c_ref[...]).astype(BF16)
    o_ref[...] = _dot(a, w_ref[...].astype(BF16)) + b_ref[...]


def _modulation(cond8, w_mod, b_mod):
    tn = 1024
    n = N_MOD * D
    return pl.pallas_call(
        _mod_kernel,
        grid=(DEPTH, n // tn),
        in_specs=[pl.BlockSpec((8, D), lambda l, j: (0, 0)),
                  pl.BlockSpec((None, D, tn), lambda l, j: (l, 0, j)),
                  pl.BlockSpec((None, 1, tn), lambda l, j: (l, 0, j))],
        out_specs=pl.BlockSpec((None, 8, tn), lambda l, j: (l, 0, j)),
        out_shape=jax.ShapeDtypeStruct((DEPTH, 8, n), F32),
        compiler_params=_params(("parallel", "parallel")),
        name="modulation",
    )(cond8, w_mod, b_mod.reshape(DEPTH, 1, n))


def _rms_rows(x, w):
    ms = jnp.mean(x * x, axis=-1, keepdims=True)
    return x * lax.rsqrt(ms + EPS) * w


def _prenorm_kernel(x_ref, nw_ref, sc_ref, sh_ref, h_ref):
    y = _rms_rows(x_ref[...], nw_ref[...])
    h_ref[...] = (y * (1.0 + sc_ref[...]) + sh_ref[...]).astype(h_ref.dtype)


def _mod_spec(layer, chunk, tm, tn=D, row_axis=0, col_axis=None):
    def index(*g):
        col = 0 if col_axis is None else g[col_axis]
        return (layer, _cond_row(g[row_axis] * tm), chunk, 0, col)
    return pl.BlockSpec((None, None, None, 1, tn), index)


def _prenorm(x, norm_w, mod, layer, sc_chunk, sh_chunk):
    tm = 512
    return pl.pallas_call(
        _prenorm_kernel,
        grid=(M // tm,),
        in_specs=[pl.BlockSpec((tm, D), lambda i: (i, 0)),
                  pl.BlockSpec((None, 1, D), lambda i: (layer, 0, 0)),
                  _mod_spec(layer, sc_chunk, tm), _mod_spec(layer, sh_chunk, tm)],
        out_specs=pl.BlockSpec((tm, D), lambda i: (i, 0)),
        out_shape=jax.ShapeDtypeStruct((M, D), BF16),
        compiler_params=_params(("parallel",)),
        name="prenorm",
    )(x, norm_w.reshape(DEPTH, 1, D), mod, mod)


def _finalnorm_kernel(x_ref, nw_ref, o_ref):
    o_ref[...] = _rms_rows(x_ref[...], nw_ref[...])


def _finalnorm(x, nw):
    tm = 512
    return pl.pallas_call(
        _finalnorm_kernel,
        grid=(M // tm,),
        in_specs=[pl.BlockSpec((tm, D), lambda i: (i, 0)),
                  pl.BlockSpec((1, D), lambda i: (0, 0))],
        out_specs=pl.BlockSpec((tm, D), lambda i: (i, 0)),
        out_shape=jax.ShapeDtypeStruct((M, D), F32),
        compiler_params=_params(("parallel",)),
        name="finalnorm",
    )(x, nw.reshape(1, D))


def _first_row_block():
    return pl.program_id(1) == 0


def _wt_rows(layer, nrows, row_fn):
    def index(*g):
        row = row_fn(*g)
        return (layer, row if isinstance(row, int) else pl.multiple_of(row, 8), 0)
    return pl.BlockSpec((pl.Element(1), pl.Element(nrows), pl.Element(D)), index)


def _mix_kernel(h_ref, w_ref, o_ref, wb_scr):
    @pl.when(_first_row_block())
    def _():
        wb_scr[...] = w_ref[0].astype(BF16)

    o_ref[...] = _dot_tb(h_ref[...], wb_scr[...])


def _mix_src_row(j):
    src = j
    for out_tile, src_tile in enumerate(MIX_SRC):
        src = jnp.where(j == out_tile, src_tile, src)
    return src * MIX_TN + jnp.where(src >= MIX_ALIGNED, MIX_SHIFT, 0)


def _mix_proj(h, w_in_t, layer):
    tm, tn = 2048, MIX_TN
    return pl.pallas_call(
        _mix_kernel,
        grid=(MIXW // tn, M // tm),
        in_specs=[pl.BlockSpec((tm, D), lambda j, i: (i, 0)),
                  _wt_rows(layer, tn, lambda j, i: _mix_src_row(j))],
        out_specs=pl.BlockSpec((tm, tn), lambda j, i: (i, j)),
        out_shape=jax.ShapeDtypeStruct((M, MIXW), F32),
        scratch_shapes=[pltpu.VMEM((tn, D), BF16)],
        compiler_params=_params(("arbitrary", "arbitrary")),
        name="mix_proj",
    )(h, w_in_t)


def _small_kernel(h_ref, wa_ref, wl_ref, o_ref, wb_scr):
    @pl.when(pl.program_id(0) == 0)
    def _():
        pad = jnp.zeros((128 - SM_LR - 2 * GLA_RANK, D), F32)
        wb_scr[...] = jnp.concatenate([wa_ref[0], wl_ref[0], pad], axis=0).astype(BF16)

    o_ref[...] = _dot_tb(h_ref[...], wb_scr[...])


def _small_proj(h, w_in_t, layer):
    tm = 1024
    return pl.pallas_call(
        _small_kernel,
        grid=(M // tm,),
        in_specs=[pl.BlockSpec((tm, D), lambda i: (i, 0)),
                  _wt_rows(layer, SM_LR, lambda i: SMALL_COL_A),
                  _wt_rows(layer, 2 * GLA_RANK, lambda i: SMALL_COL_LR)],
        out_specs=pl.BlockSpec((tm, 128), lambda i: (i, 0)),
        out_shape=jax.ShapeDtypeStruct((M, 128), F32),
        scratch_shapes=[pltpu.VMEM((128, D), BF16)],
        compiler_params=_params(("arbitrary",)),
        name="small_proj",
    )(h, w_in_t, w_in_t)


def _ffn_up_kernel(h_ref, wg_ref, wu_ref, o_ref, wg_scr, wu_scr):
    @pl.when(_first_row_block())
    def _():
        wg_scr[...] = wg_ref[...].astype(BF16)
        wu_scr[...] = wu_ref[...].astype(BF16)

    h = h_ref[...]
    o_ref[...] = (_silu(_dot(h, wg_scr[...])) * _dot(h, wu_scr[...])).astype(o_ref.dtype)


def _ffn_up(h, w_gate, w_up, layer):
    tm, tn = 2048, 512
    wspec = pl.BlockSpec((None, D, tn), lambda j, i: (layer, 0, j))
    return pl.pallas_call(
        _ffn_up_kernel,
        grid=(FFN // tn, M // tm),
        in_specs=[pl.BlockSpec((tm, D), lambda j, i: (i, 0)), wspec, wspec],
        out_specs=pl.BlockSpec((tm, tn), lambda j, i: (i, j)),
        out_shape=jax.ShapeDtypeStruct((M, FFN), BF16),
        scratch_shapes=[pltpu.VMEM((D, tn), BF16), pltpu.VMEM((D, tn), BF16)],
        compiler_params=_params(("arbitrary", "arbitrary")),
        name="ffn_up",
    )(h, w_gate, w_up)


def _ffn_down_kernel(a_ref, w_ref, x_ref, gt_ref, o_ref, wb_scr):
    @pl.when(_first_row_block())
    def _():
        wb_scr[...] = w_ref[...].astype(BF16)

    o_ref[...] = x_ref[...] + gt_ref[...] * _dot(a_ref[...], wb_scr[...])


def _ffn_down(act, w_down, x, mod, layer, gt_chunk):
    tm, tn = 512, 512
    return pl.pallas_call(
        _ffn_down_kernel,
        grid=(D // tn, M // tm),
        in_specs=[pl.BlockSpec((tm, FFN), lambda j, i: (i, 0)),
                  pl.BlockSpec((None, FFN, tn), lambda j, i: (layer, 0, j)),
                  pl.BlockSpec((tm, tn), lambda j, i: (i, j)),
                  _mod_spec(layer, gt_chunk, tm, tn, row_axis=1, col_axis=0)],
        out_specs=pl.BlockSpec((tm, tn), lambda j, i: (i, j)),
        out_shape=jax.ShapeDtypeStruct((M, D), F32),
        scratch_shapes=[pltpu.VMEM((FFN, tn), BF16)],
        compiler_params=_params(("arbitrary", "arbitrary")),
        name="ffn_down",
    )(act, w_down, x, mod)


BR_WIDTHS = (POOL_W, Q_HEADS * HD, DN_W, GLA_HEADS * GLA_DV)


def _merge_kernel(h_ref, *refs):
    g_refs, y_refs, w_refs, o_ref = refs[0:4], refs[4:8], refs[8:12], refs[12]
    h = h_ref[...]
    acc = None
    for b in range(4):
        term = jax.nn.sigmoid(_dot_tb(h, g_refs[b][...])) * _dot(y_refs[b][...], w_refs[b][...])
        acc = term if acc is None else acc + term
    o_ref[...] = acc.astype(o_ref.dtype)


def _merge(h, gates_t, ys, w_brs, layer):
    tm, tn = 1024, 512
    nj = D // tn
    gate_specs = [pl.BlockSpec((None, tn, D), functools.partial(lambda i, j, b: (layer, b * nj + j, 0), b=b))
                  for b in range(4)]
    y_specs = [pl.BlockSpec((tm, w), lambda i, j: (i, 0)) for w in BR_WIDTHS]
    w_specs = [pl.BlockSpec((None, w, tn), lambda i, j: (layer, 0, j)) for w in BR_WIDTHS]
    return pl.pallas_call(
        _merge_kernel,
        grid=(M // tm, nj),
        in_specs=[pl.BlockSpec((tm, D), lambda i, j: (i, 0))] + gate_specs + y_specs + w_specs,
        out_specs=pl.BlockSpec((tm, tn), lambda i, j: (i, j)),
        out_shape=jax.ShapeDtypeStruct((M, D), BF16),
        compiler_params=_params(("parallel", "arbitrary")),
        name="merge",
    )(h, *([gates_t] * 4), *ys, *w_brs)


def _out_proj_kernel(m_ref, w_ref, x_ref, gt_ref, nw_ref, sc_ref, sh_ref, xo_ref, h_ref):
    xn = x_ref[...] + gt_ref[...] * _dot(m_ref[...], w_ref[...])
    xo_ref[...] = xn
    h_ref[...] = (_rms_rows(xn, nw_ref[...]) * (1.0 + sc_ref[...]) + sh_ref[...]).astype(h_ref.dtype)


def _out_proj(merged, w_out_bf16, x, mod, norm_w, layer, gt_chunk, sc_chunk, sh_chunk):
    tm = 512
    row = pl.BlockSpec((tm, D), lambda i: (i, 0))
    return pl.pallas_call(
        _out_proj_kernel,
        grid=(M // tm,),
        in_specs=[row,
                  pl.BlockSpec((None, D, D), lambda i: (layer, 0, 0), pipeline_mode=pl.Buffered(1)),
                  row, _mod_spec(layer, gt_chunk, tm),
                  pl.BlockSpec((None, 1, D), lambda i: (layer, 0, 0)),
                  _mod_spec(layer, sc_chunk, tm), _mod_spec(layer, sh_chunk, tm)],
        out_specs=[row, row],
        out_shape=[jax.ShapeDtypeStruct((M, D), F32), jax.ShapeDtypeStruct((M, D), BF16)],
        compiler_params=_params(("parallel",)),
        name="out_proj",
    )(merged, w_out_bf16, x, mod, norm_w.reshape(DEPTH, 1, D), mod, mod)


def _pool_kernel(u_ref, up_ref, un_ref, w_ref, sc_ref, o_ref):
    t = pl.program_id(0)
    _, first, last, base, seq_len = _tile_info(t)
    u = u_ref[...]
    prev = jnp.where(first, 0.0, up_ref[...])
    nxt = jnp.where(last, 0.0, un_ref[...])
    ext = jnp.concatenate([prev, u, nxt], axis=0)
    pos = base + lax.broadcasted_iota(jnp.int32, (TS, 1), 0)
    outs = []
    for gi, win in enumerate(POOL_WINDOWS):
        lanes = slice(gi * 128, (gi + 1) * 128)
        p = ext[:, lanes]
        s = 1
        while s < win:
            p = p + pltpu.roll(p, s, 0)
            s *= 2
        d = HALO + win // 2 - 1
        ssum = p[d:d + TS]
        cnt = (jnp.minimum(pos + win // 2, seq_len) - jnp.maximum(pos - win // 2, 0)).astype(F32)
        pg = ssum / cnt - u[:, lanes]
        outs.append(_dot(pg.astype(BF16), w_ref[gi].astype(BF16)))
    o_ref[...] = (jnp.concatenate(outs, axis=1) * sc_ref[...]).astype(o_ref.dtype)


def _halo_specs(width, colblk):
    r = TS // HALO
    nblk = M // HALO
    prev = pl.BlockSpec((HALO, width), lambda t: (jnp.maximum(t * r - 1, 0), colblk))
    nxt = pl.BlockSpec((HALO, width), lambda t: (jnp.minimum((t + 1) * r, nblk - 1), colblk))
    return prev, nxt


def _pool(pm, pool_w, pool_scale):
    cb = C_POOL // POOL_W
    prev, nxt = _halo_specs(POOL_W, cb)
    return pl.pallas_call(
        _pool_kernel,
        grid=(NT,),
        in_specs=[pl.BlockSpec((TS, POOL_W), lambda t: (t, cb)), prev, nxt,
                  pl.BlockSpec((4, 128, 128), lambda t: (0, 0, 0)),
                  pl.BlockSpec((1, POOL_W), lambda t: (0, 0))],
        out_specs=pl.BlockSpec((TS, POOL_W), lambda t: (t, 0)),
        out_shape=jax.ShapeDtypeStruct((M, POOL_W), BF16),
        compiler_params=_params(("parallel",)),
        name="pool",
    )(pm, pm, pm, pool_w, pool_scale.reshape(1, POOL_W))


def _ctx_attn_kernel(sink_ref, q_ref, k_ref, v_ref, o_ref):
    q = q_ref[...] * (HD ** -0.5)
    k = k_ref[...].astype(BF16)
    v = v_ref[...].astype(BF16)
    for n in range(KV_HEADS):
        kn = k[:, n * HD:(n + 1) * HD]
        vn = v[:, n * HD:(n + 1) * HD]
        for g in range(GROUP):
            h = n * GROUP + g
            qh = q[:, h * HD:(h + 1) * HD].astype(BF16)
            s = _dot_tb(qh, kn)
            snk = sink_ref[h]
            m = jnp.maximum(jnp.max(s, axis=-1, keepdims=True), snk)
            e = jnp.exp(s - m)
            den = jnp.sum(e, axis=-1, keepdims=True) + jnp.exp(snk - m)
            o = _dot(e.astype(BF16), vn) / den
            o_ref[:, h * HD:(h + 1) * HD] = o.astype(o_ref.dtype)


def _ctx_attn(pm, sink):
    qw, kw = Q_HEADS * HD, KV_HEADS * HD
    return pl.pallas_call(
        _ctx_attn_kernel,
        grid=(NT_P,),
        in_specs=[pl.BlockSpec(memory_space=pltpu.SMEM),
                  pl.BlockSpec((TS, qw), lambda t: (t, C_Q // qw)),
                  pl.BlockSpec((TS, kw), lambda t: (t, C_K // kw)),
                  pl.BlockSpec((TS, kw), lambda t: (t, C_V // kw))],
        out_specs=pl.BlockSpec((TS, qw), lambda t: (t, 0)),
        out_shape=jax.ShapeDtypeStruct((MP, qw), BF16),
        compiler_params=_params(("parallel",)),
        name="ctx_attn",
    )(sink, pm, pm, pm)


def _rope(x, cs, sn):
    lane = lax.broadcasted_iota(jnp.int32, x.shape, 1)
    swapped = jnp.where((lane % 64) < 32, pltpu.roll(x, 96, 1), pltpu.roll(x, 32, 1))
    return x * cs + swapped * sn


def _lat_attn_kernel(sink_ref, q_ref, km_ref, kp_ref, kn_ref, vm_ref, vp_ref, vn_ref,
                     cm_ref, cp_ref, cn_ref, sm_ref, sp_ref, sn_ref, kc_ref, vc_ref, o_ref):
    ts = pl.program_id(0)
    base = (ts % TPS) * TS
    half = TS // 2
    nk = TS + 2 * half
    q = q_ref[...] * (HD ** -0.5)
    cq, sq = cm_ref[...], sm_ref[...]
    kl = jnp.concatenate([kp_ref[...], km_ref[...], kn_ref[...]], axis=0)
    vl = jnp.concatenate([vp_ref[...], vm_ref[...], vn_ref[...]], axis=0).astype(BF16)
    ck = jnp.concatenate([cp_ref[...], cq, cn_ref[...]], axis=0)
    sk = jnp.concatenate([sp_ref[...], sq, sn_ref[...]], axis=0)
    kc = kc_ref[...].astype(BF16)
    vc = vc_ref[...].astype(BF16)
    qpos = base + lax.broadcasted_iota(jnp.int32, (TS, 1), 0)
    kpos = base - half + lax.broadcasted_iota(jnp.int32, (1, nk), 1)
    dist = qpos - kpos
    ok = (jnp.abs(dist) <= ATTN_WINDOW).astype(jnp.int32) * (kpos >= 0).astype(jnp.int32) * (kpos < DEC_SEQ).astype(jnp.int32)
    bias = jnp.where(ok > 0, 0.0, -1e30)
    for n in range(KV_HEADS):
        lanes = slice(n * HD, (n + 1) * HD)
        kr = _rope(kl[:, lanes], ck, sk).astype(BF16)
        vln, kcn, vcn = vl[:, lanes], kc[:, lanes], vc[:, lanes]
        for g in range(GROUP):
            h = n * GROUP + g
            qh = _rope(q[:, h * HD:(h + 1) * HD], cq, sq).astype(BF16)
            s_c = _dot_tb(qh, kcn)
            s_l = _dot_tb(qh, kr) + bias
            snk = sink_ref[h]
            m = jnp.maximum(jnp.maximum(jnp.max(s_c, axis=-1, keepdims=True),
                                        jnp.max(s_l, axis=-1, keepdims=True)), snk)
            e_c = jnp.exp(s_c - m)
            e_l = jnp.exp(s_l - m)
            den = jnp.sum(e_c, axis=-1, keepdims=True) + jnp.sum(e_l, axis=-1, keepdims=True) + jnp.exp(snk - m)
            o = (_dot(e_c.astype(BF16), vcn) + _dot(e_l.astype(BF16), vln)) / den
            o_ref[:, h * HD:(h + 1) * HD] = o.astype(o_ref.dtype)


def _lat_attn(pm, sink, cos_t, sin_t, cache_k, cache_v, layer):
    qw, kw = Q_HEADS * HD, KV_HEADS * HD
    half = TS // 2
    nhb = M // half
    nhs = DEC_SEQ // half

    def main(w, cb):
        return pl.BlockSpec((TS, w), lambda ts: (NT_P + ts, cb))

    def prev(cb):
        return pl.BlockSpec((half, kw), lambda ts: (2 * (NT_P + ts) - 1, cb))

    def nxt(cb):
        return pl.BlockSpec((half, kw), lambda ts: (jnp.minimum(2 * (NT_P + ts) + 2, nhb - 1), cb))

    tab_m = pl.BlockSpec((TS, HD), lambda ts: (ts % TPS, 0))
    tab_p = pl.BlockSpec((half, HD), lambda ts: (jnp.maximum(2 * (ts % TPS) - 1, 0), 0))
    tab_n = pl.BlockSpec((half, HD), lambda ts: (jnp.minimum(2 * (ts % TPS) + 2, nhs - 1), 0))
    ctx = pl.BlockSpec((None, None, PAST, kw), lambda ts: (ts // TPS, layer, 0, 0))
    return pl.pallas_call(
        _lat_attn_kernel,
        grid=(NT_S,),
        in_specs=[pl.BlockSpec(memory_space=pltpu.SMEM),
                  main(qw, C_Q // qw),
                  main(kw, C_K // kw), prev(C_K // kw), nxt(C_K // kw),
                  main(kw, C_V // kw), prev(C_V // kw), nxt(C_V // kw),
                  tab_m, tab_p, tab_n, tab_m, tab_p, tab_n, ctx, ctx],
        out_specs=pl.BlockSpec((TS, qw), lambda ts: (ts, 0)),
        out_shape=jax.ShapeDtypeStruct((MS, qw), BF16),
        compiler_params=_params(("parallel",)),
        name="lat_attn",
    )(sink, pm, pm, pm, pm, pm, pm, pm, cos_t, cos_t, cos_t, sin_t, sin_t, sin_t,
      cache_k.reshape(DEC_BATCH, DEPTH, PAST, kw), cache_v.reshape(DEC_BATCH, DEPTH, PAST, kw))


def _rope_tables():
    quarter = HD // 4
    inv = ROPE_THETA ** (-jnp.arange(quarter, dtype=F32) / quarter)
    t = jnp.arange(DEC_SEQ)
    row = (t // GRID_W).astype(F32)
    col = (t % GRID_W).astype(F32)
    ar = row[:, None] * inv[None, :]
    ac = col[:, None] * inv[None, :]
    cos_t = jnp.concatenate([jnp.cos(ar), jnp.cos(ar), jnp.cos(ac), jnp.cos(ac)], axis=-1)
    sin_t = jnp.concatenate([-jnp.sin(ar), jnp.sin(ar), -jnp.sin(ac), jnp.sin(ac)], axis=-1)
    return cos_t, sin_t


def _scan_tile(reverse):
    i = pl.program_id(0)
    return (NT - 1 - i) if reverse else i


def _tri_masks(reverse):
    ii = lax.broadcasted_iota(jnp.int32, (CHUNK, CHUNK), 0)
    jj = lax.broadcasted_iota(jnp.int32, (CHUNK, CHUNK), 1)
    incl = (jj >= ii) if reverse else (jj <= ii)
    strict = (jj > ii) if reverse else (jj < ii)
    return incl, strict


def _gated_rms_out(o, gate, nw):
    outs = []
    for h in range(4):
        lanes = slice(h * 128, (h + 1) * 128)
        outs.append(_rms_rows(o[:, lanes], nw) * _silu(gate[:, lanes]))
    return jnp.concatenate(outs, axis=1)


def _dot_hp_all(xs, ys):
    hh = [_dot(x[0], y[0]) for x, y in zip(xs, ys)]
    hl = [_dot(x[0], y[1]) for x, y in zip(xs, ys)]
    lh = [_dot(x[1], y[0]) for x, y in zip(xs, ys)]
    return [a + (b + c) for a, b, c in zip(hh, hl, lh)]


def _inv_unit_tri_all(lms):
    ii = lax.broadcasted_iota(jnp.int32, (CHUNK, CHUNK), 0)
    jj = lax.broadcasted_iota(jnp.int32, (CHUNK, CHUNK), 1)
    eye = jnp.where(ii == jj, 1.0, 0.0)
    ps = [eye - lm for lm in lms]
    mbs = [lm.astype(BF16) for lm in lms]
    mks = [_dot(m, m) for m in mbs]
    span = 4
    while span <= CHUNK:
        mbs = [m.astype(BF16) for m in mks]
        incs = [_dot(p.astype(BF16), m) for p, m in zip(ps, mbs)]
        if span < CHUNK:
            mks = [_dot(m, m) for m in mbs]
        ps = [p + inc for p, inc in zip(ps, incs)]
        span *= 2
    lts = _dot_hp_all([_split2(lm) for lm in lms], [_split2(t) for t in ps])
    rs = [(eye - t) - lt for t, lt in zip(ps, lts)]
    return [t + _dot(t.astype(BF16), r.astype(BF16)) for t, r in zip(ps, rs)]


def _dn_kernel(reverse, has_prev, x_ref, xp_ref, xn_ref, sm_ref, cw_ref, alog_ref, dtb_ref, s0_ref, *rest):
    if has_prev:
        oprev_ref, z_ref, nw_ref = rest[:3]
        rest = rest[3:]
    (o_ref, st_ref, s_scr, q_scr, k_scr, v_scr, b_scr, gcc_scr, gcr_scr,
     u_scr, w_scr, qe_scr, kd_scr, a_scr, o_scr) = rest
    t = _scan_tile(reverse)
    is_p, first, last, _, _ = _tile_info(t)
    d = 1 if reverse else 0
    nchunk = TS // CHUNK
    e_last = 0 if reverse else CHUNK - 1

    @pl.when(last if reverse else first)
    def _():
        s_scr[...] = jnp.where(is_p, 0.0, s0_ref[...])

    x = x_ref[...]
    ext = jnp.concatenate([jnp.where(first, 0.0, xp_ref[...]), x, jnp.where(last, 0.0, xn_ref[...])], axis=0)
    cw = cw_ref[...]
    ext_rows = TS + 2 * HALO
    conv = None
    for j in range(DN_CONV):
        off = j - DN_CONV // 2
        rows = ext if off == 0 else pltpu.roll(ext, (ext_rows - off) % ext_rows, 0)
        term = rows[HALO:HALO + TS] * cw[j:j + 1, :]
        conv = term if conv is None else conv + term
    act = _silu(conv)
    for h in range(DN_HEADS):
        lanes = slice(h * 128, (h + 1) * 128)
        qh = act[:, h * 128:(h + 1) * 128]
        kh = act[:, DN_W + h * 128:DN_W + (h + 1) * 128]
        q_scr[:, lanes] = qh * lax.rsqrt(jnp.sum(qh * qh, axis=-1, keepdims=True) + EPS) * (DN_DK ** -0.5)
        k_scr[:, lanes] = kh * lax.rsqrt(jnp.sum(kh * kh, axis=-1, keepdims=True) + EPS)
    v_scr[...] = act[:, 2 * DN_W:3 * DN_W]
    sm = sm_ref[...]
    b_scr[...] = jax.nn.sigmoid(sm)
    g = -jnp.exp(alog_ref[...]) * _softplus(sm + dtb_ref[...])

    ti = lax.broadcasted_iota(jnp.int32, (TS, TS), 0)
    tj = lax.broadcasted_iota(jnp.int32, (TS, TS), 1)
    same = (ti // CHUNK) == (tj // CHUNK)
    lower = jnp.where(jnp.logical_and(same, tj <= ti), 1.0, 0.0).astype(BF16)
    upper = jnp.where(jnp.logical_and(same, tj >= ti), 1.0, 0.0).astype(BF16)
    tri, tri_t = (upper, lower) if reverse else (lower, upper)
    g1, g2, g3 = _split3(g)
    gcc_scr[...] = _dot(tri, g1) + _dot(tri, g2) + _dot(tri, g3)
    gcr = _dot_ta(g1, tri_t) + _dot_ta(g2, tri_t) + _dot_ta(g3, tri_t)
    for c in range(nchunk):
        gcr_scr[c] = gcr[:, c * CHUNK:(c + 1) * CHUNK]

    incl, strict = _tri_masks(reverse)

    def prep_body(pi, carry):
        rows, gblk, gcr_c, bt = {}, {}, {}, {}
        for cc in range(PREP_CHUNKS):
            c = pi * PREP_CHUNKS + cc
            rows[cc] = pl.ds(pl.multiple_of(c * CHUNK, CHUNK), CHUNK)
            gblk[cc] = gcc_scr[rows[cc], :]
            gcr_c[cc] = gcr_scr[c]
            bt[cc] = b_scr[rows[cc], :]
        info = []
        for cc in range(PREP_CHUNKS):
            for h in range(DN_HEADS):
                lanes = slice(h * 128, (h + 1) * 128)
                r = SM_A + d * DN_HEADS + h
                gcol = gblk[cc][:, r:r + 1]
                grow = gcr_c[cc][r:r + 1, :]
                glast = gblk[cc][e_last:e_last + 1, r:r + 1]
                beta = bt[cc][:, SM_B + d * DN_HEADS + h:SM_B + d * DN_HEADS + h + 1]
                qc, kc, vc = q_scr[rows[cc], lanes], k_scr[rows[cc], lanes], v_scr[rows[cc], lanes]
                decay = jnp.exp(jnp.where(incl, gcol - grow, -1e30))
                kb = kc * beta
                eg = jnp.exp(gcol)
                qe_scr[rows[cc], lanes] = (qc * eg).astype(BF16)
                kd_scr[rows[cc], lanes] = (kc * jnp.exp(glast - gcol)).astype(BF16)
                info.append(dict(cc=cc, h=h, lanes=lanes, decay=decay, kbb=kb.astype(BF16), kcb=kc.astype(BF16),
                                 qcb=qc.astype(BF16), vbb=(vc * beta).astype(BF16), kgb=(kb * eg).astype(BF16)))
        kk = [_dot_tb(p["kbb"], p["kcb"]) for p in info]
        qk = [_dot_tb(p["qcb"], p["kcb"]) for p in info]
        for p, m in zip(info, qk):
            a_scr[rows[p["cc"]], p["h"] * CHUNK:(p["h"] + 1) * CHUNK] = (m * p["decay"]).astype(BF16)
        lms = [jnp.where(strict, m * p["decay"], 0.0) for p, m in zip(info, kk)]
        tinvs = [tv.astype(BF16) for tv in _inv_unit_tri_all(lms)]
        us = [_dot(tv, p["vbb"]) for tv, p in zip(tinvs, info)]
        ws = [_dot(tv, p["kgb"]) for tv, p in zip(tinvs, info)]
        for p, u, w in zip(info, us, ws):
            u_scr[rows[p["cc"]], p["lanes"]] = u
            w_scr[rows[p["cc"]], p["lanes"]] = w.astype(BF16)
        return carry

    lax.fori_loop(0, nchunk // PREP_CHUNKS, prep_body, 0)

    def chunk_body(ci, carry):
        c = (nchunk - 1 - ci) if reverse else ci
        rows = pl.ds(pl.multiple_of(c * CHUNK, CHUNK), CHUNK)
        glast_row = gcc_scr[pl.ds(c * CHUNK + e_last, 1), :]
        heads = range(DN_HEADS)
        lanes = [slice(h * 128, (h + 1) * 128) for h in heads]
        s_old = [s_scr[h] for h in heads]
        sb = [s.astype(BF16) for s in s_old]
        ws = [_dot(w_scr[rows, lanes[h]], sb[h]) for h in heads]
        qs = [_dot(qe_scr[rows, lanes[h]], sb[h]) for h in heads]
        vnb = [(u_scr[rows, lanes[h]] - ws[h]).astype(BF16) for h in heads]
        av = [_dot(a_scr[rows, h * CHUNK:(h + 1) * CHUNK], vnb[h]) for h in heads]
        up = [_dot_ta(kd_scr[rows, lanes[h]], vnb[h]) for h in heads]
        for h in heads:
            r = SM_A + d * DN_HEADS + h
            o_scr[rows, lanes[h]] = qs[h] + av[h]
            s_scr[h] = jnp.exp(glast_row[:, r:r + 1]) * s_old[h] + up[h]
        return carry

    lax.fori_loop(0, nchunk, chunk_body, 0)

    if has_prev:
        o_ref[...] = _gated_rms_out(o_scr[...] + oprev_ref[...], z_ref[...], nw_ref[...]).astype(o_ref.dtype)
    else:
        o_ref[...] = o_scr[...]

    @pl.when(is_p)
    def _():
        st_ref[...] = s_scr[...]


def _seq_of_tile(t):
    return jnp.maximum(t - NT_P, 0) // TPS


def _dn_call(pm, ps, reverse, layer, conv_w, alog, dtb, state_delta, prev=None, norm_w=None):
    d = 1 if reverse else 0
    tmap = (lambda i: NT - 1 - i) if reverse else (lambda i: i)
    width = 3 * DN_W
    cb = C_DNQKV // width
    r = TS // HALO
    nblk = M // HALO
    in_specs = [
        pl.BlockSpec((TS, width), lambda i: (tmap(i), cb)),
        pl.BlockSpec((HALO, width), lambda i: (jnp.maximum(tmap(i) * r - 1, 0), cb)),
        pl.BlockSpec((HALO, width), lambda i: (jnp.minimum((tmap(i) + 1) * r, nblk - 1), cb)),
        pl.BlockSpec((TS, 128), lambda i: (tmap(i), 0)),
        pl.BlockSpec((DN_CONV, width), lambda i: (0, 0)),
        pl.BlockSpec((1, 128), lambda i: (0, 0)),
        pl.BlockSpec((1, 128), lambda i: (0, 0)),
        pl.BlockSpec((None, None, None, DN_HEADS, DN_DK, 128), lambda i: (_seq_of_tile(tmap(i)), layer, d, 0, 0, 0)),
    ]
    args = [pm, pm, pm, ps, conv_w, alog, dtb, state_delta]
    has_prev = prev is not None
    if has_prev:
        in_specs += [pl.BlockSpec((TS, DN_W), lambda i: (tmap(i), 0)),
                     pl.BlockSpec((TS, DN_W), lambda i: (tmap(i), C_DNZ // DN_W)),
                     pl.BlockSpec((1, 128), lambda i: (0, 0))]
        args += [prev, pm, norm_w.reshape(1, 128)]
    out_dtype = BF16 if has_prev else F32
    return pl.pallas_call(
        functools.partial(_dn_kernel, reverse, has_prev),
        grid=(NT,),
        in_specs=in_specs,
        out_specs=[pl.BlockSpec((TS, DN_W), lambda i: (tmap(i), 0)),
                   pl.BlockSpec((None, DN_HEADS, DN_DK, 128), lambda i: (jnp.minimum(tmap(i), NT_P - 1), 0, 0, 0))],
        out_shape=[jax.ShapeDtypeStruct((M, DN_W), out_dtype),
                   jax.ShapeDtypeStruct((BATCH, DN_HEADS, DN_DK, 128), F32)],
        scratch_shapes=[pltpu.VMEM((DN_HEADS, DN_DK, 128), F32),
                        pltpu.VMEM((TS, DN_W), F32), pltpu.VMEM((TS, DN_W), F32), pltpu.VMEM((TS, DN_W), F32),
                        pltpu.VMEM((TS, 128), F32), pltpu.VMEM((TS, 128), F32),
                        pltpu.VMEM((TS // CHUNK, 128, CHUNK), F32),
                        pltpu.VMEM((TS, DN_W), F32), pltpu.VMEM((TS, DN_W), BF16),
                        pltpu.VMEM((TS, DN_W), BF16), pltpu.VMEM((TS, DN_W), BF16),
                        pltpu.VMEM((TS, DN_HEADS * CHUNK), BF16),
                        pltpu.VMEM((TS, DN_W), F32)],
        compiler_params=_params(("arbitrary",)),
        name="deltanet_bwd" if reverse else "deltanet_fwd",
    )(*args)


def _gla_kernel(reverse, has_prev, q_ref, k_ref, v_ref, sm_ref, w2_ref, b2_ref, s0_ref, *rest):
    if has_prev:
        oprev_ref, r_ref, nw_ref, o_ref, st_ref, s_scr, o_scr = rest
    else:
        o_ref, st_ref, s_scr, o_scr = rest
    t = _scan_tile(reverse)
    is_p, first, last, _, _ = _tile_info(t)

    @pl.when(last if reverse else first)
    def _():
        s_scr[...] = jnp.where(is_p, 0.0, s0_ref[...])

    logits = _dot(sm_ref[...].astype(BF16), w2_ref[...]) + b2_ref[...]
    gk = -_softplus(-logits) / GLA_NORMALIZER

    ti = lax.broadcasted_iota(jnp.int32, (TS, TS), 0)
    tj = lax.broadcasted_iota(jnp.int32, (TS, TS), 1)
    same = (ti // CHUNK) == (tj // CHUNK)
    tri = jnp.where(jnp.logical_and(same, (tj >= ti) if reverse else (tj <= ti)), 1.0, 0.0).astype(BF16)
    g1, g2, g3 = _split3(gk)
    gc = _dot(tri, g1) + _dot(tri, g2) + _dot(tri, g3)

    incl, _ = _tri_masks(reverse)
    nchunk = TS // CHUNK
    e_last = 0 if reverse else CHUNK - 1
    eg = jnp.exp(gc)
    q_all = q_ref[...] * (GLA_DK ** -0.5)
    k_all = k_ref[...]
    qg_all = (q_all * eg).astype(BF16)
    kg_all = (k_all * jnp.exp(-gc)).astype(BF16)
    v_all = v_ref[...].astype(BF16)

    probs = [(c, h) for c in range(nchunk) for h in range(GLA_HEADS)]
    rws = {c: slice(c * CHUNK, (c + 1) * CHUNK) for c in range(nchunk)}
    kls = {h: slice(h * GLA_DK, (h + 1) * GLA_DK) for h in range(GLA_HEADS)}
    vls = {h: slice(h * GLA_DV, (h + 1) * GLA_DV) for h in range(GLA_HEADS)}
    glast = {c: gc[c * CHUNK + e_last:c * CHUNK + e_last + 1, :] for c in range(nchunk)}
    kd = {c: (k_all[rws[c], :] * jnp.exp(glast[c] - gc[rws[c], :])).astype(BF16) for c in range(nchunk)}
    a = {p: jnp.where(incl, _dot_tb(qg_all[rws[p[0]], kls[p[1]]], kg_all[rws[p[0]], kls[p[1]]]), 0.0).astype(BF16)
         for p in probs}
    av = {p: _dot(a[p], v_all[rws[p[0]], vls[p[1]]]) for p in probs}
    up = {p: _dot_ta(v_all[rws[p[0]], vls[p[1]]], kd[p[0]][:, kls[p[1]]]) for p in probs}
    order = list(reversed(range(nchunk))) if reverse else list(range(nchunk))
    s_in = {}
    for h in range(GLA_HEADS):
        s = s_scr[h]
        for c in order:
            s_in[(c, h)] = s.astype(BF16)
            s = jnp.exp(glast[c][:, kls[h]]) * s + up[(c, h)]
        s_scr[h] = s
    for p in probs:
        c, h = p
        o_scr[rws[c], vls[h]] = _dot_tb(qg_all[rws[c], kls[h]], s_in[p]) + av[p]

    if has_prev:
        o_ref[...] = _gated_rms_out(o_scr[...] + oprev_ref[...], r_ref[...], nw_ref[...]).astype(o_ref.dtype)
    else:
        o_ref[...] = o_scr[...]

    @pl.when(is_p)
    def _():
        st_ref[...] = s_scr[...]


def _gla_call(pm, ps, reverse, w2pad, b2, s0t, prev=None, norm_w=None):
    tmap = (lambda i: NT - 1 - i) if reverse else (lambda i: i)
    qkw = GLA_HEADS * GLA_DK
    vw = GLA_HEADS * GLA_DV
    in_specs = [
        pl.BlockSpec((TS, qkw), lambda i: (tmap(i), C_GQ // qkw)),
        pl.BlockSpec((TS, qkw), lambda i: (tmap(i), C_GK // qkw)),
        pl.BlockSpec((TS, vw), lambda i: (tmap(i), C_GV // vw)),
        pl.BlockSpec((TS, 128), lambda i: (tmap(i), 0)),
        pl.BlockSpec((128, qkw), lambda i: (0, 0)),
        pl.BlockSpec((1, qkw), lambda i: (0, 0)),
        pl.BlockSpec((None, GLA_HEADS, GLA_DV, GLA_DK), lambda i: (_seq_of_tile(tmap(i)), 0, 0, 0)),
    ]
    args = [pm, pm, pm, ps, w2pad, b2, s0t]
    has_prev = prev is not None
    if has_prev:
        in_specs += [pl.BlockSpec((TS, vw), lambda i: (tmap(i), 0)),
                     pl.BlockSpec((TS, vw), lambda i: (tmap(i), C_GR // vw)),
                     pl.BlockSpec((1, 128), lambda i: (0, 0))]
        args += [prev, pm, norm_w.reshape(1, 128)]
    out_dtype = BF16 if has_prev else F32
    return pl.pallas_call(
        functools.partial(_gla_kernel, reverse, has_prev),
        grid=(NT,),
        in_specs=in_specs,
        out_specs=[pl.BlockSpec((TS, vw), lambda i: (tmap(i), 0)),
                   pl.BlockSpec((None, GLA_HEADS, GLA_DV, GLA_DK), lambda i: (jnp.minimum(tmap(i), NT_P - 1), 0, 0, 0))],
        out_shape=[jax.ShapeDtypeStruct((M, vw), out_dtype),
                   jax.ShapeDtypeStruct((BATCH, GLA_HEADS, GLA_DV, GLA_DK), F32)],
        scratch_shapes=[pltpu.VMEM((GLA_HEADS, GLA_DV, GLA_DK), F32),
                        pltpu.VMEM((TS, vw), F32)],
        compiler_params=_params(("arbitrary",)),
        name="gla_bwd" if reverse else "gla_fwd",
    )(*args)


MOD_SH1, MOD_SC1, MOD_GT1, MOD_SH2, MOD_SC2, MOD_GT2 = range(N_MOD)


def kernel(x_prompt, x_sample, cache_k, cache_v, state_delta, state_gla, c, c_ctx, w_mod, b_mod, norm1, w_in, pool_w, pool_scale, attn_sink, dn_conv, dn_a_log, dn_dt_bias, dn_norm, gla_w2, gla_b2, gla_norm, w_br_pool, w_br_attn, w_br_delta, w_br_gla, w_out, norm2, w_gate, w_up, w_down, norm_f):
    x = jnp.concatenate([x_prompt.reshape(MP, D), x_sample.reshape(MS, D)], axis=0)
    cond8 = jnp.concatenate([c_ctx[None, :], c, jnp.zeros((8 - 1 - DEC_BATCH, D), F32)], axis=0)
    mod = _modulation(cond8, w_mod, b_mod).reshape(DEPTH, 8, N_MOD, 1, D)
    cos_t, sin_t = _rope_tables()
    s_gla_t = jnp.swapaxes(state_gla, -1, -2)
    w_out_bf16 = w_out.astype(BF16)
    w_brs = tuple(w.astype(BF16) for w in (w_br_pool, w_br_attn, w_br_delta, w_br_gla))
    w_in_t = jnp.swapaxes(w_in, 1, 2)
    gates_t = w_in_t[:, GATE_COL:, :].astype(BF16)

    ks, vs, sds, sgs = [], [], [], []
    for l in range(DEPTH):
        h = _prenorm(x, norm1, mod, l, MOD_SC1, MOD_SH1)
        pm = _mix_proj(h, w_in_t, l)
        ps = _small_proj(h, w_in_t, l)

        ya = _pool(pm, pool_w[l], pool_scale[l])
        yb = jnp.concatenate([_ctx_attn(pm, attn_sink[l]),
                              _lat_attn(pm, attn_sink[l], cos_t, sin_t, cache_k, cache_v, l)], axis=0)

        alog = jnp.zeros((1, 128), F32).at[0, SM_A:SM_A + 8].set(dn_a_log[l].reshape(8))
        dtb = jnp.zeros((1, 128), F32).at[0, SM_A:SM_A + 8].set(dn_dt_bias[l].reshape(8))
        o_b, sd_b = _dn_call(pm, ps, True, l, dn_conv[l], alog, dtb, state_delta)
        yc, sd_f = _dn_call(pm, ps, False, l, dn_conv[l], alog, dtb, state_delta, prev=o_b, norm_w=dn_norm[l])

        w2pad = [jnp.zeros((128, GLA_HEADS * GLA_DK), F32).at[SM_LR + o * GLA_RANK:SM_LR + (o + 1) * GLA_RANK].set(gla_w2[l, o]).astype(BF16)
                 for o in range(2)]
        g_b, sg_b = _gla_call(pm, ps, True, w2pad[1], gla_b2[l, 1].reshape(1, -1), s_gla_t[:, l, 1])
        yd, sg_f = _gla_call(pm, ps, False, w2pad[0], gla_b2[l, 0].reshape(1, -1), s_gla_t[:, l, 0], prev=g_b, norm_w=gla_norm[l])

        merged = _merge(h, gates_t, (ya, yb, yc, yd), w_brs, l)
        x, h2 = _out_proj(merged, w_out_bf16, x, mod, norm2, l, MOD_GT1, MOD_SC2, MOD_SH2)
        act = _ffn_up(h2, w_gate, w_up, l)
        x = _ffn_down(act, w_down, x, mod, l, MOD_GT2)

        ks.append(pm[:MP, C_K:C_K + KV_HEADS * HD].reshape(BATCH, SEQ, KV_HEADS, HD))
        vs.append(pm[:MP, C_V:C_V + KV_HEADS * HD].reshape(BATCH, SEQ, KV_HEADS, HD))
        sds.append(jnp.stack([sd_f, sd_b], axis=1))
        sgs.append(jnp.swapaxes(jnp.stack([sg_f, sg_b], axis=1), -1, -2))

    y = _finalnorm(x, norm_f)
    return (y[:MP].reshape(BATCH, SEQ, D), y[MP:].reshape(DEC_BATCH, DEC_SEQ, D),
            jnp.stack(ks, axis=1), jnp.stack(vs, axis=1), jnp.stack(sds, axis=1), jnp.stack(sgs, axis=1))
```

```python
import functools

import numpy as np
import jax
import jax.numpy as jnp
from jax import lax
from jax.experimental import pallas as pl
from jax.experimental.pallas import tpu as pltpu

F32 = jnp.float32
BF16 = jnp.bfloat16

D = 2048
BATCH = 16
SEQ = 256
DEPTH = 4
DEC_BATCH = 4
DEC_SEQ = 2048
PAST = 256
GRID_W = 64
HD = 128
ROPE_THETA = 10000.0
EPS = 1e-6
N_MOD = 6
POOL_WINDOWS = (2, 4, 8, 16)
POOL_W = 512
Q_HEADS = 8
KV_HEADS = 2
GROUP = 4
ATTN_WINDOW = 128
DN_HEADS = 4
DN_DK = 128
DN_W = 512
DN_CONV = 4
GLA_HEADS = 4
GLA_DK = 64
GLA_DV = 128
GLA_RANK = 16
GLA_NORMALIZER = 16.0
FFN = 5632
CHUNK = 64
PREP_CHUNKS = 4

MP = BATCH * SEQ
MS = DEC_BATCH * DEC_SEQ
M = MP + MS
TS = 256
NT_P = MP // TS
TPS = DEC_SEQ // TS
NT_S = MS // TS
NT = NT_P + NT_S
HALO = 8

C_Q, C_POOL, C_DNQKV, C_K, C_V, C_DNZ = 0, 1024, 1536, 3072, 3328, 3584
C_GQ, C_GK, C_GV, C_GR = 4096, 4352, 4608, 5120
MIXW = 5632
MIX_TN = 512
MIX_SRC = (1, 2, 0, 4, 5, 6, 3, 7, 8, 9, 10)
MIX_ALIGNED = 8
MIX_SHIFT = 16
SMALL_COL_A = 4096
SMALL_COL_LR = 5648
GATE_COL = 5680
IN_WIDTH = GATE_COL + 4 * D
SM_A, SM_B, SM_LR = 0, 8, 16

VMEM_LIMIT = 56 * 1024 * 1024


def _params(sem):
    return pltpu.CompilerParams(dimension_semantics=sem, vmem_limit_bytes=VMEM_LIMIT)


def _cond_row(row0):
    return jnp.where(row0 < MP, 0, 1 + (row0 - MP) // DEC_SEQ)


def _tile_info(t):
    is_p = t < NT_P
    s = jnp.maximum(t - NT_P, 0)
    j = s % TPS
    first = jnp.logical_or(is_p, j == 0)
    last = jnp.logical_or(is_p, j == TPS - 1)
    base = jnp.where(is_p, 0, j * TS)
    seq_len = jnp.where(is_p, SEQ, DEC_SEQ)
    return is_p, first, last, base, seq_len


def _silu(x):
    return x * jax.nn.sigmoid(x)


def _softplus(x):
    return jnp.maximum(x, 0.0) + jnp.log(1.0 + jnp.exp(-jnp.abs(x)))


def _dot(a, b):
    return jnp.dot(a, b, preferred_element_type=F32)


def _dot_tb(a, b):
    return lax.dot_general(a, b, (((1,), (1,)), ((), ())), preferred_element_type=F32)


def _dot_ta(a, b):
    return lax.dot_general(a, b, (((0,), (0,)), ((), ())), preferred_element_type=F32)


def _split2(x):
    hi = x.astype(BF16)
    lo = (x - hi.astype(F32)).astype(BF16)
    return hi, lo


def _split3(x):
    p1 = x.astype(BF16)
    r = x - p1.astype(F32)
    p2 = r.astype(BF16)
    p3 = (r - p2.astype(F32)).astype(BF16)
    return p1, p2, p3


def _dot_hp(a, b):
    ah, al = _split2(a)
    bh, bl = _split2(b)
    return _dot(ah, bh) + (_dot(ah, bl) + _dot(al, bh))


def _mod_kernel(c_ref, w_ref, b_ref, o_ref):
    a = _silu(c_ref[...]).astype(BF16)
    o_ref[...] = _dot(a, w_ref[...].astype(BF16)) + b_ref[...]


def _modulation(cond8, w_mod, b_mod):
    tn = 1024
    n = N_MOD * D
    return pl.pallas_call(
        _mod_kernel,
        grid=(DEPTH, n // tn),
        in_specs=[pl.BlockSpec((8, D), lambda l, j: (0, 0)),
                  pl.BlockSpec((None, D, tn), lambda l, j: (l, 0, j)),
                  pl.BlockSpec((None, 1, tn), lambda l, j: (l, 0, j))],
        out_specs=pl.BlockSpec((None, 8, tn), lambda l, j: (l, 0, j)),
        out_shape=jax.ShapeDtypeStruct((DEPTH, 8, n), F32),
        compiler_params=_params(("parallel", "parallel")),
        name="modulation",
    )(cond8, w_mod, b_mod.reshape(DEPTH, 1, n))


def _rms_rows(x, w):
    ms = jnp.mean(x * x, axis=-1, keepdims=True)
    return x * lax.rsqrt(ms + EPS) * w


def _prenorm_kernel(x_ref, nw_ref, sc_ref, sh_ref, wa_ref, wl_ref, h_ref, ps_ref, wb_scr):
    @pl.when(pl.program_id(0) == 0)
    def _():
        pad = jnp.zeros((128 - SM_LR - 2 * GLA_RANK, D), F32)
        wb_scr[...] = jnp.concatenate([wa_ref[0], wl_ref[0], pad], axis=0).astype(BF16)

    y = _rms_rows(x_ref[...], nw_ref[...])
    h = (y * (1.0 + sc_ref[...]) + sh_ref[...]).astype(h_ref.dtype)
    h_ref[...] = h
    ps_ref[...] = _dot_tb(h, wb_scr[...])


def _mod_spec(layer, chunk, tm, tn=D, row_axis=0, col_axis=None):
    def index(*g):
        col = 0 if col_axis is None else g[col_axis]
        return (layer, _cond_row(g[row_axis] * tm), chunk, 0, col)
    return pl.BlockSpec((None, None, None, 1, tn), index)


def _prenorm(x, norm_w, mod, w_in_t, layer, sc_chunk, sh_chunk):
    tm = 512
    return pl.pallas_call(
        _prenorm_kernel,
        grid=(M // tm,),
        in_specs=[pl.BlockSpec((tm, D), lambda i: (i, 0)),
                  pl.BlockSpec((None, 1, D), lambda i: (layer, 0, 0)),
                  _mod_spec(layer, sc_chunk, tm), _mod_spec(layer, sh_chunk, tm),
                  _wt_rows(layer, SM_LR, lambda i: SMALL_COL_A),
                  _wt_rows(layer, 2 * GLA_RANK, lambda i: SMALL_COL_LR)],
        out_specs=[pl.BlockSpec((tm, D), lambda i: (i, 0)), pl.BlockSpec((tm, 128), lambda i: (i, 0))],
        out_shape=[jax.ShapeDtypeStruct((M, D), BF16), jax.ShapeDtypeStruct((M, 128), F32)],
        scratch_shapes=[pltpu.VMEM((128, D), BF16)],
        compiler_params=_params(("arbitrary",)),
        name="prenorm",
    )(x, norm_w.reshape(DEPTH, 1, D), mod, mod, w_in_t, w_in_t)


def _finalnorm_kernel(x_ref, nw_ref, o_ref):
    o_ref[...] = _rms_rows(x_ref[...], nw_ref[...])


def _finalnorm(x, nw):
    tm = 512
    return pl.pallas_call(
        _finalnorm_kernel,
        grid=(M // tm,),
        in_specs=[pl.BlockSpec((tm, D), lambda i: (i, 0)),
                  pl.BlockSpec((1, D), lambda i: (0, 0))],
        out_specs=pl.BlockSpec((tm, D), lambda i: (i, 0)),
        out_shape=jax.ShapeDtypeStruct((M, D), F32),
        compiler_params=_params(("parallel",)),
        name="finalnorm",
    )(x, nw.reshape(1, D))


def _first_row_block():
    return pl.program_id(1) == 0


def _wt_rows(layer, nrows, row_fn):
    def index(*g):
        row = row_fn(*g)
        return (layer, row if isinstance(row, int) else pl.multiple_of(row, 8), 0)
    return pl.BlockSpec((pl.Element(1), pl.Element(nrows), pl.Element(D)), index)


def _mix_kernel(h_ref, w_ref, o_ref, wb_scr):
    @pl.when(_first_row_block())
    def _():
        wb_scr[...] = w_ref[0].astype(BF16)

    o_ref[...] = _dot_tb(h_ref[...], wb_scr[...])


def _mix_src_row(j):
    src = j
    for out_tile, src_tile in enumerate(MIX_SRC):
        src = jnp.where(j == out_tile, src_tile, src)
    return src * MIX_TN + jnp.where(src >= MIX_ALIGNED, MIX_SHIFT, 0)


def _mix_proj(h, w_in_t, layer):
    tm, tn = 2048, MIX_TN
    return pl.pallas_call(
        _mix_kernel,
        grid=(MIXW // tn, M // tm),
        in_specs=[pl.BlockSpec((tm, D), lambda j, i: (i, 0)),
                  _wt_rows(layer, tn, lambda j, i: _mix_src_row(j))],
        out_specs=pl.BlockSpec((tm, tn), lambda j, i: (i, j)),
        out_shape=jax.ShapeDtypeStruct((M, MIXW), F32),
        scratch_shapes=[pltpu.VMEM((tn, D), BF16)],
        compiler_params=_params(("arbitrary", "arbitrary")),
        name="mix_proj",
    )(h, w_in_t)


def _ffn_up_kernel(h_ref, wg_ref, wu_ref, o_ref, wg_scr, wu_scr):
    @pl.when(_first_row_block())
    def _():
        wg_scr[...] = wg_ref[...].astype(BF16)
        wu_scr[...] = wu_ref[...].astype(BF16)

    h = h_ref[...]
    o_ref[...] = (_silu(_dot(h, wg_scr[...])) * _dot(h, wu_scr[...])).astype(o_ref.dtype)


def _ffn_up(h, w_gate, w_up, layer):
    tm, tn = 2048, 512
    wspec = pl.BlockSpec((None, D, tn), lambda j, i: (layer, 0, j))
    return pl.pallas_call(
        _ffn_up_kernel,
        grid=(FFN // tn, M // tm),
        in_specs=[pl.BlockSpec((tm, D), lambda j, i: (i, 0)), wspec, wspec],
        out_specs=pl.BlockSpec((tm, tn), lambda j, i: (i, j)),
        out_shape=jax.ShapeDtypeStruct((M, FFN), BF16),
        scratch_shapes=[pltpu.VMEM((D, tn), BF16), pltpu.VMEM((D, tn), BF16)],
        compiler_params=_params(("arbitrary", "arbitrary")),
        name="ffn_up",
    )(h, w_gate, w_up)


def _ffn_down_kernel(a_ref, w_ref, x_ref, gt_ref, o_ref, wb_scr):
    @pl.when(_first_row_block())
    def _():
        wb_scr[...] = w_ref[...].astype(BF16)

    o_ref[...] = x_ref[...] + gt_ref[...] * _dot(a_ref[...], wb_scr[...])


def _ffn_down(act, w_down, x, mod, layer, gt_chunk):
    tm, tn = 1024, 512
    return pl.pallas_call(
        _ffn_down_kernel,
        grid=(D // tn, M // tm),
        in_specs=[pl.BlockSpec((tm, FFN), lambda j, i: (i, 0)),
                  pl.BlockSpec((None, FFN, tn), lambda j, i: (layer, 0, j), pipeline_mode=pl.Buffered(1)),
                  pl.BlockSpec((tm, tn), lambda j, i: (i, j)),
                  _mod_spec(layer, gt_chunk, tm, tn, row_axis=1, col_axis=0)],
        out_specs=pl.BlockSpec((tm, tn), lambda j, i: (i, j)),
        out_shape=jax.ShapeDtypeStruct((M, D), F32),
        scratch_shapes=[pltpu.VMEM((FFN, tn), BF16)],
        compiler_params=_params(("arbitrary", "arbitrary")),
        name="ffn_down",
    )(act, w_down, x, mod)


BR_WIDTHS = (POOL_W, Q_HEADS * HD, DN_W, GLA_HEADS * GLA_DV)


def _merge_kernel(h_ref, *refs):
    g_refs, y_refs, w_refs, o_ref = refs[0:4], refs[4:8], refs[8:12], refs[12]
    h = h_ref[...]
    acc = None
    for b in range(4):
        term = jax.nn.sigmoid(_dot_tb(h, g_refs[b][...])) * _dot(y_refs[b][...], w_refs[b][...])
        acc = term if acc is None else acc + term
    o_ref[...] = acc.astype(o_ref.dtype)


def _merge(h, gates_t, ys, w_brs, layer):
    tm, tn = 1024, 512
    nj = D // tn
    gate_specs = [pl.BlockSpec((None, tn, D), functools.partial(lambda i, j, b: (layer, b * nj + j, 0), b=b))
                  for b in range(4)]
    y_specs = [pl.BlockSpec((tm, w), lambda i, j: (i, 0)) for w in BR_WIDTHS]
    w_specs = [pl.BlockSpec((None, w, tn), lambda i, j: (layer, 0, j)) for w in BR_WIDTHS]
    return pl.pallas_call(
        _merge_kernel,
        grid=(M // tm, nj),
        in_specs=[pl.BlockSpec((tm, D), lambda i, j: (i, 0))] + gate_specs + y_specs + w_specs,
        out_specs=pl.BlockSpec((tm, tn), lambda i, j: (i, j)),
        out_shape=jax.ShapeDtypeStruct((M, D), BF16),
        compiler_params=_params(("parallel", "arbitrary")),
        name="merge",
    )(h, *([gates_t] * 4), *ys, *w_brs)


def _out_proj_kernel(m_ref, w_ref, x_ref, gt_ref, nw_ref, sc_ref, sh_ref, xo_ref, h_ref):
    xn = x_ref[...] + gt_ref[...] * _dot(m_ref[...], w_ref[...])
    xo_ref[...] = xn
    h_ref[...] = (_rms_rows(xn, nw_ref[...]) * (1.0 + sc_ref[...]) + sh_ref[...]).astype(h_ref.dtype)


def _out_proj(merged, w_out_bf16, x, mod, norm_w, layer, gt_chunk, sc_chunk, sh_chunk):
    tm = 512
    row = pl.BlockSpec((tm, D), lambda i: (i, 0))
    return pl.pallas_call(
        _out_proj_kernel,
        grid=(M // tm,),
        in_specs=[row,
                  pl.BlockSpec((None, D, D), lambda i: (layer, 0, 0), pipeline_mode=pl.Buffered(1)),
                  row, _mod_spec(layer, gt_chunk, tm),
                  pl.BlockSpec((None, 1, D), lambda i: (layer, 0, 0)),
                  _mod_spec(layer, sc_chunk, tm), _mod_spec(layer, sh_chunk, tm)],
        out_specs=[row, row],
        out_shape=[jax.ShapeDtypeStruct((M, D), F32), jax.ShapeDtypeStruct((M, D), BF16)],
        compiler_params=_params(("parallel",)),
        name="out_proj",
    )(merged, w_out_bf16, x, mod, norm_w.reshape(DEPTH, 1, D), mod, mod)


def _pool_kernel(u_ref, up_ref, un_ref, w_ref, sc_ref, o_ref):
    t = pl.program_id(0)
    _, first, last, base, seq_len = _tile_info(t)
    u = u_ref[...]
    prev = jnp.where(first, 0.0, up_ref[...])
    nxt = jnp.where(last, 0.0, un_ref[...])
    ext = jnp.concatenate([prev, u, nxt], axis=0)
    pos = base + lax.broadcasted_iota(jnp.int32, (TS, 1), 0)
    outs = []
    for gi, win in enumerate(POOL_WINDOWS):
        lanes = slice(gi * 128, (gi + 1) * 128)
        p = ext[:, lanes]
        s = 1
        while s < win:
            p = p + pltpu.roll(p, s, 0)
            s *= 2
        d = HALO + win // 2 - 1
        ssum = p[d:d + TS]
        cnt = (jnp.minimum(pos + win // 2, seq_len) - jnp.maximum(pos - win // 2, 0)).astype(F32)
        pg = ssum / cnt - u[:, lanes]
        outs.append(_dot(pg.astype(BF16), w_ref[gi].astype(BF16)))
    o_ref[...] = (jnp.concatenate(outs, axis=1) * sc_ref[...]).astype(o_ref.dtype)


def _halo_specs(width, colblk):
    r = TS // HALO
    nblk = M // HALO
    prev = pl.BlockSpec((HALO, width), lambda t: (jnp.maximum(t * r - 1, 0), colblk))
    nxt = pl.BlockSpec((HALO, width), lambda t: (jnp.minimum((t + 1) * r, nblk - 1), colblk))
    return prev, nxt


def _pool(pm, pool_w, pool_scale):
    cb = C_POOL // POOL_W
    prev, nxt = _halo_specs(POOL_W, cb)
    return pl.pallas_call(
        _pool_kernel,
        grid=(NT,),
        in_specs=[pl.BlockSpec((TS, POOL_W), lambda t: (t, cb)), prev, nxt,
                  pl.BlockSpec((4, 128, 128), lambda t: (0, 0, 0)),
                  pl.BlockSpec((1, POOL_W), lambda t: (0, 0))],
        out_specs=pl.BlockSpec((TS, POOL_W), lambda t: (t, 0)),
        out_shape=jax.ShapeDtypeStruct((M, POOL_W), BF16),
        compiler_params=_params(("parallel",)),
        name="pool",
    )(pm, pm, pm, pool_w, pool_scale.reshape(1, POOL_W))


def _ctx_attn_kernel(sink_ref, q_ref, k_ref, v_ref, o_ref):
    q = q_ref[...] * (HD ** -0.5)
    k = k_ref[...].astype(BF16)
    v = v_ref[...].astype(BF16)
    for n in range(KV_HEADS):
        kn = k[:, n * HD:(n + 1) * HD]
        vn = v[:, n * HD:(n + 1) * HD]
        for g in range(GROUP):
            h = n * GROUP + g
            qh = q[:, h * HD:(h + 1) * HD].astype(BF16)
            s = _dot_tb(qh, kn)
            snk = sink_ref[h]
            m = jnp.maximum(jnp.max(s, axis=-1, keepdims=True), snk)
            e = jnp.exp(s - m)
            den = jnp.sum(e, axis=-1, keepdims=True) + jnp.exp(snk - m)
            o = _dot(e.astype(BF16), vn) / den
            o_ref[:, h * HD:(h + 1) * HD] = o.astype(o_ref.dtype)


def _ctx_attn(pm, sink):
    qw, kw = Q_HEADS * HD, KV_HEADS * HD
    return pl.pallas_call(
        _ctx_attn_kernel,
        grid=(NT_P,),
        in_specs=[pl.BlockSpec(memory_space=pltpu.SMEM),
                  pl.BlockSpec((TS, qw), lambda t: (t, C_Q // qw)),
                  pl.BlockSpec((TS, kw), lambda t: (t, C_K // kw)),
                  pl.BlockSpec((TS, kw), lambda t: (t, C_V // kw))],
        out_specs=pl.BlockSpec((TS, qw), lambda t: (t, 0)),
        out_shape=jax.ShapeDtypeStruct((M, qw), BF16),
        compiler_params=_params(("parallel",)),
        name="ctx_attn",
    )(sink, pm, pm, pm)


def _rope(x, cs, sn):
    lane = lax.broadcasted_iota(jnp.int32, x.shape, 1)
    swapped = jnp.where((lane % 64) < 32, pltpu.roll(x, 96, 1), pltpu.roll(x, 32, 1))
    return x * cs + swapped * sn


def _lat_attn_kernel(sink_ref, q_ref, km_ref, kp_ref, kn_ref, vm_ref, vp_ref, vn_ref,
                     cm_ref, cp_ref, cn_ref, sm_ref, sp_ref, sn_ref, kc_ref, vc_ref, yb_ref, o_ref):
    del yb_ref
    ts = pl.program_id(0)
    base = (ts % TPS) * TS
    half = TS // 2
    nk = TS + 2 * half
    q = q_ref[...] * (HD ** -0.5)
    cq, sq = cm_ref[...], sm_ref[...]
    kl = jnp.concatenate([kp_ref[...], km_ref[...], kn_ref[...]], axis=0)
    vl = jnp.concatenate([vp_ref[...], vm_ref[...], vn_ref[...]], axis=0).astype(BF16)
    ck = jnp.concatenate([cp_ref[...], cq, cn_ref[...]], axis=0)
    sk = jnp.concatenate([sp_ref[...], sq, sn_ref[...]], axis=0)
    kc = kc_ref[...].astype(BF16)
    vc = vc_ref[...].astype(BF16)
    qpos = base + lax.broadcasted_iota(jnp.int32, (TS, 1), 0)
    kpos = base - half + lax.broadcasted_iota(jnp.int32, (1, nk), 1)
    dist = qpos - kpos
    ok = (jnp.abs(dist) <= ATTN_WINDOW).astype(jnp.int32) * (kpos >= 0).astype(jnp.int32) * (kpos < DEC_SEQ).astype(jnp.int32)
    bias = jnp.where(ok > 0, 0.0, -1e30)
    for n in range(KV_HEADS):
        lanes = slice(n * HD, (n + 1) * HD)
        kr = _rope(kl[:, lanes], ck, sk).astype(BF16)
        vln, kcn, vcn = vl[:, lanes], kc[:, lanes], vc[:, lanes]
        for g in range(GROUP):
            h = n * GROUP + g
            qh = _rope(q[:, h * HD:(h + 1) * HD], cq, sq).astype(BF16)
            s_c = _dot_tb(qh, kcn)
            s_l = _dot_tb(qh, kr) + bias
            snk = sink_ref[h]
            m = jnp.maximum(jnp.maximum(jnp.max(s_c, axis=-1, keepdims=True),
                                        jnp.max(s_l, axis=-1, keepdims=True)), snk)
            e_c = jnp.exp(s_c - m)
            e_l = jnp.exp(s_l - m)
            den = jnp.sum(e_c, axis=-1, keepdims=True) + jnp.sum(e_l, axis=-1, keepdims=True) + jnp.exp(snk - m)
            o = (_dot(e_c.astype(BF16), vcn) + _dot(e_l.astype(BF16), vln)) / den
            o_ref[:, h * HD:(h + 1) * HD] = o.astype(o_ref.dtype)


def _lat_attn(pm, sink, cos_t, sin_t, cache_k, cache_v, layer, yb):
    qw, kw = Q_HEADS * HD, KV_HEADS * HD
    half = TS // 2
    nhb = M // half
    nhs = DEC_SEQ // half

    def main(w, cb):
        return pl.BlockSpec((TS, w), lambda ts: (NT_P + ts, cb))

    def prev(cb):
        return pl.BlockSpec((half, kw), lambda ts: (2 * (NT_P + ts) - 1, cb))

    def nxt(cb):
        return pl.BlockSpec((half, kw), lambda ts: (jnp.minimum(2 * (NT_P + ts) + 2, nhb - 1), cb))

    tab_m = pl.BlockSpec((TS, HD), lambda ts: (ts % TPS, 0))
    tab_p = pl.BlockSpec((half, HD), lambda ts: (jnp.maximum(2 * (ts % TPS) - 1, 0), 0))
    tab_n = pl.BlockSpec((half, HD), lambda ts: (jnp.minimum(2 * (ts % TPS) + 2, nhs - 1), 0))
    ctx = pl.BlockSpec((None, None, PAST, kw), lambda ts: (ts // TPS, layer, 0, 0))
    return pl.pallas_call(
        _lat_attn_kernel,
        grid=(NT_S,),
        in_specs=[pl.BlockSpec(memory_space=pltpu.SMEM),
                  main(qw, C_Q // qw),
                  main(kw, C_K // kw), prev(C_K // kw), nxt(C_K // kw),
                  main(kw, C_V // kw), prev(C_V // kw), nxt(C_V // kw),
                  tab_m, tab_p, tab_n, tab_m, tab_p, tab_n, ctx, ctx,
                  pl.BlockSpec(memory_space=pl.ANY)],
        out_specs=pl.BlockSpec((TS, qw), lambda ts: (NT_P + ts, 0)),
        out_shape=jax.ShapeDtypeStruct((M, qw), BF16),
        input_output_aliases={16: 0},
        compiler_params=_params(("parallel",)),
        name="lat_attn",
    )(sink, pm, pm, pm, pm, pm, pm, pm, cos_t, cos_t, cos_t, sin_t, sin_t, sin_t,
      cache_k.reshape(DEC_BATCH, DEPTH, PAST, kw), cache_v.reshape(DEC_BATCH, DEPTH, PAST, kw), yb)


def _rope_tables():
    quarter = HD // 4
    inv = ROPE_THETA ** (-jnp.arange(quarter, dtype=F32) / quarter)
    t = jnp.arange(DEC_SEQ)
    row = (t // GRID_W).astype(F32)
    col = (t % GRID_W).astype(F32)
    ar = row[:, None] * inv[None, :]
    ac = col[:, None] * inv[None, :]
    cos_t = jnp.concatenate([jnp.cos(ar), jnp.cos(ar), jnp.cos(ac), jnp.cos(ac)], axis=-1)
    sin_t = jnp.concatenate([-jnp.sin(ar), jnp.sin(ar), -jnp.sin(ac), jnp.sin(ac)], axis=-1)
    return cos_t, sin_t


def _scan_tile(reverse):
    i = pl.program_id(0)
    return (NT - 1 - i) if reverse else i


def _tri_masks(reverse):
    ii = lax.broadcasted_iota(jnp.int32, (CHUNK, CHUNK), 0)
    jj = lax.broadcasted_iota(jnp.int32, (CHUNK, CHUNK), 1)
    incl = (jj >= ii) if reverse else (jj <= ii)
    strict = (jj > ii) if reverse else (jj < ii)
    return incl, strict


def _gated_rms_out(o, gate, nw):
    outs = []
    for h in range(4):
        lanes = slice(h * 128, (h + 1) * 128)
        outs.append(_rms_rows(o[:, lanes], nw) * _silu(gate[:, lanes]))
    return jnp.concatenate(outs, axis=1)


def _dot_hp_all(xs, ys):
    hh = [_dot(x[0], y[0]) for x, y in zip(xs, ys)]
    hl = [_dot(x[0], y[1]) for x, y in zip(xs, ys)]
    lh = [_dot(x[1], y[0]) for x, y in zip(xs, ys)]
    return [a + (b + c) for a, b, c in zip(hh, hl, lh)]


def _inv_unit_tri_all(lms):
    ii = lax.broadcasted_iota(jnp.int32, (CHUNK, CHUNK), 0)
    jj = lax.broadcasted_iota(jnp.int32, (CHUNK, CHUNK), 1)
    eye = jnp.where(ii == jj, 1.0, 0.0)
    ps = [eye - lm for lm in lms]
    mbs = [lm.astype(BF16) for lm in lms]
    mks = [_dot(m, m) for m in mbs]
    span = 4
    while span <= CHUNK:
        mbs = [m.astype(BF16) for m in mks]
        incs = [_dot(p.astype(BF16), m) for p, m in zip(ps, mbs)]
        if span < CHUNK:
            mks = [_dot(m, m) for m in mbs]
        ps = [p + inc for p, inc in zip(ps, incs)]
        span *= 2
    lts = _dot_hp_all([_split2(lm) for lm in lms], [_split2(t) for t in ps])
    rs = [(eye - t) - lt for t, lt in zip(ps, lts)]
    return [t + _dot(t.astype(BF16), r.astype(BF16)) for t, r in zip(ps, rs)]


def _dn_kernel(reverse, has_prev, *refs):
    if has_prev:
        qkv_ref, sm_ref, alog_ref, dtb_ref, s0_ref, oprev_ref, z_ref, nw_ref, o_ref, st_ref = refs[:10]
        scratch = refs[10:]
    else:
        x_ref, xp_ref, xn_ref, sm_ref, cw_ref, alog_ref, dtb_ref, s0_ref, o_ref, st_ref, qkv_out_ref = refs[:11]
        scratch = refs[11:]
    (s_scr, q_scr, k_scr, v_scr, b_scr, gcc_scr, gcr_scr, u_scr, w_scr, qe_scr, kd_scr, a_scr, o_scr) = scratch
    t = _scan_tile(reverse)
    is_p, first, last, _, _ = _tile_info(t)
    d = 1 if reverse else 0
    nchunk = TS // CHUNK
    e_last = 0 if reverse else CHUNK - 1

    @pl.when(last if reverse else first)
    def _():
        s_scr[...] = jnp.where(is_p, 0.0, s0_ref[...])

    if has_prev:
        q_scr[...] = qkv_ref[:, 0:DN_W]
        k_scr[...] = qkv_ref[:, DN_W:2 * DN_W]
        v_scr[...] = qkv_ref[:, 2 * DN_W:3 * DN_W]
    else:
        x = x_ref[...]
        ext = jnp.concatenate([jnp.where(first, 0.0, xp_ref[...]), x, jnp.where(last, 0.0, xn_ref[...])], axis=0)
        cw = cw_ref[...]
        ext_rows = TS + 2 * HALO
        conv = None
        for j in range(DN_CONV):
            off = j - DN_CONV // 2
            rows = ext if off == 0 else pltpu.roll(ext, (ext_rows - off) % ext_rows, 0)
            term = rows[HALO:HALO + TS] * cw[j:j + 1, :]
            conv = term if conv is None else conv + term
        act = _silu(conv)
        for h in range(DN_HEADS):
            lanes = slice(h * 128, (h + 1) * 128)
            qh = act[:, h * 128:(h + 1) * 128]
            kh = act[:, DN_W + h * 128:DN_W + (h + 1) * 128]
            q_scr[:, lanes] = qh * lax.rsqrt(jnp.sum(qh * qh, axis=-1, keepdims=True) + EPS) * (DN_DK ** -0.5)
            k_scr[:, lanes] = kh * lax.rsqrt(jnp.sum(kh * kh, axis=-1, keepdims=True) + EPS)
        v_scr[...] = act[:, 2 * DN_W:3 * DN_W]
        qkv_out_ref[:, 0:DN_W] = q_scr[...]
        qkv_out_ref[:, DN_W:2 * DN_W] = k_scr[...]
        qkv_out_ref[:, 2 * DN_W:3 * DN_W] = v_scr[...]
    sm = sm_ref[...]
    b_scr[...] = jax.nn.sigmoid(sm)
    g = -jnp.exp(alog_ref[...]) * _softplus(sm + dtb_ref[...])

    ti = lax.broadcasted_iota(jnp.int32, (TS, TS), 0)
    tj = lax.broadcasted_iota(jnp.int32, (TS, TS), 1)
    same = (ti // CHUNK) == (tj // CHUNK)
    lower = jnp.where(jnp.logical_and(same, tj <= ti), 1.0, 0.0).astype(BF16)
    upper = jnp.where(jnp.logical_and(same, tj >= ti), 1.0, 0.0).astype(BF16)
    tri, tri_t = (upper, lower) if reverse else (lower, upper)
    g1, g2, g3 = _split3(g)
    gcc_scr[...] = _dot(tri, g1) + _dot(tri, g2) + _dot(tri, g3)
    gcr = _dot_ta(g1, tri_t) + _dot_ta(g2, tri_t) + _dot_ta(g3, tri_t)
    for c in range(nchunk):
        gcr_scr[c] = gcr[:, c * CHUNK:(c + 1) * CHUNK]

    incl, strict = _tri_masks(reverse)

    def prep_body(pi, carry):
        rows, gblk, gcr_c, bt = {}, {}, {}, {}
        for cc in range(PREP_CHUNKS):
            c = pi * PREP_CHUNKS + cc
            rows[cc] = pl.ds(pl.multiple_of(c * CHUNK, CHUNK), CHUNK)
            gblk[cc] = gcc_scr[rows[cc], :]
            gcr_c[cc] = gcr_scr[c]
            bt[cc] = b_scr[rows[cc], :]
        info = []
        for cc in range(PREP_CHUNKS):
            for h in range(DN_HEADS):
                lanes = slice(h * 128, (h + 1) * 128)
                r = SM_A + d * DN_HEADS + h
                gcol = gblk[cc][:, r:r + 1]
                grow = gcr_c[cc][r:r + 1, :]
                glast = gblk[cc][e_last:e_last + 1, r:r + 1]
                beta = bt[cc][:, SM_B + d * DN_HEADS + h:SM_B + d * DN_HEADS + h + 1]
                qc, kc, vc = q_scr[rows[cc], lanes], k_scr[rows[cc], lanes], v_scr[rows[cc], lanes]
                decay = jnp.exp(jnp.where(incl, gcol - grow, -1e30))
                kb = kc * beta
                eg = jnp.exp(gcol)
                qe_scr[rows[cc], lanes] = (qc * eg).astype(BF16)
                kd_scr[rows[cc], lanes] = (kc * jnp.exp(glast - gcol)).astype(BF16)
                info.append(dict(cc=cc, h=h, lanes=lanes, decay=decay, kbb=kb.astype(BF16), kcb=kc.astype(BF16),
                                 qcb=qc.astype(BF16), vbb=(vc * beta).astype(BF16), kgb=(kb * eg).astype(BF16)))
        kk = [_dot_tb(p["kbb"], p["kcb"]) for p in info]
        qk = [_dot_tb(p["qcb"], p["kcb"]) for p in info]
        for p, m in zip(info, qk):
            a_scr[rows[p["cc"]], p["h"] * CHUNK:(p["h"] + 1) * CHUNK] = (m * p["decay"]).astype(BF16)
        lms = [jnp.where(strict, m * p["decay"], 0.0) for p, m in zip(info, kk)]
        tinvs = [tv.astype(BF16) for tv in _inv_unit_tri_all(lms)]
        us = [_dot(tv, p["vbb"]) for tv, p in zip(tinvs, info)]
        ws = [_dot(tv, p["kgb"]) for tv, p in zip(tinvs, info)]
        for p, u, w in zip(info, us, ws):
            u_scr[rows[p["cc"]], p["lanes"]] = u
            w_scr[rows[p["cc"]], p["lanes"]] = w.astype(BF16)
        return carry

    lax.fori_loop(0, nchunk // PREP_CHUNKS, prep_body, 0)

    def chunk_body(ci, carry):
        c = (nchunk - 1 - ci) if reverse else ci
        rows = pl.ds(pl.multiple_of(c * CHUNK, CHUNK), CHUNK)
        glast_row = gcc_scr[pl.ds(c * CHUNK + e_last, 1), :]
        heads = range(DN_HEADS)
        lanes = [slice(h * 128, (h + 1) * 128) for h in heads]
        s_old = [s_scr[h] for h in heads]
        sb = [s.astype(BF16) for s in s_old]
        ws = [_dot(w_scr[rows, lanes[h]], sb[h]) for h in heads]
        qs = [_dot(qe_scr[rows, lanes[h]], sb[h]) for h in heads]
        vnb = [(u_scr[rows, lanes[h]] - ws[h]).astype(BF16) for h in heads]
        av = [_dot(a_scr[rows, h * CHUNK:(h + 1) * CHUNK], vnb[h]) for h in heads]
        up = [_dot_ta(kd_scr[rows, lanes[h]], vnb[h]) for h in heads]
        for h in heads:
            r = SM_A + d * DN_HEADS + h
            o_scr[rows, lanes[h]] = qs[h] + av[h]
            s_scr[h] = jnp.exp(glast_row[:, r:r + 1]) * s_old[h] + up[h]
        return carry

    lax.fori_loop(0, nchunk, chunk_body, 0)

    if has_prev:
        o_ref[...] = _gated_rms_out(o_scr[...] + oprev_ref[...], z_ref[...], nw_ref[...]).astype(o_ref.dtype)
    else:
        o_ref[...] = o_scr[...]

    @pl.when(is_p)
    def _():
        st_ref[...] = s_scr[...]


def _seq_of_tile(t):
    return jnp.maximum(t - NT_P, 0) // TPS


def _dn_call(pm, ps, reverse, layer, conv_w, alog, dtb, state_delta, prev=None, norm_w=None):
    d = 1 if reverse else 0
    tmap = (lambda i: NT - 1 - i) if reverse else (lambda i: i)
    width = 3 * DN_W
    cb = C_DNQKV // width
    r = TS // HALO
    nblk = M // HALO
    small = [pl.BlockSpec((TS, 128), lambda i: (tmap(i), 0))]
    params = [pl.BlockSpec((1, 128), lambda i: (0, 0)),
              pl.BlockSpec((1, 128), lambda i: (0, 0)),
              pl.BlockSpec((None, None, None, DN_HEADS, DN_DK, 128), lambda i: (_seq_of_tile(tmap(i)), layer, d, 0, 0, 0))]
    tile_w = lambda w: pl.BlockSpec((TS, w), lambda i: (tmap(i), 0))
    has_prev = prev is not None
    out_specs = [tile_w(DN_W),
                 pl.BlockSpec((None, DN_HEADS, DN_DK, 128), lambda i: (jnp.minimum(tmap(i), NT_P - 1), 0, 0, 0))]
    out_shape = [jax.ShapeDtypeStruct((M, DN_W), BF16 if has_prev else F32),
                 jax.ShapeDtypeStruct((BATCH, DN_HEADS, DN_DK, 128), F32)]
    if has_prev:
        o_first, qkv = prev
        in_specs = [tile_w(width)] + small + params + [
            tile_w(DN_W), pl.BlockSpec((TS, DN_W), lambda i: (tmap(i), C_DNZ // DN_W)), pl.BlockSpec((1, 128), lambda i: (0, 0))]
        args = [qkv, ps, alog, dtb, state_delta, o_first, pm, norm_w.reshape(1, 128)]
    else:
        in_specs = [pl.BlockSpec((TS, width), lambda i: (tmap(i), cb)),
                    pl.BlockSpec((HALO, width), lambda i: (jnp.maximum(tmap(i) * r - 1, 0), cb)),
                    pl.BlockSpec((HALO, width), lambda i: (jnp.minimum((tmap(i) + 1) * r, nblk - 1), cb))] + small + [
            pl.BlockSpec((DN_CONV, width), lambda i: (0, 0))] + params
        args = [pm, pm, pm, ps, conv_w, alog, dtb, state_delta]
        out_specs.append(tile_w(width))
        out_shape.append(jax.ShapeDtypeStruct((M, width), F32))
    return pl.pallas_call(
        functools.partial(_dn_kernel, reverse, has_prev),
        grid=(NT,),
        in_specs=in_specs,
        out_specs=out_specs,
        out_shape=out_shape,
        scratch_shapes=[pltpu.VMEM((DN_HEADS, DN_DK, 128), F32),
                        pltpu.VMEM((TS, DN_W), F32), pltpu.VMEM((TS, DN_W), F32), pltpu.VMEM((TS, DN_W), F32),
                        pltpu.VMEM((TS, 128), F32), pltpu.VMEM((TS, 128), F32),
                        pltpu.VMEM((TS // CHUNK, 128, CHUNK), F32),
                        pltpu.VMEM((TS, DN_W), F32), pltpu.VMEM((TS, DN_W), BF16),
                        pltpu.VMEM((TS, DN_W), BF16), pltpu.VMEM((TS, DN_W), BF16),
                        pltpu.VMEM((TS, DN_HEADS * CHUNK), BF16),
                        pltpu.VMEM((TS, DN_W), F32)],
        compiler_params=_params(("arbitrary",)),
        name="deltanet_bwd" if reverse else "deltanet_fwd",
    )(*args)


def _gla_kernel(reverse, has_prev, q_ref, k_ref, v_ref, sm_ref, w2_ref, b2_ref, s0_ref, *rest):
    if has_prev:
        oprev_ref, r_ref, nw_ref, o_ref, st_ref, s_scr, o_scr = rest
    else:
        o_ref, st_ref, s_scr, o_scr = rest
    t = _scan_tile(reverse)
    is_p, first, last, _, _ = _tile_info(t)

    @pl.when(last if reverse else first)
    def _():
        s_scr[...] = jnp.where(is_p, 0.0, s0_ref[...])

    logits = _dot(sm_ref[...].astype(BF16), w2_ref[...]) + b2_ref[...]
    gk = -_softplus(-logits) / GLA_NORMALIZER

    ti = lax.broadcasted_iota(jnp.int32, (TS, TS), 0)
    tj = lax.broadcasted_iota(jnp.int32, (TS, TS), 1)
    same = (ti // CHUNK) == (tj // CHUNK)
    tri = jnp.where(jnp.logical_and(same, (tj >= ti) if reverse else (tj <= ti)), 1.0, 0.0).astype(BF16)
    g1, g2, g3 = _split3(gk)
    gc = _dot(tri, g1) + _dot(tri, g2) + _dot(tri, g3)

    incl, _ = _tri_masks(reverse)
    nchunk = TS // CHUNK
    e_last = 0 if reverse else CHUNK - 1
    eg = jnp.exp(gc)
    q_all = q_ref[...] * (GLA_DK ** -0.5)
    k_all = k_ref[...]
    qg_all = (q_all * eg).astype(BF16)
    kg_all = (k_all * jnp.exp(-gc)).astype(BF16)
    v_all = v_ref[...].astype(BF16)

    probs = [(c, h) for c in range(nchunk) for h in range(GLA_HEADS)]
    rws = {c: slice(c * CHUNK, (c + 1) * CHUNK) for c in range(nchunk)}
    kls = {h: slice(h * GLA_DK, (h + 1) * GLA_DK) for h in range(GLA_HEADS)}
    vls = {h: slice(h * GLA_DV, (h + 1) * GLA_DV) for h in range(GLA_HEADS)}
    glast = {c: gc[c * CHUNK + e_last:c * CHUNK + e_last + 1, :] for c in range(nchunk)}
    kd = {c: (k_all[rws[c], :] * jnp.exp(glast[c] - gc[rws[c], :])).astype(BF16) for c in range(nchunk)}
    a = {p: jnp.where(incl, _dot_tb(qg_all[rws[p[0]], kls[p[1]]], kg_all[rws[p[0]], kls[p[1]]]), 0.0).astype(BF16)
         for p in probs}
    av = {p: _dot(a[p], v_all[rws[p[0]], vls[p[1]]]) for p in probs}
    up = {p: _dot_ta(v_all[rws[p[0]], vls[p[1]]], kd[p[0]][:, kls[p[1]]]) for p in probs}
    order = list(reversed(range(nchunk))) if reverse else list(range(nchunk))
    s_in = {}
    for h in range(GLA_HEADS):
        s = s_scr[h]
        for c in order:
            s_in[(c, h)] = s.astype(BF16)
            s = jnp.exp(glast[c][:, kls[h]]) * s + up[(c, h)]
        s_scr[h] = s
    for p in probs:
        c, h = p
        o_scr[rws[c], vls[h]] = _dot_tb(qg_all[rws[c], kls[h]], s_in[p]) + av[p]

    if has_prev:
        o_ref[...] = _gated_rms_out(o_scr[...] + oprev_ref[...], r_ref[...], nw_ref[...]).astype(o_ref.dtype)
    else:
        o_ref[...] = o_scr[...]

    @pl.when(is_p)
    def _():
        st_ref[...] = s_scr[...]


def _gla_call(pm, ps, reverse, w2pad, b2, s0t, prev=None, norm_w=None):
    tmap = (lambda i: NT - 1 - i) if reverse else (lambda i: i)
    qkw = GLA_HEADS * GLA_DK
    vw = GLA_HEADS * GLA_DV
    in_specs = [
        pl.BlockSpec((TS, qkw), lambda i: (tmap(i), C_GQ // qkw)),
        pl.BlockSpec((TS, qkw), lambda i: (tmap(i), C_GK // qkw)),
        pl.BlockSpec((TS, vw), lambda i: (tmap(i), C_GV // vw)),
        pl.BlockSpec((TS, 128), lambda i: (tmap(i), 0)),
        pl.BlockSpec((128, qkw), lambda i: (0, 0)),
        pl.BlockSpec((1, qkw), lambda i: (0, 0)),
        pl.BlockSpec((None, GLA_HEADS, GLA_DV, GLA_DK), lambda i: (_seq_of_tile(tmap(i)), 0, 0, 0)),
    ]
    args = [pm, pm, pm, ps, w2pad, b2, s0t]
    has_prev = prev is not None
    if has_prev:
        in_specs += [pl.BlockSpec((TS, vw), lambda i: (tmap(i), 0)),
                     pl.BlockSpec((TS, vw), lambda i: (tmap(i), C_GR // vw)),
                     pl.BlockSpec((1, 128), lambda i: (0, 0))]
        args += [prev, pm, norm_w.reshape(1, 128)]
    out_dtype = BF16 if has_prev else F32
    return pl.pallas_call(
        functools.partial(_gla_kernel, reverse, has_prev),
        grid=(NT,),
        in_specs=in_specs,
        out_specs=[pl.BlockSpec((TS, vw), lambda i: (tmap(i), 0)),
                   pl.BlockSpec((None, GLA_HEADS, GLA_DV, GLA_DK), lambda i: (jnp.minimum(tmap(i), NT_P - 1), 0, 0, 0))],
        out_shape=[jax.ShapeDtypeStruct((M, vw), out_dtype),
                   jax.ShapeDtypeStruct((BATCH, GLA_HEADS, GLA_DV, GLA_DK), F32)],
        scratch_shapes=[pltpu.VMEM((GLA_HEADS, GLA_DV, GLA_DK), F32),
                        pltpu.VMEM((TS, vw), F32)],
        compiler_params=_params(("arbitrary",)),
        name="gla_bwd" if reverse else "gla_fwd",
    )(*args)


MOD_SH1, MOD_SC1, MOD_GT1, MOD_SH2, MOD_SC2, MOD_GT2 = range(N_MOD)


def kernel(x_prompt, x_sample, cache_k, cache_v, state_delta, state_gla, c, c_ctx, w_mod, b_mod, norm1, w_in, pool_w, pool_scale, attn_sink, dn_conv, dn_a_log, dn_dt_bias, dn_norm, gla_w2, gla_b2, gla_norm, w_br_pool, w_br_attn, w_br_delta, w_br_gla, w_out, norm2, w_gate, w_up, w_down, norm_f):
    x = jnp.concatenate([x_prompt.reshape(MP, D), x_sample.reshape(MS, D)], axis=0)
    cond8 = jnp.concatenate([c_ctx[None, :], c, jnp.zeros((8 - 1 - DEC_BATCH, D), F32)], axis=0)
    mod = _modulation(cond8, w_mod, b_mod).reshape(DEPTH, 8, N_MOD, 1, D)
    cos_t, sin_t = _rope_tables()
    s_gla_t = jnp.swapaxes(state_gla, -1, -2)
    w_out_bf16 = w_out.astype(BF16)
    w_brs = tuple(w.astype(BF16) for w in (w_br_pool, w_br_attn, w_br_delta, w_br_gla))
    w_in_t = jnp.swapaxes(w_in, 1, 2)
    gates_t = w_in_t[:, GATE_COL:, :].astype(BF16)

    ks, vs, sds, sgs = [], [], [], []
    for l in range(DEPTH):
        h, ps = _prenorm(x, norm1, mod, w_in_t, l, MOD_SC1, MOD_SH1)
        pm = _mix_proj(h, w_in_t, l)

        ya = _pool(pm, pool_w[l], pool_scale[l])
        yb = _lat_attn(pm, attn_sink[l], cos_t, sin_t, cache_k, cache_v, l, _ctx_attn(pm, attn_sink[l]))

        alog = jnp.zeros((1, 128), F32).at[0, SM_A:SM_A + 8].set(dn_a_log[l].reshape(8))
        dtb = jnp.zeros((1, 128), F32).at[0, SM_A:SM_A + 8].set(dn_dt_bias[l].reshape(8))
        o_b, sd_b, dn_qkv = _dn_call(pm, ps, True, l, dn_conv[l], alog, dtb, state_delta)
        yc, sd_f = _dn_call(pm, ps, False, l, dn_conv[l], alog, dtb, state_delta, prev=(o_b, dn_qkv), norm_w=dn_norm[l])

        w2pad = [jnp.zeros((128, GLA_HEADS * GLA_DK), F32).at[SM_LR + o * GLA_RANK:SM_LR + (o + 1) * GLA_RANK].set(gla_w2[l, o]).astype(BF16)
                 for o in range(2)]
        g_b, sg_b = _gla_call(pm, ps, True, w2pad[1], gla_b2[l, 1].reshape(1, -1), s_gla_t[:, l, 1])
        yd, sg_f = _gla_call(pm, ps, False, w2pad[0], gla_b2[l, 0].reshape(1, -1), s_gla_t[:, l, 0], prev=g_b, norm_w=gla_norm[l])

        merged = _merge(h, gates_t, (ya, yb, yc, yd), w_brs, l)
        x, h2 = _out_proj(merged, w_out_bf16, x, mod, norm2, l, MOD_GT1, MOD_SC2, MOD_SH2)
        act = _ffn_up(h2, w_gate, w_up, l)
        x = _ffn_down(act, w_down, x, mod, l, MOD_GT2)

        ks.append(pm[:MP, C_K:C_K + KV_HEADS * HD].reshape(BATCH, SEQ, KV_HEADS, HD))
        vs.append(pm[:MP, C_V:C_V + KV_HEADS * HD].reshape(BATCH, SEQ, KV_HEADS, HD))
        sds.append(jnp.stack([sd_f, sd_b], axis=1))
        sgs.append(jnp.swapaxes(jnp.stack([sg_f, sg_b], axis=1), -1, -2))

    y = _finalnorm(x, norm_f)
    return (y[:MP].reshape(BATCH, SEQ, D), y[MP:].reshape(DEC_BATCH, DEC_SEQ, D),
            jnp.stack(ks, axis=1), jnp.stack(vs, axis=1), jnp.stack(sds, axis=1), jnp.stack(sgs, axis=1))
```

```python
import functools

import jax
import jax.numpy as jnp
from jax import lax
from jax.experimental import pallas as pl
from jax.experimental.pallas import tpu as pltpu

F32 = jnp.float32
BF16 = jnp.bfloat16

D = 2048
BATCH = 16
SEQ = 256
DEPTH = 4
DEC_BATCH = 4
DEC_SEQ = 2048
PAST = 256
GRID_W = 64
HD = 128
ROPE_THETA = 10000.0
EPS = 1e-6
N_MOD = 6
POOL_WINDOWS = (2, 4, 8, 16)
POOL_W = 512
Q_HEADS = 8
KV_HEADS = 2
GROUP = 4
ATTN_WINDOW = 128
DN_HEADS = 4
DN_DK = 128
DN_W = 512
DN_CONV = 4
GLA_HEADS = 4
GLA_DK = 64
GLA_DV = 128
GLA_RANK = 16
GLA_NORMALIZER = 16.0
FFN = 5632
CHUNK = 64
PREP_CHUNKS = 4

MP = BATCH * SEQ
MS = DEC_BATCH * DEC_SEQ
M = MP + MS
TS = 256
NT_P = MP // TS
TPS = DEC_SEQ // TS
NT_S = MS // TS
NT = NT_P + NT_S
HALO = 8

C_Q, C_POOL, C_DNQKV, C_K, C_V, C_DNZ = 0, 1024, 1536, 3072, 3328, 3584
C_GQ, C_GK, C_GV, C_GR = 4096, 4352, 4608, 5120
MIXW = 5632
MIX_TN = 512
MIX_SRC = (1, 2, 0, 4, 5, 6, 3, 7, 8, 9, 10)
MIX_ALIGNED = 8
MIX_SHIFT = 16
SMALL_COL_A = 4096
SMALL_COL_LR = 5648
GATE_COL = 5680
IN_WIDTH = GATE_COL + 4 * D
SM_A, SM_B, SM_LR = 0, 8, 16

VMEM_LIMIT = 56 * 1024 * 1024


def _params(sem):
    return pltpu.CompilerParams(dimension_semantics=sem, vmem_limit_bytes=VMEM_LIMIT)


def _cond_row(row0):
    return jnp.where(row0 < MP, 0, 1 + (row0 - MP) // DEC_SEQ)


def _tile_info(t):
    is_p = t < NT_P
    s = jnp.maximum(t - NT_P, 0)
    j = s % TPS
    first = jnp.logical_or(is_p, j == 0)
    last = jnp.logical_or(is_p, j == TPS - 1)
    base = jnp.where(is_p, 0, j * TS)
    seq_len = jnp.where(is_p, SEQ, DEC_SEQ)
    return is_p, first, last, base, seq_len


def _silu(x):
    return x * jax.nn.sigmoid(x)


def _softplus(x):
    return jnp.maximum(x, 0.0) + jnp.log(1.0 + jnp.exp(-jnp.abs(x)))


def _dot(a, b):
    return jnp.dot(a, b, preferred_element_type=F32)


def _dot_tb(a, b):
    return lax.dot_general(a, b, (((1,), (1,)), ((), ())), preferred_element_type=F32)


def _dot_ta(a, b):
    return lax.dot_general(a, b, (((0,), (0,)), ((), ())), preferred_element_type=F32)


def _split2(x):
    hi = x.astype(BF16)
    lo = (x - hi.astype(F32)).astype(BF16)
    return hi, lo


def _split3(x):
    p1 = x.astype(BF16)
    r = x - p1.astype(F32)
    p2 = r.astype(BF16)
    p3 = (r - p2.astype(F32)).astype(BF16)
    return p1, p2, p3


def _mod_kernel(c_ref, w_ref, b_ref, o_ref):
    a = _silu(c_ref[...]).astype(BF16)
    o_ref[...] = _dot(a, w_ref[...].astype(BF16)) + b_ref[...]


def _modulation(cond8, w_mod, b_mod):
    tn = 1024
    n = N_MOD * D
    return pl.pallas_call(
        _mod_kernel,
        grid=(DEPTH, n // tn),
        in_specs=[pl.BlockSpec((8, D), lambda l, j: (0, 0)),
                  pl.BlockSpec((None, D, tn), lambda l, j: (l, 0, j)),
                  pl.BlockSpec((None, 1, tn), lambda l, j: (l, 0, j))],
        out_specs=pl.BlockSpec((None, 8, tn), lambda l, j: (l, 0, j)),
        out_shape=jax.ShapeDtypeStruct((DEPTH, 8, n), F32),
        compiler_params=_params(("parallel", "parallel")),
        name="modulation",
    )(cond8, w_mod, b_mod.reshape(DEPTH, 1, n))


def _rms_rows(x, w):
    ms = jnp.mean(x * x, axis=-1, keepdims=True)
    return x * lax.rsqrt(ms + EPS) * w


def _prenorm_kernel(x_ref, nw_ref, sc_ref, sh_ref, wa_ref, wl_ref, h_ref, ps_ref, wb_scr):
    @pl.when(pl.program_id(0) == 0)
    def _():
        pad = jnp.zeros((128 - SM_LR - 2 * GLA_RANK, D), F32)
        wb_scr[...] = jnp.concatenate([wa_ref[0], wl_ref[0], pad], axis=0).astype(BF16)

    y = _rms_rows(x_ref[...], nw_ref[...])
    h = (y * (1.0 + sc_ref[...]) + sh_ref[...]).astype(h_ref.dtype)
    h_ref[...] = h
    ps_ref[...] = _dot_tb(h, wb_scr[...])


def _mod_spec(layer, chunk, tm, tn=D, row_axis=0, col_axis=None):
    def index(*g):
        col = 0 if col_axis is None else g[col_axis]
        return (layer, _cond_row(g[row_axis] * tm), chunk, 0, col)
    return pl.BlockSpec((None, None, None, 1, tn), index)


def _prenorm(x, norm_w, mod, w_in_t, layer, sc_chunk, sh_chunk):
    tm = 512
    return pl.pallas_call(
        _prenorm_kernel,
        grid=(M // tm,),
        in_specs=[pl.BlockSpec((tm, D), lambda i: (i, 0)),
                  pl.BlockSpec((None, 1, D), lambda i: (layer, 0, 0)),
                  _mod_spec(layer, sc_chunk, tm), _mod_spec(layer, sh_chunk, tm),
                  _wt_rows(layer, SM_LR, lambda i: SMALL_COL_A),
                  _wt_rows(layer, 2 * GLA_RANK, lambda i: SMALL_COL_LR)],
        out_specs=[pl.BlockSpec((tm, D), lambda i: (i, 0)), pl.BlockSpec((tm, 128), lambda i: (i, 0))],
        out_shape=[jax.ShapeDtypeStruct((M, D), BF16), jax.ShapeDtypeStruct((M, 128), F32)],
        scratch_shapes=[pltpu.VMEM((128, D), BF16)],
        compiler_params=_params(("arbitrary",)),
        name="prenorm",
    )(x, norm_w.reshape(DEPTH, 1, D), mod, mod, w_in_t, w_in_t)


def _finalnorm_kernel(x_ref, nw_ref, o_ref):
    o_ref[...] = _rms_rows(x_ref[...], nw_ref[...])


def _finalnorm(x, nw):
    tm = 512
    return pl.pallas_call(
        _finalnorm_kernel,
        grid=(M // tm,),
        in_specs=[pl.BlockSpec((tm, D), lambda i: (i, 0)),
                  pl.BlockSpec((1, D), lambda i: (0, 0))],
        out_specs=pl.BlockSpec((tm, D), lambda i: (i, 0)),
        out_shape=jax.ShapeDtypeStruct((M, D), F32),
        compiler_params=_params(("parallel",)),
        name="finalnorm",
    )(x, nw.reshape(1, D))


def _first_row_block():
    return pl.program_id(1) == 0


def _wt_rows(layer, nrows, row_fn):
    def index(*g):
        row = row_fn(*g)
        return (layer, row if isinstance(row, int) else pl.multiple_of(row, 8), 0)
    return pl.BlockSpec((pl.Element(1), pl.Element(nrows), pl.Element(D)), index)


def _mix_kernel(h_ref, w_ref, o_ref, wb_scr):
    @pl.when(_first_row_block())
    def _():
        wb_scr[...] = w_ref[0].astype(BF16)

    o_ref[...] = _dot_tb(h_ref[...], wb_scr[...])


def _mix_src_row(j):
    src = j
    for out_tile, src_tile in enumerate(MIX_SRC):
        src = jnp.where(j == out_tile, src_tile, src)
    return src * MIX_TN + jnp.where(src >= MIX_ALIGNED, MIX_SHIFT, 0)


def _mix_proj(h, w_in_t, layer):
    tm, tn = 2048, MIX_TN
    return pl.pallas_call(
        _mix_kernel,
        grid=(MIXW // tn, M // tm),
        in_specs=[pl.BlockSpec((tm, D), lambda j, i: (i, 0)),
                  _wt_rows(layer, tn, lambda j, i: _mix_src_row(j))],
        out_specs=pl.BlockSpec((tm, tn), lambda j, i: (i, j)),
        out_shape=jax.ShapeDtypeStruct((M, MIXW), F32),
        scratch_shapes=[pltpu.VMEM((tn, D), BF16)],
        compiler_params=_params(("arbitrary", "arbitrary")),
        name="mix_proj",
    )(h, w_in_t)


def _ffn_up_kernel(h_ref, wg_ref, wu_ref, o_ref, wg_scr, wu_scr):
    @pl.when(_first_row_block())
    def _():
        wg_scr[...] = wg_ref[...].astype(BF16)
        wu_scr[...] = wu_ref[...].astype(BF16)

    h = h_ref[...]
    o_ref[...] = (_silu(_dot(h, wg_scr[...])) * _dot(h, wu_scr[...])).astype(o_ref.dtype)


def _ffn_up(h, w_gate, w_up, layer):
    tm, tn = 2048, 512
    wspec = pl.BlockSpec((None, D, tn), lambda j, i: (layer, 0, j))
    return pl.pallas_call(
        _ffn_up_kernel,
        grid=(FFN // tn, M // tm),
        in_specs=[pl.BlockSpec((tm, D), lambda j, i: (i, 0)), wspec, wspec],
        out_specs=pl.BlockSpec((tm, tn), lambda j, i: (i, j)),
        out_shape=jax.ShapeDtypeStruct((M, FFN), BF16),
        scratch_shapes=[pltpu.VMEM((D, tn), BF16), pltpu.VMEM((D, tn), BF16)],
        compiler_params=_params(("arbitrary", "arbitrary")),
        name="ffn_up",
    )(h, w_gate, w_up)


def _ffn_down_kernel(a_ref, w_ref, x_ref, gt_ref, o_ref, wb_scr):
    @pl.when(_first_row_block())
    def _():
        wb_scr[...] = w_ref[...].astype(BF16)

    o_ref[...] = x_ref[...] + gt_ref[...] * _dot(a_ref[...], wb_scr[...])


def _ffn_down(act, w_down, x, mod, layer, gt_chunk):
    tm, tn = 1024, 512
    return pl.pallas_call(
        _ffn_down_kernel,
        grid=(D // tn, M // tm),
        in_specs=[pl.BlockSpec((tm, FFN), lambda j, i: (i, 0)),
                  pl.BlockSpec((None, FFN, tn), lambda j, i: (layer, 0, j), pipeline_mode=pl.Buffered(1)),
                  pl.BlockSpec((tm, tn), lambda j, i: (i, j)),
                  _mod_spec(layer, gt_chunk, tm, tn, row_axis=1, col_axis=0)],
        out_specs=pl.BlockSpec((tm, tn), lambda j, i: (i, j)),
        out_shape=jax.ShapeDtypeStruct((M, D), F32),
        scratch_shapes=[pltpu.VMEM((FFN, tn), BF16)],
        compiler_params=_params(("arbitrary", "arbitrary")),
        name="ffn_down",
    )(act, w_down, x, mod)


def _gate_cast_kernel(w_ref, o_ref):
    o_ref[...] = w_ref[0].astype(o_ref.dtype)


def _gate_weights(w_in_t):
    tr = 1024
    return pl.pallas_call(
        _gate_cast_kernel,
        grid=(DEPTH, 4 * D // tr),
        in_specs=[pl.BlockSpec((pl.Element(1), pl.Element(tr), pl.Element(D)),
                               lambda l, j: (l, pl.multiple_of(GATE_COL + j * tr, 8), 0))],
        out_specs=pl.BlockSpec((None, tr, D), lambda l, j: (l, j, 0)),
        out_shape=jax.ShapeDtypeStruct((DEPTH, 4 * D, D), BF16),
        compiler_params=_params(("parallel", "parallel")),
        name="gate_cast",
    )(w_in_t)


BR_WIDTHS = (POOL_W, Q_HEADS * HD, DN_W, GLA_HEADS * GLA_DV)


def _merge_kernel(h_ref, *refs):
    g_refs, y_refs, w_refs, o_ref = refs[0:4], refs[4:8], refs[8:12], refs[12]
    h = h_ref[...]
    acc = None
    for b in range(4):
        term = jax.nn.sigmoid(_dot_tb(h, g_refs[b][...])) * _dot(y_refs[b][...], w_refs[b][...])
        acc = term if acc is None else acc + term
    o_ref[...] = acc.astype(o_ref.dtype)


def _merge(h, gates_t, ys, w_brs, layer):
    tm, tn = 1024, 512
    nj = D // tn
    gate_specs = [pl.BlockSpec((None, tn, D), functools.partial(lambda i, j, b: (layer, b * nj + j, 0), b=b))
                  for b in range(4)]
    y_specs = [pl.BlockSpec((tm, w), lambda i, j: (i, 0)) for w in BR_WIDTHS]
    w_specs = [pl.BlockSpec((None, w, tn), lambda i, j: (layer, 0, j)) for w in BR_WIDTHS]
    return pl.pallas_call(
        _merge_kernel,
        grid=(M // tm, nj),
        in_specs=[pl.BlockSpec((tm, D), lambda i, j: (i, 0))] + gate_specs + y_specs + w_specs,
        out_specs=pl.BlockSpec((tm, tn), lambda i, j: (i, j)),
        out_shape=jax.ShapeDtypeStruct((M, D), BF16),
        compiler_params=_params(("parallel", "arbitrary")),
        name="merge",
    )(h, *([gates_t] * 4), *ys, *w_brs)


def _out_proj_kernel(m_ref, w_ref, x_ref, gt_ref, nw_ref, sc_ref, sh_ref, xo_ref, h_ref):
    xn = x_ref[...] + gt_ref[...] * _dot(m_ref[...], w_ref[...])
    xo_ref[...] = xn
    h_ref[...] = (_rms_rows(xn, nw_ref[...]) * (1.0 + sc_ref[...]) + sh_ref[...]).astype(h_ref.dtype)


def _out_proj(merged, w_out_bf16, x, mod, norm_w, layer, gt_chunk, sc_chunk, sh_chunk):
    tm = 512
    row = pl.BlockSpec((tm, D), lambda i: (i, 0))
    return pl.pallas_call(
        _out_proj_kernel,
        grid=(M // tm,),
        in_specs=[row,
                  pl.BlockSpec((None, D, D), lambda i: (layer, 0, 0), pipeline_mode=pl.Buffered(1)),
                  row, _mod_spec(layer, gt_chunk, tm),
                  pl.BlockSpec((None, 1, D), lambda i: (layer, 0, 0)),
                  _mod_spec(layer, sc_chunk, tm), _mod_spec(layer, sh_chunk, tm)],
        out_specs=[row, row],
        out_shape=[jax.ShapeDtypeStruct((M, D), F32), jax.ShapeDtypeStruct((M, D), BF16)],
        compiler_params=_params(("parallel",)),
        name="out_proj",
    )(merged, w_out_bf16, x, mod, norm_w.reshape(DEPTH, 1, D), mod, mod)


def _pool_kernel(u_ref, up_ref, un_ref, w_ref, sc_ref, o_ref):
    t = pl.program_id(0)
    _, first, last, base, seq_len = _tile_info(t)
    u = u_ref[...]
    prev = jnp.where(first, 0.0, up_ref[...])
    nxt = jnp.where(last, 0.0, un_ref[...])
    ext = jnp.concatenate([prev, u, nxt], axis=0)
    pos = base + lax.broadcasted_iota(jnp.int32, (TS, 1), 0)
    outs = []
    for gi, win in enumerate(POOL_WINDOWS):
        lanes = slice(gi * 128, (gi + 1) * 128)
        p = ext[:, lanes]
        s = 1
        while s < win:
            p = p + pltpu.roll(p, s, 0)
            s *= 2
        d = HALO + win // 2 - 1
        ssum = p[d:d + TS]
        cnt = (jnp.minimum(pos + win // 2, seq_len) - jnp.maximum(pos - win // 2, 0)).astype(F32)
        pg = ssum / cnt - u[:, lanes]
        outs.append(_dot(pg.astype(BF16), w_ref[gi].astype(BF16)))
    o_ref[...] = (jnp.concatenate(outs, axis=1) * sc_ref[...]).astype(o_ref.dtype)


def _halo_specs(width, colblk):
    r = TS // HALO
    nblk = M // HALO
    prev = pl.BlockSpec((HALO, width), lambda t: (jnp.maximum(t * r - 1, 0), colblk))
    nxt = pl.BlockSpec((HALO, width), lambda t: (jnp.minimum((t + 1) * r, nblk - 1), colblk))
    return prev, nxt


def _pool(pm, pool_w, pool_scale):
    cb = C_POOL // POOL_W
    prev, nxt = _halo_specs(POOL_W, cb)
    return pl.pallas_call(
        _pool_kernel,
        grid=(NT,),
        in_specs=[pl.BlockSpec((TS, POOL_W), lambda t: (t, cb)), prev, nxt,
                  pl.BlockSpec((4, 128, 128), lambda t: (0, 0, 0)),
                  pl.BlockSpec((1, POOL_W), lambda t: (0, 0))],
        out_specs=pl.BlockSpec((TS, POOL_W), lambda t: (t, 0)),
        out_shape=jax.ShapeDtypeStruct((M, POOL_W), BF16),
        compiler_params=_params(("parallel",)),
        name="pool",
    )(pm, pm, pm, pool_w, pool_scale.reshape(1, POOL_W))


def _ctx_attn_kernel(sink_ref, q_ref, k_ref, v_ref, o_ref):
    q = q_ref[...] * (HD ** -0.5)
    k = k_ref[...].astype(BF16)
    v = v_ref[...].astype(BF16)
    for n in range(KV_HEADS):
        kn = k[:, n * HD:(n + 1) * HD]
        vn = v[:, n * HD:(n + 1) * HD]
        for g in range(GROUP):
            h = n * GROUP + g
            qh = q[:, h * HD:(h + 1) * HD].astype(BF16)
            s = _dot_tb(qh, kn)
            snk = sink_ref[h]
            m = jnp.maximum(jnp.max(s, axis=-1, keepdims=True), snk)
            e = jnp.exp(s - m)
            den = jnp.sum(e, axis=-1, keepdims=True) + jnp.exp(snk - m)
            o = _dot(e.astype(BF16), vn) / den
            o_ref[:, h * HD:(h + 1) * HD] = o.astype(o_ref.dtype)


def _ctx_attn(pm, sink):
    qw, kw = Q_HEADS * HD, KV_HEADS * HD
    return pl.pallas_call(
        _ctx_attn_kernel,
        grid=(NT_P,),
        in_specs=[pl.BlockSpec(memory_space=pltpu.SMEM),
                  pl.BlockSpec((TS, qw), lambda t: (t, C_Q // qw)),
                  pl.BlockSpec((TS, kw), lambda t: (t, C_K // kw)),
                  pl.BlockSpec((TS, kw), lambda t: (t, C_V // kw))],
        out_specs=pl.BlockSpec((TS, qw), lambda t: (t, 0)),
        out_shape=jax.ShapeDtypeStruct((M, qw), BF16),
        compiler_params=_params(("parallel",)),
        name="ctx_attn",
    )(sink, pm, pm, pm)


def _rope(x, cs, sn):
    lane = lax.broadcasted_iota(jnp.int32, x.shape, 1)
    swapped = jnp.where((lane % 64) < 32, pltpu.roll(x, 96, 1), pltpu.roll(x, 32, 1))
    return x * cs + swapped * sn


def _lat_attn_kernel(sink_ref, q_ref, km_ref, kp_ref, kn_ref, vm_ref, vp_ref, vn_ref,
                     cm_ref, cp_ref, cn_ref, sm_ref, sp_ref, sn_ref, kc_ref, vc_ref, yb_ref, o_ref):
    del yb_ref
    ts = pl.program_id(0)
    base = (ts % TPS) * TS
    half = TS // 2
    nk = TS + 2 * half
    q = q_ref[...] * (HD ** -0.5)
    cq, sq = cm_ref[...], sm_ref[...]
    kl = jnp.concatenate([kp_ref[...], km_ref[...], kn_ref[...]], axis=0)
    vl = jnp.concatenate([vp_ref[...], vm_ref[...], vn_ref[...]], axis=0).astype(BF16)
    ck = jnp.concatenate([cp_ref[...], cq, cn_ref[...]], axis=0)
    sk = jnp.concatenate([sp_ref[...], sq, sn_ref[...]], axis=0)
    kc = kc_ref[...].astype(BF16)
    vc = vc_ref[...].astype(BF16)
    qpos = base + lax.broadcasted_iota(jnp.int32, (TS, 1), 0)
    kpos = base - half + lax.broadcasted_iota(jnp.int32, (1, nk), 1)
    dist = qpos - kpos
    ok = (jnp.abs(dist) <= ATTN_WINDOW).astype(jnp.int32) * (kpos >= 0).astype(jnp.int32) * (kpos < DEC_SEQ).astype(jnp.int32)
    bias = jnp.where(ok > 0, 0.0, -1e30)
    for n in range(KV_HEADS):
        lanes = slice(n * HD, (n + 1) * HD)
        kr = _rope(kl[:, lanes], ck, sk).astype(BF16)
        vln, kcn, vcn = vl[:, lanes], kc[:, lanes], vc[:, lanes]
        for g in range(GROUP):
            h = n * GROUP + g
            qh = _rope(q[:, h * HD:(h + 1) * HD], cq, sq).astype(BF16)
            s_c = _dot_tb(qh, kcn)
            s_l = _dot_tb(qh, kr) + bias
            snk = sink_ref[h]
            m = jnp.maximum(jnp.maximum(jnp.max(s_c, axis=-1, keepdims=True),
                                        jnp.max(s_l, axis=-1, keepdims=True)), snk)
            e_c = jnp.exp(s_c - m)
            e_l = jnp.exp(s_l - m)
            den = jnp.sum(e_c, axis=-1, keepdims=True) + jnp.sum(e_l, axis=-1, keepdims=True) + jnp.exp(snk - m)
            o = (_dot(e_c.astype(BF16), vcn) + _dot(e_l.astype(BF16), vln)) / den
            o_ref[:, h * HD:(h + 1) * HD] = o.astype(o_ref.dtype)


def _lat_attn(pm, sink, cos_t, sin_t, cache_k, cache_v, layer, yb):
    qw, kw = Q_HEADS * HD, KV_HEADS * HD
    half = TS // 2
    nhb = M // half
    nhs = DEC_SEQ // half

    def main(w, cb):
        return pl.BlockSpec((TS, w), lambda ts: (NT_P + ts, cb))

    def prev(cb):
        return pl.BlockSpec((half, kw), lambda ts: (2 * (NT_P + ts) - 1, cb))

    def nxt(cb):
        return pl.BlockSpec((half, kw), lambda ts: (jnp.minimum(2 * (NT_P + ts) + 2, nhb - 1), cb))

    tab_m = pl.BlockSpec((TS, HD), lambda ts: (ts % TPS, 0))
    tab_p = pl.BlockSpec((half, HD), lambda ts: (jnp.maximum(2 * (ts % TPS) - 1, 0), 0))
    tab_n = pl.BlockSpec((half, HD), lambda ts: (jnp.minimum(2 * (ts % TPS) + 2, nhs - 1), 0))
    ctx = pl.BlockSpec((None, None, PAST, kw), lambda ts: (ts // TPS, layer, 0, 0))
    return pl.pallas_call(
        _lat_attn_kernel,
        grid=(NT_S,),
        in_specs=[pl.BlockSpec(memory_space=pltpu.SMEM),
                  main(qw, C_Q // qw),
                  main(kw, C_K // kw), prev(C_K // kw), nxt(C_K // kw),
                  main(kw, C_V // kw), prev(C_V // kw), nxt(C_V // kw),
                  tab_m, tab_p, tab_n, tab_m, tab_p, tab_n, ctx, ctx,
                  pl.BlockSpec(memory_space=pl.ANY)],
        out_specs=pl.BlockSpec((TS, qw), lambda ts: (NT_P + ts, 0)),
        out_shape=jax.ShapeDtypeStruct((M, qw), BF16),
        input_output_aliases={16: 0},
        compiler_params=_params(("parallel",)),
        name="lat_attn",
    )(sink, pm, pm, pm, pm, pm, pm, pm, cos_t, cos_t, cos_t, sin_t, sin_t, sin_t,
      cache_k.reshape(DEC_BATCH, DEPTH, PAST, kw), cache_v.reshape(DEC_BATCH, DEPTH, PAST, kw), yb)


def _rope_tables():
    quarter = HD // 4
    inv = ROPE_THETA ** (-jnp.arange(quarter, dtype=F32) / quarter)
    t = jnp.arange(DEC_SEQ)
    row = (t // GRID_W).astype(F32)
    col = (t % GRID_W).astype(F32)
    ar = row[:, None] * inv[None, :]
    ac = col[:, None] * inv[None, :]
    cos_t = jnp.concatenate([jnp.cos(ar), jnp.cos(ar), jnp.cos(ac), jnp.cos(ac)], axis=-1)
    sin_t = jnp.concatenate([-jnp.sin(ar), jnp.sin(ar), -jnp.sin(ac), jnp.sin(ac)], axis=-1)
    return cos_t, sin_t


def _scan_tile(reverse):
    i = pl.program_id(0)
    return (NT - 1 - i) if reverse else i


def _tri_masks(reverse):
    ii = lax.broadcasted_iota(jnp.int32, (CHUNK, CHUNK), 0)
    jj = lax.broadcasted_iota(jnp.int32, (CHUNK, CHUNK), 1)
    incl = (jj >= ii) if reverse else (jj <= ii)
    strict = (jj > ii) if reverse else (jj < ii)
    return incl, strict


def _gated_rms_out(o, gate, nw):
    outs = []
    for h in range(4):
        lanes = slice(h * 128, (h + 1) * 128)
        outs.append(_rms_rows(o[:, lanes], nw) * _silu(gate[:, lanes]))
    return jnp.concatenate(outs, axis=1)


def _dot_hp_all(xs, ys):
    hh = [_dot(x[0], y[0]) for x, y in zip(xs, ys)]
    hl = [_dot(x[0], y[1]) for x, y in zip(xs, ys)]
    lh = [_dot(x[1], y[0]) for x, y in zip(xs, ys)]
    return [a + (b + c) for a, b, c in zip(hh, hl, lh)]


def _inv_unit_tri_all(lms):
    ii = lax.broadcasted_iota(jnp.int32, (CHUNK, CHUNK), 0)
    jj = lax.broadcasted_iota(jnp.int32, (CHUNK, CHUNK), 1)
    eye = jnp.where(ii == jj, 1.0, 0.0)
    ps = [eye - lm for lm in lms]
    mbs = [lm.astype(BF16) for lm in lms]
    mks = [_dot(m, m) for m in mbs]
    span = 4
    while span <= CHUNK:
        mbs = [m.astype(BF16) for m in mks]
        incs = [_dot(p.astype(BF16), m) for p, m in zip(ps, mbs)]
        if span < CHUNK:
            mks = [_dot(m, m) for m in mbs]
        ps = [p + inc for p, inc in zip(ps, incs)]
        span *= 2
    lts = _dot_hp_all([_split2(lm) for lm in lms], [_split2(t) for t in ps])
    rs = [(eye - t) - lt for t, lt in zip(ps, lts)]
    return [t + _dot(t.astype(BF16), r.astype(BF16)) for t, r in zip(ps, rs)]


def _dn_kernel(reverse, has_prev, *refs):
    if has_prev:
        qkv_ref, sm_ref, alog_ref, dtb_ref, s0_ref, oprev_ref, z_ref, nw_ref, o_ref, st_ref = refs[:10]
        scratch = refs[10:]
    else:
        x_ref, xp_ref, xn_ref, sm_ref, cw_ref, alog_ref, dtb_ref, s0_ref, o_ref, st_ref, qkv_out_ref = refs[:11]
        scratch = refs[11:]
    (s_scr, q_scr, k_scr, v_scr, b_scr, gcc_scr, gcr_scr, u_scr, w_scr, qe_scr, kd_scr, a_scr, o_scr) = scratch
    t = _scan_tile(reverse)
    is_p, first, last, _, _ = _tile_info(t)
    d = 1 if reverse else 0
    nchunk = TS // CHUNK
    e_last = 0 if reverse else CHUNK - 1

    @pl.when(last if reverse else first)
    def _():
        s_scr[...] = jnp.where(is_p, 0.0, s0_ref[...])

    if has_prev:
        q_scr[...] = qkv_ref[:, 0:DN_W]
        k_scr[...] = qkv_ref[:, DN_W:2 * DN_W]
        v_scr[...] = qkv_ref[:, 2 * DN_W:3 * DN_W]
    else:
        x = x_ref[...]
        ext = jnp.concatenate([jnp.where(first, 0.0, xp_ref[...]), x, jnp.where(last, 0.0, xn_ref[...])], axis=0)
        cw = cw_ref[...]
        ext_rows = TS + 2 * HALO
        conv = None
        for j in range(DN_CONV):
            off = j - DN_CONV // 2
            rows = ext if off == 0 else pltpu.roll(ext, (ext_rows - off) % ext_rows, 0)
            term = rows[HALO:HALO + TS] * cw[j:j + 1, :]
            conv = term if conv is None else conv + term
        act = _silu(conv)
        for h in range(DN_HEADS):
            lanes = slice(h * 128, (h + 1) * 128)
            qh = act[:, h * 128:(h + 1) * 128]
            kh = act[:, DN_W + h * 128:DN_W + (h + 1) * 128]
            q_scr[:, lanes] = qh * lax.rsqrt(jnp.sum(qh * qh, axis=-1, keepdims=True) + EPS) * (DN_DK ** -0.5)
            k_scr[:, lanes] = kh * lax.rsqrt(jnp.sum(kh * kh, axis=-1, keepdims=True) + EPS)
        v_scr[...] = act[:, 2 * DN_W:3 * DN_W]
        qkv_out_ref[:, 0:DN_W] = q_scr[...]
        qkv_out_ref[:, DN_W:2 * DN_W] = k_scr[...]
        qkv_out_ref[:, 2 * DN_W:3 * DN_W] = v_scr[...]
    sm = sm_ref[...]
    b_scr[...] = jax.nn.sigmoid(sm)
    g = -jnp.exp(alog_ref[...]) * _softplus(sm + dtb_ref[...])

    ti = lax.broadcasted_iota(jnp.int32, (TS, TS), 0)
    tj = lax.broadcasted_iota(jnp.int32, (TS, TS), 1)
    same = (ti // CHUNK) == (tj // CHUNK)
    lower = jnp.where(jnp.logical_and(same, tj <= ti), 1.0, 0.0).astype(BF16)
    upper = jnp.where(jnp.logical_and(same, tj >= ti), 1.0, 0.0).astype(BF16)
    tri, tri_t = (upper, lower) if reverse else (lower, upper)
    g1, g2, g3 = _split3(g)
    gcc_scr[...] = _dot(tri, g1) + _dot(tri, g2) + _dot(tri, g3)
    gcr = _dot_ta(g1, tri_t) + _dot_ta(g2, tri_t) + _dot_ta(g3, tri_t)
    for c in range(nchunk):
        gcr_scr[c] = gcr[:, c * CHUNK:(c + 1) * CHUNK]

    incl, strict = _tri_masks(reverse)

    def prep_body(pi, carry):
        rows, gblk, gcr_c, bt = {}, {}, {}, {}
        for cc in range(PREP_CHUNKS):
            c = pi * PREP_CHUNKS + cc
            rows[cc] = pl.ds(c * CHUNK, CHUNK)
            gblk[cc] = gcc_scr[rows[cc], :]
            gcr_c[cc] = gcr_scr[c]
            bt[cc] = b_scr[rows[cc], :]
        info = []
        for cc in range(PREP_CHUNKS):
            for h in range(DN_HEADS):
                lanes = slice(h * 128, (h + 1) * 128)
                r = SM_A + d * DN_HEADS + h
                gcol = gblk[cc][:, r:r + 1]
                grow = gcr_c[cc][r:r + 1, :]
                glast = gblk[cc][e_last:e_last + 1, r:r + 1]
                beta = bt[cc][:, SM_B + d * DN_HEADS + h:SM_B + d * DN_HEADS + h + 1]
                qc, kc, vc = q_scr[rows[cc], lanes], k_scr[rows[cc], lanes], v_scr[rows[cc], lanes]
                decay = jnp.exp(jnp.where(incl, gcol - grow, -1e30))
                kb = kc * beta
                eg = jnp.exp(gcol)
                qe_scr[rows[cc], lanes] = (qc * eg).astype(BF16)
                kd_scr[rows[cc], lanes] = (kc * jnp.exp(glast - gcol)).astype(BF16)
                info.append(dict(cc=cc, h=h, lanes=lanes, decay=decay, kbb=kb.astype(BF16), kcb=kc.astype(BF16),
                                 qcb=qc.astype(BF16), vbb=(vc * beta).astype(BF16), kgb=(kb * eg).astype(BF16)))
        kk = [_dot_tb(p["kbb"], p["kcb"]) for p in info]
        qk = [_dot_tb(p["qcb"], p["kcb"]) for p in info]
        for p, m in zip(info, qk):
            a_scr[rows[p["cc"]], p["h"] * CHUNK:(p["h"] + 1) * CHUNK] = (m * p["decay"]).astype(BF16)
        lms = [jnp.where(strict, m * p["decay"], 0.0) for p, m in zip(info, kk)]
        tinvs = [tv.astype(BF16) for tv in _inv_unit_tri_all(lms)]
        us = [_dot(tv, p["vbb"]) for tv, p in zip(tinvs, info)]
        ws = [_dot(tv, p["kgb"]) for tv, p in zip(tinvs, info)]
        for p, u, w in zip(info, us, ws):
            u_scr[rows[p["cc"]], p["lanes"]] = u
            w_scr[rows[p["cc"]], p["lanes"]] = w.astype(BF16)
        return carry

    for pi in range(nchunk // PREP_CHUNKS):
        prep_body(pi, 0)

    def chunk_body(ci, carry):
        c = (nchunk - 1 - ci) if reverse else ci
        rows = pl.ds(c * CHUNK, CHUNK)
        glast_row = gcc_scr[pl.ds(c * CHUNK + e_last, 1), :]
        heads = range(DN_HEADS)
        lanes = [slice(h * 128, (h + 1) * 128) for h in heads]
        s_old = [s_scr[h] for h in heads]
        sb = [s.astype(BF16) for s in s_old]
        ws = [_dot(w_scr[rows, lanes[h]], sb[h]) for h in heads]
        qs = [_dot(qe_scr[rows, lanes[h]], sb[h]) for h in heads]
        vnb = [(u_scr[rows, lanes[h]] - ws[h]).astype(BF16) for h in heads]
        av = [_dot(a_scr[rows, h * CHUNK:(h + 1) * CHUNK], vnb[h]) for h in heads]
        up = [_dot_ta(kd_scr[rows, lanes[h]], vnb[h]) for h in heads]
        for h in heads:
            r = SM_A + d * DN_HEADS + h
            o_scr[rows, lanes[h]] = qs[h] + av[h]
            s_scr[h] = jnp.exp(glast_row[:, r:r + 1]) * s_old[h] + up[h]
        return carry

    for ci in range(nchunk):
        chunk_body(ci, 0)

    if has_prev:
        o_ref[...] = _gated_rms_out(o_scr[...] + oprev_ref[...], z_ref[...], nw_ref[...]).astype(o_ref.dtype)
    else:
        o_ref[...] = o_scr[...]

    @pl.when(is_p)
    def _():
        st_ref[...] = s_scr[...]


def _seq_of_tile(t):
    return jnp.maximum(t - NT_P, 0) // TPS


def _dn_call(pm, ps, reverse, layer, conv_w, alog, dtb, state_delta, prev=None, norm_w=None):
    d = 1 if reverse else 0
    tmap = (lambda i: NT - 1 - i) if reverse else (lambda i: i)
    width = 3 * DN_W
    cb = C_DNQKV // width
    r = TS // HALO
    nblk = M // HALO
    small = [pl.BlockSpec((TS, 128), lambda i: (tmap(i), 0))]
    params = [pl.BlockSpec((1, 128), lambda i: (0, 0)),
              pl.BlockSpec((1, 128), lambda i: (0, 0)),
              pl.BlockSpec((None, None, None, DN_HEADS, DN_DK, 128), lambda i: (_seq_of_tile(tmap(i)), layer, d, 0, 0, 0))]
    tile_w = lambda w: pl.BlockSpec((TS, w), lambda i: (tmap(i), 0))
    has_prev = prev is not None
    out_specs = [tile_w(DN_W),
                 pl.BlockSpec((None, DN_HEADS, DN_DK, 128), lambda i: (jnp.minimum(tmap(i), NT_P - 1), 0, 0, 0))]
    out_shape = [jax.ShapeDtypeStruct((M, DN_W), BF16 if has_prev else F32),
                 jax.ShapeDtypeStruct((BATCH, DN_HEADS, DN_DK, 128), F32)]
    if has_prev:
        o_first, qkv = prev
        in_specs = [tile_w(width)] + small + params + [
            tile_w(DN_W), pl.BlockSpec((TS, DN_W), lambda i: (tmap(i), C_DNZ // DN_W)), pl.BlockSpec((1, 128), lambda i: (0, 0))]
        args = [qkv, ps, alog, dtb, state_delta, o_first, pm, norm_w.reshape(1, 128)]
    else:
        in_specs = [pl.BlockSpec((TS, width), lambda i: (tmap(i), cb)),
                    pl.BlockSpec((HALO, width), lambda i: (jnp.maximum(tmap(i) * r - 1, 0), cb)),
                    pl.BlockSpec((HALO, width), lambda i: (jnp.minimum((tmap(i) + 1) * r, nblk - 1), cb))] + small + [
            pl.BlockSpec((DN_CONV, width), lambda i: (0, 0))] + params
        args = [pm, pm, pm, ps, conv_w, alog, dtb, state_delta]
        out_specs.append(tile_w(width))
        out_shape.append(jax.ShapeDtypeStruct((M, width), F32))
    return pl.pallas_call(
        functools.partial(_dn_kernel, reverse, has_prev),
        grid=(NT,),
        in_specs=in_specs,
        out_specs=out_specs,
        out_shape=out_shape,
        scratch_shapes=[pltpu.VMEM((DN_HEADS, DN_DK, 128), F32),
                        pltpu.VMEM((TS, DN_W), F32), pltpu.VMEM((TS, DN_W), F32), pltpu.VMEM((TS, DN_W), F32),
                        pltpu.VMEM((TS, 128), F32), pltpu.VMEM((TS, 128), F32),
                        pltpu.VMEM((TS // CHUNK, 128, CHUNK), F32),
                        pltpu.VMEM((TS, DN_W), F32), pltpu.VMEM((TS, DN_W), BF16),
                        pltpu.VMEM((TS, DN_W), BF16), pltpu.VMEM((TS, DN_W), BF16),
                        pltpu.VMEM((TS, DN_HEADS * CHUNK), BF16),
                        pltpu.VMEM((TS, DN_W), F32)],
        compiler_params=_params(("arbitrary",)),
        name="deltanet_bwd" if reverse else "deltanet_fwd",
    )(*args)


def _gla_kernel(reverse, has_prev, q_ref, k_ref, v_ref, sm_ref, w2_ref, b2_ref, s0_ref, *rest):
    if has_prev:
        oprev_ref, r_ref, nw_ref, o_ref, st_ref, s_scr, o_scr = rest
    else:
        o_ref, st_ref, s_scr, o_scr = rest
    t = _scan_tile(reverse)
    is_p, first, last, _, _ = _tile_info(t)

    @pl.when(last if reverse else first)
    def _():
        s_scr[...] = jnp.where(is_p, 0.0, s0_ref[...])

    logits = _dot(sm_ref[...].astype(BF16), w2_ref[...]) + b2_ref[...]
    gk = -_softplus(-logits) / GLA_NORMALIZER

    ti = lax.broadcasted_iota(jnp.int32, (TS, TS), 0)
    tj = lax.broadcasted_iota(jnp.int32, (TS, TS), 1)
    same = (ti // CHUNK) == (tj // CHUNK)
    tri = jnp.where(jnp.logical_and(same, (tj >= ti) if reverse else (tj <= ti)), 1.0, 0.0).astype(BF16)
    g1, g2, g3 = _split3(gk)
    gc = _dot(tri, g1) + _dot(tri, g2) + _dot(tri, g3)

    incl, _ = _tri_masks(reverse)
    nchunk = TS // CHUNK
    e_last = 0 if reverse else CHUNK - 1
    eg = jnp.exp(gc)
    q_all = q_ref[...] * (GLA_DK ** -0.5)
    k_all = k_ref[...]
    qg_all = (q_all * eg).astype(BF16)
    kg_all = (k_all * jnp.exp(-gc)).astype(BF16)
    v_all = v_ref[...].astype(BF16)

    probs = [(c, h) for c in range(nchunk) for h in range(GLA_HEADS)]
    rws = {c: slice(c * CHUNK, (c + 1) * CHUNK) for c in range(nchunk)}
    kls = {h: slice(h * GLA_DK, (h + 1) * GLA_DK) for h in range(GLA_HEADS)}
    vls = {h: slice(h * GLA_DV, (h + 1) * GLA_DV) for h in range(GLA_HEADS)}
    glast = {c: gc[c * CHUNK + e_last:c * CHUNK + e_last + 1, :] for c in range(nchunk)}
    kd = {c: (k_all[rws[c], :] * jnp.exp(glast[c] - gc[rws[c], :])).astype(BF16) for c in range(nchunk)}
    a = {p: jnp.where(incl, _dot_tb(qg_all[rws[p[0]], kls[p[1]]], kg_all[rws[p[0]], kls[p[1]]]), 0.0).astype(BF16)
         for p in probs}
    av = {p: _dot(a[p], v_all[rws[p[0]], vls[p[1]]]) for p in probs}
    up = {p: _dot_ta(v_all[rws[p[0]], vls[p[1]]], kd[p[0]][:, kls[p[1]]]) for p in probs}
    order = list(reversed(range(nchunk))) if reverse else list(range(nchunk))
    s_in = {}
    for h in range(GLA_HEADS):
        s = s_scr[h]
        for c in order:
            s_in[(c, h)] = s.astype(BF16)
            s = jnp.exp(glast[c][:, kls[h]]) * s + up[(c, h)]
        s_scr[h] = s
    for p in probs:
        c, h = p
        o_scr[rws[c], vls[h]] = _dot_tb(qg_all[rws[c], kls[h]], s_in[p]) + av[p]

    if has_prev:
        o_ref[...] = _gated_rms_out(o_scr[...] + oprev_ref[...], r_ref[...], nw_ref[...]).astype(o_ref.dtype)
    else:
        o_ref[...] = o_scr[...]

    @pl.when(is_p)
    def _():
        st_ref[...] = s_scr[...]


def _gla_call(pm, ps, reverse, w2pad, b2, s0t, prev=None, norm_w=None):
    tmap = (lambda i: NT - 1 - i) if reverse else (lambda i: i)
    qkw = GLA_HEADS * GLA_DK
    vw = GLA_HEADS * GLA_DV
    in_specs = [
        pl.BlockSpec((TS, qkw), lambda i: (tmap(i), C_GQ // qkw)),
        pl.BlockSpec((TS, qkw), lambda i: (tmap(i), C_GK // qkw)),
        pl.BlockSpec((TS, vw), lambda i: (tmap(i), C_GV // vw)),
        pl.BlockSpec((TS, 128), lambda i: (tmap(i), 0)),
        pl.BlockSpec((128, qkw), lambda i: (0, 0)),
        pl.BlockSpec((1, qkw), lambda i: (0, 0)),
        pl.BlockSpec((None, GLA_HEADS, GLA_DV, GLA_DK), lambda i: (_seq_of_tile(tmap(i)), 0, 0, 0)),
    ]
    args = [pm, pm, pm, ps, w2pad, b2, s0t]
    has_prev = prev is not None
    if has_prev:
        in_specs += [pl.BlockSpec((TS, vw), lambda i: (tmap(i), 0)),
                     pl.BlockSpec((TS, vw), lambda i: (tmap(i), C_GR // vw)),
                     pl.BlockSpec((1, 128), lambda i: (0, 0))]
        args += [prev, pm, norm_w.reshape(1, 128)]
    out_dtype = BF16 if has_prev else F32
    return pl.pallas_call(
        functools.partial(_gla_kernel, reverse, has_prev),
        grid=(NT,),
        in_specs=in_specs,
        out_specs=[pl.BlockSpec((TS, vw), lambda i: (tmap(i), 0)),
                   pl.BlockSpec((None, GLA_HEADS, GLA_DV, GLA_DK), lambda i: (jnp.minimum(tmap(i), NT_P - 1), 0, 0, 0))],
        out_shape=[jax.ShapeDtypeStruct((M, vw), out_dtype),
                   jax.ShapeDtypeStruct((BATCH, GLA_HEADS, GLA_DV, GLA_DK), F32)],
        scratch_shapes=[pltpu.VMEM((GLA_HEADS, GLA_DV, GLA_DK), F32),
                        pltpu.VMEM((TS, vw), F32)],
        compiler_params=_params(("arbitrary",)),
        name="gla_bwd" if reverse else "gla_fwd",
    )(*args)


MOD_SH1, MOD_SC1, MOD_GT1, MOD_SH2, MOD_SC2, MOD_GT2 = range(N_MOD)


def kernel(x_prompt, x_sample, cache_k, cache_v, state_delta, state_gla, c, c_ctx, w_mod, b_mod, norm1, w_in, pool_w, pool_scale, attn_sink, dn_conv, dn_a_log, dn_dt_bias, dn_norm, gla_w2, gla_b2, gla_norm, w_br_pool, w_br_attn, w_br_delta, w_br_gla, w_out, norm2, w_gate, w_up, w_down, norm_f):
    x = jnp.concatenate([x_prompt.reshape(MP, D), x_sample.reshape(MS, D)], axis=0)
    cond8 = jnp.concatenate([c_ctx[None, :], c, jnp.zeros((8 - 1 - DEC_BATCH, D), F32)], axis=0)
    mod = _modulation(cond8, w_mod, b_mod).reshape(DEPTH, 8, N_MOD, 1, D)
    cos_t, sin_t = _rope_tables()
    s_gla_t = jnp.swapaxes(state_gla, -1, -2)
    w_out_bf16 = w_out.astype(BF16)
    w_brs = tuple(w.astype(BF16) for w in (w_br_pool, w_br_attn, w_br_delta, w_br_gla))
    w_in_t = jnp.swapaxes(w_in, 1, 2)
    gates_t = _gate_weights(w_in_t)

    ks, vs, sds, sgs = [], [], [], []
    for l in range(DEPTH):
        h, ps = _prenorm(x, norm1, mod, w_in_t, l, MOD_SC1, MOD_SH1)
        pm = _mix_proj(h, w_in_t, l)

        ya = _pool(pm, pool_w[l], pool_scale[l])
        yb = _lat_attn(pm, attn_sink[l], cos_t, sin_t, cache_k, cache_v, l, _ctx_attn(pm, attn_sink[l]))

        alog = jnp.zeros((1, 128), F32).at[0, SM_A:SM_A + 8].set(dn_a_log[l].reshape(8))
        dtb = jnp.zeros((1, 128), F32).at[0, SM_A:SM_A + 8].set(dn_dt_bias[l].reshape(8))
        o_b, sd_b, dn_qkv = _dn_call(pm, ps, True, l, dn_conv[l], alog, dtb, state_delta)
        yc, sd_f = _dn_call(pm, ps, False, l, dn_conv[l], alog, dtb, state_delta, prev=(o_b, dn_qkv), norm_w=dn_norm[l])

        w2pad = [jnp.zeros((128, GLA_HEADS * GLA_DK), F32).at[SM_LR + o * GLA_RANK:SM_LR + (o + 1) * GLA_RANK].set(gla_w2[l, o]).astype(BF16)
                 for o in range(2)]
        g_b, sg_b = _gla_call(pm, ps, True, w2pad[1], gla_b2[l, 1].reshape(1, -1), s_gla_t[:, l, 1])
        yd, sg_f = _gla_call(pm, ps, False, w2pad[0], gla_b2[l, 0].reshape(1, -1), s_gla_t[:, l, 0], prev=g_b, norm_w=gla_norm[l])

        merged = _merge(h, gates_t, (ya, yb, yc, yd), w_brs, l)
        x, h2 = _out_proj(merged, w_out_bf16, x, mod, norm2, l, MOD_GT1, MOD_SC2, MOD_SH2)
        act = _ffn_up(h2, w_gate, w_up, l)
        x = _ffn_down(act, w_down, x, mod, l, MOD_GT2)

        ks.append(pm[:MP, C_K:C_K + KV_HEADS * HD].reshape(BATCH, SEQ, KV_HEADS, HD))
        vs.append(pm[:MP, C_V:C_V + KV_HEADS * HD].reshape(BATCH, SEQ, KV_HEADS, HD))
        sds.append(jnp.stack([sd_f, sd_b], axis=1))
        sgs.append(jnp.swapaxes(jnp.stack([sg_f, sg_b], axis=1), -1, -2))

    y = _finalnorm(x, norm_f)
    return (y[:MP].reshape(BATCH, SEQ, D), y[MP:].reshape(DEC_BATCH, DEC_SEQ, D),
            jnp.stack(ks, axis=1), jnp.stack(vs, axis=1), jnp.stack(sds, axis=1), jnp.stack(sgs, axis=1))
```

```python
import functools

import jax
import jax.numpy as jnp
from jax import lax
from jax.experimental import pallas as pl
from jax.experimental.pallas import tpu as pltpu

F32 = jnp.float32
BF16 = jnp.bfloat16

D = 2048
BATCH = 16
SEQ = 256
DEPTH = 4
DEC_BATCH = 4
DEC_SEQ = 2048
PAST = 256
GRID_W = 64
HD = 128
ROPE_THETA = 10000.0
EPS = 1e-6
N_MOD = 6
POOL_WINDOWS = (2, 4, 8, 16)
POOL_W = 512
Q_HEADS = 8
KV_HEADS = 2
GROUP = 4
ATTN_WINDOW = 128
DN_HEADS = 4
DN_DK = 128
DN_W = 512
DN_CONV = 4
GLA_HEADS = 4
GLA_DK = 64
GLA_DV = 128
GLA_RANK = 16
GLA_NORMALIZER = 16.0
FFN = 5632
CHUNK = 64
PREP_CHUNKS = 4

MP = BATCH * SEQ
MS = DEC_BATCH * DEC_SEQ
M = MP + MS
TS = 256
NT_P = MP // TS
TPS = DEC_SEQ // TS
NT_S = MS // TS
NT = NT_P + NT_S
HALO = 8

C_Q, C_POOL, C_DNQKV, C_K, C_V, C_DNZ = 0, 1024, 1536, 3072, 3328, 3584
C_GQ, C_GK, C_GV, C_GR = 4096, 4352, 4608, 5120
MIXW = 5632
MIX_TN = 512
MIX_SRC = (1, 2, 0, 4, 5, 6, 3, 7, 8, 9, 10)
MIX_ALIGNED = 8
MIX_SHIFT = 16
SMALL_COL_A = 4096
SMALL_COL_LR = 5648
GATE_COL = 5680
IN_WIDTH = GATE_COL + 4 * D
SM_A, SM_B, SM_LR = 0, 8, 16

VMEM_LIMIT = 56 * 1024 * 1024


def _params(sem):
    return pltpu.CompilerParams(dimension_semantics=sem, vmem_limit_bytes=VMEM_LIMIT)


def _cond_row(row0):
    return jnp.where(row0 < MP, 0, 1 + (row0 - MP) // DEC_SEQ)


def _tile_info(t):
    is_p = t < NT_P
    s = jnp.maximum(t - NT_P, 0)
    j = s % TPS
    first = jnp.logical_or(is_p, j == 0)
    last = jnp.logical_or(is_p, j == TPS - 1)
    base = jnp.where(is_p, 0, j * TS)
    seq_len = jnp.where(is_p, SEQ, DEC_SEQ)
    return is_p, first, last, base, seq_len


def _silu(x):
    return x * jax.nn.sigmoid(x)


def _softplus(x):
    return jnp.maximum(x, 0.0) + jnp.log(1.0 + jnp.exp(-jnp.abs(x)))


def _dot(a, b):
    return jnp.dot(a, b, preferred_element_type=F32)


def _dot_tb(a, b):
    return lax.dot_general(a, b, (((1,), (1,)), ((), ())), preferred_element_type=F32)


def _dot_ta(a, b):
    return lax.dot_general(a, b, (((0,), (0,)), ((), ())), preferred_element_type=F32)


def _split2(x):
    hi = x.astype(BF16)
    lo = (x - hi.astype(F32)).astype(BF16)
    return hi, lo


def _split3(x):
    p1 = x.astype(BF16)
    r = x - p1.astype(F32)
    p2 = r.astype(BF16)
    p3 = (r - p2.astype(F32)).astype(BF16)
    return p1, p2, p3


def _mod_kernel(c_ref, w_ref, b_ref, o_ref):
    a = _silu(c_ref[...]).astype(BF16)
    o_ref[...] = _dot(a, w_ref[...].astype(BF16)) + b_ref[...]


def _modulation(cond8, w_mod, b_mod):
    tn = 1024
    n = N_MOD * D
    return pl.pallas_call(
        _mod_kernel,
        grid=(DEPTH, n // tn),
        in_specs=[pl.BlockSpec((8, D), lambda l, j: (0, 0)),
                  pl.BlockSpec((None, D, tn), lambda l, j: (l, 0, j)),
                  pl.BlockSpec((None, 1, tn), lambda l, j: (l, 0, j))],
        out_specs=pl.BlockSpec((None, 8, tn), lambda l, j: (l, 0, j)),
        out_shape=jax.ShapeDtypeStruct((DEPTH, 8, n), F32),
        compiler_params=_params(("parallel", "parallel")),
        name="modulation",
    )(cond8, w_mod, b_mod.reshape(DEPTH, 1, n))


def _rms_rows(x, w):
    ms = jnp.mean(x * x, axis=-1, keepdims=True)
    return x * lax.rsqrt(ms + EPS) * w


def _prenorm_kernel(x_ref, nw_ref, sc_ref, sh_ref, wa_ref, wl_ref, h_ref, ps_ref, wb_scr):
    @pl.when(pl.program_id(0) == 0)
    def _():
        pad = jnp.zeros((128 - SM_LR - 2 * GLA_RANK, D), F32)
        wb_scr[...] = jnp.concatenate([wa_ref[0], wl_ref[0], pad], axis=0).astype(BF16)

    y = _rms_rows(x_ref[...], nw_ref[...])
    h = (y * (1.0 + sc_ref[...]) + sh_ref[...]).astype(h_ref.dtype)
    h_ref[...] = h
    ps_ref[...] = _dot_tb(h, wb_scr[...])


def _mod_spec(layer, chunk, tm, tn=D, row_axis=0, col_axis=None):
    def index(*g):
        col = 0 if col_axis is None else g[col_axis]
        return (layer, _cond_row(g[row_axis] * tm), chunk, 0, col)
    return pl.BlockSpec((None, None, None, 1, tn), index)


def _prenorm(x, norm_w, mod, w_in_t, layer, sc_chunk, sh_chunk):
    tm = 512
    return pl.pallas_call(
        _prenorm_kernel,
        grid=(M // tm,),
        in_specs=[pl.BlockSpec((tm, D), lambda i: (i, 0)),
                  pl.BlockSpec((None, 1, D), lambda i: (layer, 0, 0)),
                  _mod_spec(layer, sc_chunk, tm), _mod_spec(layer, sh_chunk, tm),
                  _wt_rows(layer, SM_LR, lambda i: SMALL_COL_A),
                  _wt_rows(layer, 2 * GLA_RANK, lambda i: SMALL_COL_LR)],
        out_specs=[pl.BlockSpec((tm, D), lambda i: (i, 0)), pl.BlockSpec((tm, 128), lambda i: (i, 0))],
        out_shape=[jax.ShapeDtypeStruct((M, D), BF16), jax.ShapeDtypeStruct((M, 128), F32)],
        scratch_shapes=[pltpu.VMEM((128, D), BF16)],
        compiler_params=_params(("arbitrary",)),
        name="prenorm",
    )(x, norm_w.reshape(DEPTH, 1, D), mod, mod, w_in_t, w_in_t)


def _finalnorm_kernel(x_ref, nw_ref, o_ref):
    o_ref[...] = _rms_rows(x_ref[...], nw_ref[...])


def _finalnorm(x, nw):
    tm = 512
    return pl.pallas_call(
        _finalnorm_kernel,
        grid=(M // tm,),
        in_specs=[pl.BlockSpec((tm, D), lambda i: (i, 0)),
                  pl.BlockSpec((1, D), lambda i: (0, 0))],
        out_specs=pl.BlockSpec((tm, D), lambda i: (i, 0)),
        out_shape=jax.ShapeDtypeStruct((M, D), F32),
        compiler_params=_params(("parallel",)),
        name="finalnorm",
    )(x, nw.reshape(1, D))


def _first_row_block():
    return pl.program_id(1) == 0


def _wt_rows(layer, nrows, row_fn):
    def index(*g):
        row = row_fn(*g)
        return (layer, row if isinstance(row, int) else pl.multiple_of(row, 8), 0)
    return pl.BlockSpec((pl.Element(1), pl.Element(nrows), pl.Element(D)), index)


def _mix_kernel(h_ref, w_ref, o_ref, wb_scr):
    @pl.when(_first_row_block())
    def _():
        wb_scr[...] = w_ref[0].astype(BF16)

    o_ref[...] = _dot_tb(h_ref[...], wb_scr[...])


def _mix_src_row(j):
    src = j
    for out_tile, src_tile in enumerate(MIX_SRC):
        src = jnp.where(j == out_tile, src_tile, src)
    return src * MIX_TN + jnp.where(src >= MIX_ALIGNED, MIX_SHIFT, 0)


def _mix_proj(h, w_in_t, layer):
    tm, tn = 2048, MIX_TN
    return pl.pallas_call(
        _mix_kernel,
        grid=(MIXW // tn, M // tm),
        in_specs=[pl.BlockSpec((tm, D), lambda j, i: (i, 0)),
                  _wt_rows(layer, tn, lambda j, i: _mix_src_row(j))],
        out_specs=pl.BlockSpec((tm, tn), lambda j, i: (i, j)),
        out_shape=jax.ShapeDtypeStruct((M, MIXW), F32),
        scratch_shapes=[pltpu.VMEM((tn, D), BF16)],
        compiler_params=_params(("arbitrary", "arbitrary")),
        name="mix_proj",
    )(h, w_in_t)


def _ffn_up_kernel(h_ref, wg_ref, wu_ref, o_ref, wg_scr, wu_scr):
    @pl.when(_first_row_block())
    def _():
        wg_scr[...] = wg_ref[...].astype(BF16)
        wu_scr[...] = wu_ref[...].astype(BF16)

    h = h_ref[...]
    o_ref[...] = (_silu(_dot(h, wg_scr[...])) * _dot(h, wu_scr[...])).astype(o_ref.dtype)


def _ffn_up(h, w_gate, w_up, layer):
    tm, tn = 2048, 512
    wspec = pl.BlockSpec((None, D, tn), lambda j, i: (layer, 0, j))
    return pl.pallas_call(
        _ffn_up_kernel,
        grid=(FFN // tn, M // tm),
        in_specs=[pl.BlockSpec((tm, D), lambda j, i: (i, 0)), wspec, wspec],
        out_specs=pl.BlockSpec((tm, tn), lambda j, i: (i, j)),
        out_shape=jax.ShapeDtypeStruct((M, FFN), BF16),
        scratch_shapes=[pltpu.VMEM((D, tn), BF16), pltpu.VMEM((D, tn), BF16)],
        compiler_params=_params(("arbitrary", "arbitrary")),
        name="ffn_up",
    )(h, w_gate, w_up)


def _ffn_down_kernel(a_ref, w_ref, x_ref, gt_ref, o_ref, wb_scr):
    @pl.when(_first_row_block())
    def _():
        wb_scr[...] = w_ref[...].astype(BF16)

    o_ref[...] = x_ref[...] + gt_ref[...] * _dot(a_ref[...], wb_scr[...])


def _ffn_down(act, w_down, x, mod, layer, gt_chunk):
    tm, tn = 1024, 512
    return pl.pallas_call(
        _ffn_down_kernel,
        grid=(D // tn, M // tm),
        in_specs=[pl.BlockSpec((tm, FFN), lambda j, i: (i, 0)),
                  pl.BlockSpec((None, FFN, tn), lambda j, i: (layer, 0, j), pipeline_mode=pl.Buffered(1)),
                  pl.BlockSpec((tm, tn), lambda j, i: (i, j)),
                  _mod_spec(layer, gt_chunk, tm, tn, row_axis=1, col_axis=0)],
        out_specs=pl.BlockSpec((tm, tn), lambda j, i: (i, j)),
        out_shape=jax.ShapeDtypeStruct((M, D), F32),
        scratch_shapes=[pltpu.VMEM((FFN, tn), BF16)],
        compiler_params=_params(("arbitrary", "arbitrary")),
        name="ffn_down",
    )(act, w_down, x, mod)


def _gate_cast_kernel(w_ref, o_ref):
    o_ref[...] = w_ref[0].astype(o_ref.dtype)


def _gate_weights(w_in_t):
    tr = 1024
    return pl.pallas_call(
        _gate_cast_kernel,
        grid=(DEPTH, 4 * D // tr),
        in_specs=[pl.BlockSpec((pl.Element(1), pl.Element(tr), pl.Element(D)),
                               lambda l, j: (l, pl.multiple_of(GATE_COL + j * tr, 8), 0))],
        out_specs=pl.BlockSpec((None, tr, D), lambda l, j: (l, j, 0)),
        out_shape=jax.ShapeDtypeStruct((DEPTH, 4 * D, D), BF16),
        compiler_params=_params(("parallel", "parallel")),
        name="gate_cast",
    )(w_in_t)


BR_WIDTHS = (POOL_W, Q_HEADS * HD, DN_W, GLA_HEADS * GLA_DV)


def _merge_kernel(h_ref, *refs):
    g_refs, y_refs, w_refs, o_ref = refs[0:4], refs[4:8], refs[8:12], refs[12]
    h = h_ref[...]
    acc = None
    for b in range(4):
        term = jax.nn.sigmoid(_dot_tb(h, g_refs[b][...])) * _dot(y_refs[b][...], w_refs[b][...].astype(BF16))
        acc = term if acc is None else acc + term
    o_ref[...] = acc.astype(o_ref.dtype)


def _merge(h, gates_t, ys, w_brs, layer):
    tm, tn = 1024, 512
    nj = D // tn
    gate_specs = [pl.BlockSpec((None, tn, D), functools.partial(lambda i, j, b: (layer, b * nj + j, 0), b=b))
                  for b in range(4)]
    y_specs = [pl.BlockSpec((tm, w), lambda i, j: (i, 0)) for w in BR_WIDTHS]
    w_specs = [pl.BlockSpec((None, w, tn), lambda i, j: (layer, 0, j)) for w in BR_WIDTHS]
    return pl.pallas_call(
        _merge_kernel,
        grid=(M // tm, nj),
        in_specs=[pl.BlockSpec((tm, D), lambda i, j: (i, 0))] + gate_specs + y_specs + w_specs,
        out_specs=pl.BlockSpec((tm, tn), lambda i, j: (i, j)),
        out_shape=jax.ShapeDtypeStruct((M, D), BF16),
        compiler_params=_params(("parallel", "arbitrary")),
        name="merge",
    )(h, *([gates_t] * 4), *ys, *w_brs)


def _out_proj_kernel(m_ref, w_ref, x_ref, gt_ref, nw_ref, sc_ref, sh_ref, xo_ref, h_ref):
    xn = x_ref[...] + gt_ref[...] * _dot(m_ref[...], w_ref[...])
    xo_ref[...] = xn
    h_ref[...] = (_rms_rows(xn, nw_ref[...]) * (1.0 + sc_ref[...]) + sh_ref[...]).astype(h_ref.dtype)


def _out_proj(merged, w_out_bf16, x, mod, norm_w, layer, gt_chunk, sc_chunk, sh_chunk):
    tm = 512
    row = pl.BlockSpec((tm, D), lambda i: (i, 0))
    return pl.pallas_call(
        _out_proj_kernel,
        grid=(M // tm,),
        in_specs=[row,
                  pl.BlockSpec((None, D, D), lambda i: (layer, 0, 0), pipeline_mode=pl.Buffered(1)),
                  row, _mod_spec(layer, gt_chunk, tm),
                  pl.BlockSpec((None, 1, D), lambda i: (layer, 0, 0)),
                  _mod_spec(layer, sc_chunk, tm), _mod_spec(layer, sh_chunk, tm)],
        out_specs=[row, row],
        out_shape=[jax.ShapeDtypeStruct((M, D), F32), jax.ShapeDtypeStruct((M, D), BF16)],
        compiler_params=_params(("parallel",)),
        name="out_proj",
    )(merged, w_out_bf16, x, mod, norm_w.reshape(DEPTH, 1, D), mod, mod)


def _pool_kernel(u_ref, up_ref, un_ref, w_ref, sc_ref, o_ref):
    t = pl.program_id(0)
    _, first, last, base, seq_len = _tile_info(t)
    u = u_ref[...]
    prev = jnp.where(first, 0.0, up_ref[...])
    nxt = jnp.where(last, 0.0, un_ref[...])
    ext = jnp.concatenate([prev, u, nxt], axis=0)
    pos = base + lax.broadcasted_iota(jnp.int32, (TS, 1), 0)
    outs = []
    for gi, win in enumerate(POOL_WINDOWS):
        lanes = slice(gi * 128, (gi + 1) * 128)
        p = ext[:, lanes]
        s = 1
        while s < win:
            p = p + pltpu.roll(p, s, 0)
            s *= 2
        d = HALO + win // 2 - 1
        ssum = p[d:d + TS]
        cnt = (jnp.minimum(pos + win // 2, seq_len) - jnp.maximum(pos - win // 2, 0)).astype(F32)
        pg = ssum / cnt - u[:, lanes]
        outs.append(_dot(pg.astype(BF16), w_ref[gi].astype(BF16)))
    o_ref[...] = (jnp.concatenate(outs, axis=1) * sc_ref[...]).astype(o_ref.dtype)


def _halo_specs(width, colblk):
    r = TS // HALO
    nblk = M // HALO
    prev = pl.BlockSpec((HALO, width), lambda t: (jnp.maximum(t * r - 1, 0), colblk))
    nxt = pl.BlockSpec((HALO, width), lambda t: (jnp.minimum((t + 1) * r, nblk - 1), colblk))
    return prev, nxt


def _pool(pm, pool_w, pool_scale):
    cb = C_POOL // POOL_W
    prev, nxt = _halo_specs(POOL_W, cb)
    return pl.pallas_call(
        _pool_kernel,
        grid=(NT,),
        in_specs=[pl.BlockSpec((TS, POOL_W), lambda t: (t, cb)), prev, nxt,
                  pl.BlockSpec((4, 128, 128), lambda t: (0, 0, 0)),
                  pl.BlockSpec((1, POOL_W), lambda t: (0, 0))],
        out_specs=pl.BlockSpec((TS, POOL_W), lambda t: (t, 0)),
        out_shape=jax.ShapeDtypeStruct((M, POOL_W), BF16),
        compiler_params=_params(("parallel",)),
        name="pool",
    )(pm, pm, pm, pool_w, pool_scale.reshape(1, POOL_W))


def _ctx_attn_kernel(sink_ref, q_ref, k_ref, v_ref, nk_in, nv_in, o_ref, nk_ref, nv_ref):
    del nk_in, nv_in
    q = q_ref[...] * (HD ** -0.5)
    k32 = k_ref[...]
    v32 = v_ref[...]
    k = k32.astype(BF16)
    v = v32.astype(BF16)
    for n in range(KV_HEADS):
        nk_ref[:, n, :] = k32[:, n * HD:(n + 1) * HD]
        nv_ref[:, n, :] = v32[:, n * HD:(n + 1) * HD]
        kn = k[:, n * HD:(n + 1) * HD]
        vn = v[:, n * HD:(n + 1) * HD]
        for g in range(GROUP):
            h = n * GROUP + g
            qh = q[:, h * HD:(h + 1) * HD].astype(BF16)
            s = _dot_tb(qh, kn)
            snk = sink_ref[h]
            m = jnp.maximum(jnp.max(s, axis=-1, keepdims=True), snk)
            e = jnp.exp(s - m)
            den = jnp.sum(e, axis=-1, keepdims=True) + jnp.exp(snk - m)
            o = _dot(e.astype(BF16), vn) / den
            o_ref[:, h * HD:(h + 1) * HD] = o.astype(o_ref.dtype)


def _ctx_attn(pm, sink, new_k, new_v, layer):
    qw, kw = Q_HEADS * HD, KV_HEADS * HD
    cache_blk = pl.BlockSpec((None, None, SEQ, KV_HEADS, HD), lambda t: (t, layer, 0, 0, 0))
    anyspace = pl.BlockSpec(memory_space=pl.ANY)
    return pl.pallas_call(
        _ctx_attn_kernel,
        grid=(NT_P,),
        in_specs=[pl.BlockSpec(memory_space=pltpu.SMEM),
                  pl.BlockSpec((TS, qw), lambda t: (t, C_Q // qw)),
                  pl.BlockSpec((TS, kw), lambda t: (t, C_K // kw)),
                  pl.BlockSpec((TS, kw), lambda t: (t, C_V // kw)),
                  anyspace, anyspace],
        out_specs=[pl.BlockSpec((TS, qw), lambda t: (t, 0)), cache_blk, cache_blk],
        out_shape=[jax.ShapeDtypeStruct((M, qw), BF16),
                   jax.ShapeDtypeStruct(new_k.shape, new_k.dtype), jax.ShapeDtypeStruct(new_v.shape, new_v.dtype)],
        input_output_aliases={4: 1, 5: 2},
        compiler_params=_params(("parallel",)),
        name="ctx_attn",
    )(sink, pm, pm, pm, new_k, new_v)


def _rope(x, cs, sn):
    lane = lax.broadcasted_iota(jnp.int32, x.shape, 1)
    swapped = jnp.where((lane % 64) < 32, pltpu.roll(x, 96, 1), pltpu.roll(x, 32, 1))
    return x * cs + swapped * sn


def _lat_attn_kernel(sink_ref, q_ref, km_ref, kp_ref, kn_ref, vm_ref, vp_ref, vn_ref,
                     cm_ref, cp_ref, cn_ref, sm_ref, sp_ref, sn_ref, kc_ref, vc_ref, yb_ref, o_ref):
    del yb_ref
    ts = pl.program_id(0)
    base = (ts % TPS) * TS
    half = TS // 2
    nk = TS + 2 * half
    q = q_ref[...] * (HD ** -0.5)
    cq, sq = cm_ref[...], sm_ref[...]
    kl = jnp.concatenate([kp_ref[...], km_ref[...], kn_ref[...]], axis=0)
    vl = jnp.concatenate([vp_ref[...], vm_ref[...], vn_ref[...]], axis=0).astype(BF16)
    ck = jnp.concatenate([cp_ref[...], cq, cn_ref[...]], axis=0)
    sk = jnp.concatenate([sp_ref[...], sq, sn_ref[...]], axis=0)
    qpos = base + lax.broadcasted_iota(jnp.int32, (TS, 1), 0)
    kpos = base - half + lax.broadcasted_iota(jnp.int32, (1, nk), 1)
    dist = qpos - kpos
    ok = (jnp.abs(dist) <= ATTN_WINDOW).astype(jnp.int32) * (kpos >= 0).astype(jnp.int32) * (kpos < DEC_SEQ).astype(jnp.int32)
    bias = jnp.where(ok > 0, 0.0, -1e30)
    for n in range(KV_HEADS):
        lanes = slice(n * HD, (n + 1) * HD)
        kr = _rope(kl[:, lanes], ck, sk).astype(BF16)
        vln = vl[:, lanes]
        kcn = kc_ref[:, n, :].astype(BF16)
        vcn = vc_ref[:, n, :].astype(BF16)
        for g in range(GROUP):
            h = n * GROUP + g
            qh = _rope(q[:, h * HD:(h + 1) * HD], cq, sq).astype(BF16)
            s_c = _dot_tb(qh, kcn)
            s_l = _dot_tb(qh, kr) + bias
            snk = sink_ref[h]
            m = jnp.maximum(jnp.maximum(jnp.max(s_c, axis=-1, keepdims=True),
                                        jnp.max(s_l, axis=-1, keepdims=True)), snk)
            e_c = jnp.exp(s_c - m)
            e_l = jnp.exp(s_l - m)
            den = jnp.sum(e_c, axis=-1, keepdims=True) + jnp.sum(e_l, axis=-1, keepdims=True) + jnp.exp(snk - m)
            o = (_dot(e_c.astype(BF16), vcn) + _dot(e_l.astype(BF16), vln)) / den
            o_ref[:, h * HD:(h + 1) * HD] = o.astype(o_ref.dtype)


def _lat_attn(pm, sink, cos_t, sin_t, cache_k, cache_v, layer, yb):
    qw, kw = Q_HEADS * HD, KV_HEADS * HD
    half = TS // 2
    nhb = M // half
    nhs = DEC_SEQ // half

    def main(w, cb):
        return pl.BlockSpec((TS, w), lambda ts: (NT_P + ts, cb))

    def prev(cb):
        return pl.BlockSpec((half, kw), lambda ts: (2 * (NT_P + ts) - 1, cb))

    def nxt(cb):
        return pl.BlockSpec((half, kw), lambda ts: (jnp.minimum(2 * (NT_P + ts) + 2, nhb - 1), cb))

    tab_m = pl.BlockSpec((TS, HD), lambda ts: (ts % TPS, 0))
    tab_p = pl.BlockSpec((half, HD), lambda ts: (jnp.maximum(2 * (ts % TPS) - 1, 0), 0))
    tab_n = pl.BlockSpec((half, HD), lambda ts: (jnp.minimum(2 * (ts % TPS) + 2, nhs - 1), 0))
    ctx = pl.BlockSpec((None, None, PAST, KV_HEADS, HD), lambda ts: (ts // TPS, layer, 0, 0, 0))
    return pl.pallas_call(
        _lat_attn_kernel,
        grid=(NT_S,),
        in_specs=[pl.BlockSpec(memory_space=pltpu.SMEM),
                  main(qw, C_Q // qw),
                  main(kw, C_K // kw), prev(C_K // kw), nxt(C_K // kw),
                  main(kw, C_V // kw), prev(C_V // kw), nxt(C_V // kw),
                  tab_m, tab_p, tab_n, tab_m, tab_p, tab_n, ctx, ctx,
                  pl.BlockSpec(memory_space=pl.ANY)],
        out_specs=pl.BlockSpec((TS, qw), lambda ts: (NT_P + ts, 0)),
        out_shape=jax.ShapeDtypeStruct((M, qw), BF16),
        input_output_aliases={16: 0},
        compiler_params=_params(("parallel",)),
        name="lat_attn",
    )(sink, pm, pm, pm, pm, pm, pm, pm, cos_t, cos_t, cos_t, sin_t, sin_t, sin_t,
      cache_k, cache_v, yb)


def _rope_tables():
    quarter = HD // 4
    inv = ROPE_THETA ** (-jnp.arange(quarter, dtype=F32) / quarter)
    t = jnp.arange(DEC_SEQ)
    row = (t // GRID_W).astype(F32)
    col = (t % GRID_W).astype(F32)
    ar = row[:, None] * inv[None, :]
    ac = col[:, None] * inv[None, :]
    cos_t = jnp.concatenate([jnp.cos(ar), jnp.cos(ar), jnp.cos(ac), jnp.cos(ac)], axis=-1)
    sin_t = jnp.concatenate([-jnp.sin(ar), jnp.sin(ar), -jnp.sin(ac), jnp.sin(ac)], axis=-1)
    return cos_t, sin_t


def _scan_tile(reverse):
    i = pl.program_id(0)
    return (NT - 1 - i) if reverse else i


def _tri_masks(reverse):
    ii = lax.broadcasted_iota(jnp.int32, (CHUNK, CHUNK), 0)
    jj = lax.broadcasted_iota(jnp.int32, (CHUNK, CHUNK), 1)
    incl = (jj >= ii) if reverse else (jj <= ii)
    strict = (jj > ii) if reverse else (jj < ii)
    return incl, strict


def _gated_rms_out(o, gate, nw):
    outs = []
    for h in range(4):
        lanes = slice(h * 128, (h + 1) * 128)
        outs.append(_rms_rows(o[:, lanes], nw) * _silu(gate[:, lanes]))
    return jnp.concatenate(outs, axis=1)


def _dot_hp_all(xs, ys):
    hh = [_dot(x[0], y[0]) for x, y in zip(xs, ys)]
    hl = [_dot(x[0], y[1]) for x, y in zip(xs, ys)]
    lh = [_dot(x[1], y[0]) for x, y in zip(xs, ys)]
    return [a + (b + c) for a, b, c in zip(hh, hl, lh)]


def _inv_unit_tri_all(lms):
    ii = lax.broadcasted_iota(jnp.int32, (CHUNK, CHUNK), 0)
    jj = lax.broadcasted_iota(jnp.int32, (CHUNK, CHUNK), 1)
    eye = jnp.where(ii == jj, 1.0, 0.0)
    ps = [eye - lm for lm in lms]
    mbs = [lm.astype(BF16) for lm in lms]
    mks = [_dot(m, m) for m in mbs]
    span = 4
    while span <= CHUNK:
        mbs = [m.astype(BF16) for m in mks]
        incs = [_dot(p.astype(BF16), m) for p, m in zip(ps, mbs)]
        if span < CHUNK:
            mks = [_dot(m, m) for m in mbs]
        ps = [p + inc for p, inc in zip(ps, incs)]
        span *= 2
    lts = _dot_hp_all([_split2(lm) for lm in lms], [_split2(t) for t in ps])
    rs = [(eye - t) - lt for t, lt in zip(ps, lts)]
    return [t + _dot(t.astype(BF16), r.astype(BF16)) for t, r in zip(ps, rs)]


def _dn_kernel(reverse, has_prev, *refs):
    if has_prev:
        qkv_ref, sm_ref, alog_ref, dtb_ref, s0_ref, oprev_ref, z_ref, nw_ref, o_ref, st_ref = refs[:10]
        scratch = refs[10:]
    else:
        x_ref, xp_ref, xn_ref, sm_ref, cw_ref, alog_ref, dtb_ref, s0_ref, o_ref, st_ref, qkv_out_ref = refs[:11]
        scratch = refs[11:]
    (s_scr, q_scr, k_scr, v_scr, b_scr, gcc_scr, gcr_scr, u_scr, w_scr, qe_scr, kd_scr, a_scr, o_scr) = scratch
    t = _scan_tile(reverse)
    is_p, first, last, _, _ = _tile_info(t)
    d = 1 if reverse else 0
    nchunk = TS // CHUNK
    e_last = 0 if reverse else CHUNK - 1

    @pl.when(last if reverse else first)
    def _():
        s_scr[...] = jnp.where(is_p, 0.0, s0_ref[...])

    if has_prev:
        q_scr[...] = qkv_ref[:, 0:DN_W]
        k_scr[...] = qkv_ref[:, DN_W:2 * DN_W]
        v_scr[...] = qkv_ref[:, 2 * DN_W:3 * DN_W]
    else:
        x = x_ref[...]
        ext = jnp.concatenate([jnp.where(first, 0.0, xp_ref[...]), x, jnp.where(last, 0.0, xn_ref[...])], axis=0)
        cw = cw_ref[...]
        ext_rows = TS + 2 * HALO
        conv = None
        for j in range(DN_CONV):
            off = j - DN_CONV // 2
            rows = ext if off == 0 else pltpu.roll(ext, (ext_rows - off) % ext_rows, 0)
            term = rows[HALO:HALO + TS] * cw[j:j + 1, :]
            conv = term if conv is None else conv + term
        act = _silu(conv)
        for h in range(DN_HEADS):
            lanes = slice(h * 128, (h + 1) * 128)
            qh = act[:, h * 128:(h + 1) * 128]
            kh = act[:, DN_W + h * 128:DN_W + (h + 1) * 128]
            q_scr[:, lanes] = qh * lax.rsqrt(jnp.sum(qh * qh, axis=-1, keepdims=True) + EPS) * (DN_DK ** -0.5)
            k_scr[:, lanes] = kh * lax.rsqrt(jnp.sum(kh * kh, axis=-1, keepdims=True) + EPS)
        v_scr[...] = act[:, 2 * DN_W:3 * DN_W]
        qkv_out_ref[:, 0:DN_W] = q_scr[...]
        qkv_out_ref[:, DN_W:2 * DN_W] = k_scr[...]
        qkv_out_ref[:, 2 * DN_W:3 * DN_W] = v_scr[...]
    sm = sm_ref[...]
    b_scr[...] = jax.nn.sigmoid(sm)
    g = -jnp.exp(alog_ref[...]) * _softplus(sm + dtb_ref[...])

    ti = lax.broadcasted_iota(jnp.int32, (TS, TS), 0)
    tj = lax.broadcasted_iota(jnp.int32, (TS, TS), 1)
    same = (ti // CHUNK) == (tj // CHUNK)
    lower = jnp.where(jnp.logical_and(same, tj <= ti), 1.0, 0.0).astype(BF16)
    upper = jnp.where(jnp.logical_and(same, tj >= ti), 1.0, 0.0).astype(BF16)
    tri, tri_t = (upper, lower) if reverse else (lower, upper)
    g1, g2, g3 = _split3(g)
    gcc_scr[...] = _dot(tri, g1) + _dot(tri, g2) + _dot(tri, g3)
    gcr = _dot_ta(g1, tri_t) + _dot_ta(g2, tri_t) + _dot_ta(g3, tri_t)
    for c in range(nchunk):
        gcr_scr[c] = gcr[:, c * CHUNK:(c + 1) * CHUNK]

    incl, strict = _tri_masks(reverse)

    def prep_body(pi, carry):
        rows, gblk, gcr_c, bt = {}, {}, {}, {}
        for cc in range(PREP_CHUNKS):
            c = pi * PREP_CHUNKS + cc
            rows[cc] = pl.ds(c * CHUNK, CHUNK)
            gblk[cc] = gcc_scr[rows[cc], :]
            gcr_c[cc] = gcr_scr[c]
            bt[cc] = b_scr[rows[cc], :]
        info = []
        for cc in range(PREP_CHUNKS):
            for h in range(DN_HEADS):
                lanes = slice(h * 128, (h + 1) * 128)
                r = SM_A + d * DN_HEADS + h
                gcol = gblk[cc][:, r:r + 1]
                grow = gcr_c[cc][r:r + 1, :]
                glast = gblk[cc][e_last:e_last + 1, r:r + 1]
                beta = bt[cc][:, SM_B + d * DN_HEADS + h:SM_B + d * DN_HEADS + h + 1]
                qc, kc, vc = q_scr[rows[cc], lanes], k_scr[rows[cc], lanes], v_scr[rows[cc], lanes]
                decay = jnp.exp(jnp.where(incl, gcol - grow, -1e30))
                kb = kc * beta
                eg = jnp.exp(gcol)
                qe_scr[rows[cc], lanes] = (qc * eg).astype(BF16)
                kd_scr[rows[cc], lanes] = (kc * jnp.exp(glast - gcol)).astype(BF16)
                info.append(dict(cc=cc, h=h, lanes=lanes, decay=decay, kbb=kb.astype(BF16), kcb=kc.astype(BF16),
                                 qcb=qc.astype(BF16), vbb=(vc * beta).astype(BF16), kgb=(kb * eg).astype(BF16)))
        kk = [_dot_tb(p["kbb"], p["kcb"]) for p in info]
        qk = [_dot_tb(p["qcb"], p["kcb"]) for p in info]
        for p, m in zip(info, qk):
            a_scr[rows[p["cc"]], p["h"] * CHUNK:(p["h"] + 1) * CHUNK] = (m * p["decay"]).astype(BF16)
        lms = [jnp.where(strict, m * p["decay"], 0.0) for p, m in zip(info, kk)]
        tinvs = [tv.astype(BF16) for tv in _inv_unit_tri_all(lms)]
        us = [_dot(tv, p["vbb"]) for tv, p in zip(tinvs, info)]
        ws = [_dot(tv, p["kgb"]) for tv, p in zip(tinvs, info)]
        for p, u, w in zip(info, us, ws):
            u_scr[rows[p["cc"]], p["lanes"]] = u
            w_scr[rows[p["cc"]], p["lanes"]] = w.astype(BF16)
        return carry

    for pi in range(nchunk // PREP_CHUNKS):
        prep_body(pi, 0)

    def chunk_body(ci, carry):
        c = (nchunk - 1 - ci) if reverse else ci
        rows = pl.ds(c * CHUNK, CHUNK)
        glast_row = gcc_scr[pl.ds(c * CHUNK + e_last, 1), :]
        heads = range(DN_HEADS)
        lanes = [slice(h * 128, (h + 1) * 128) for h in heads]
        s_old = [s_scr[h] for h in heads]
        sb = [s.astype(BF16) for s in s_old]
        ws = [_dot(w_scr[rows, lanes[h]], sb[h]) for h in heads]
        qs = [_dot(qe_scr[rows, lanes[h]], sb[h]) for h in heads]
        vnb = [(u_scr[rows, lanes[h]] - ws[h]).astype(BF16) for h in heads]
        av = [_dot(a_scr[rows, h * CHUNK:(h + 1) * CHUNK], vnb[h]) for h in heads]
        up = [_dot_ta(kd_scr[rows, lanes[h]], vnb[h]) for h in heads]
        for h in heads:
            r = SM_A + d * DN_HEADS + h
            o_scr[rows, lanes[h]] = qs[h] + av[h]
            s_scr[h] = jnp.exp(glast_row[:, r:r + 1]) * s_old[h] + up[h]
        return carry

    for ci in range(nchunk):
        chunk_body(ci, 0)

    if has_prev:
        o_ref[...] = _gated_rms_out(o_scr[...] + oprev_ref[...], z_ref[...], nw_ref[...]).astype(o_ref.dtype)
    else:
        o_ref[...] = o_scr[...]

    @pl.when(is_p)
    def _():
        st_ref[...] = s_scr[...]


def _seq_of_tile(t):
    return jnp.maximum(t - NT_P, 0) // TPS


def _dn_call(pm, ps, reverse, layer, conv_w, alog, dtb, state_delta, prev=None, norm_w=None):
    d = 1 if reverse else 0
    tmap = (lambda i: NT - 1 - i) if reverse else (lambda i: i)
    width = 3 * DN_W
    cb = C_DNQKV // width
    r = TS // HALO
    nblk = M // HALO
    small = [pl.BlockSpec((TS, 128), lambda i: (tmap(i), 0))]
    params = [pl.BlockSpec((1, 128), lambda i: (0, 0)),
              pl.BlockSpec((1, 128), lambda i: (0, 0)),
              pl.BlockSpec((None, None, None, DN_HEADS, DN_DK, 128), lambda i: (_seq_of_tile(tmap(i)), layer, d, 0, 0, 0))]
    tile_w = lambda w: pl.BlockSpec((TS, w), lambda i: (tmap(i), 0))
    has_prev = prev is not None
    out_specs = [tile_w(DN_W),
                 pl.BlockSpec((None, DN_HEADS, DN_DK, 128), lambda i: (jnp.minimum(tmap(i), NT_P - 1), 0, 0, 0))]
    out_shape = [jax.ShapeDtypeStruct((M, DN_W), BF16 if has_prev else F32),
                 jax.ShapeDtypeStruct((BATCH, DN_HEADS, DN_DK, 128), F32)]
    if has_prev:
        o_first, qkv = prev
        in_specs = [tile_w(width)] + small + params + [
            tile_w(DN_W), pl.BlockSpec((TS, DN_W), lambda i: (tmap(i), C_DNZ // DN_W)), pl.BlockSpec((1, 128), lambda i: (0, 0))]
        args = [qkv, ps, alog, dtb, state_delta, o_first, pm, norm_w.reshape(1, 128)]
    else:
        in_specs = [pl.BlockSpec((TS, width), lambda i: (tmap(i), cb)),
                    pl.BlockSpec((HALO, width), lambda i: (jnp.maximum(tmap(i) * r - 1, 0), cb)),
                    pl.BlockSpec((HALO, width), lambda i: (jnp.minimum((tmap(i) + 1) * r, nblk - 1), cb))] + small + [
            pl.BlockSpec((DN_CONV, width), lambda i: (0, 0))] + params
        args = [pm, pm, pm, ps, conv_w, alog, dtb, state_delta]
        out_specs.append(tile_w(width))
        out_shape.append(jax.ShapeDtypeStruct((M, width), F32))
    return pl.pallas_call(
        functools.partial(_dn_kernel, reverse, has_prev),
        grid=(NT,),
        in_specs=in_specs,
        out_specs=out_specs,
        out_shape=out_shape,
        scratch_shapes=[pltpu.VMEM((DN_HEADS, DN_DK, 128), F32),
                        pltpu.VMEM((TS, DN_W), F32), pltpu.VMEM((TS, DN_W), F32), pltpu.VMEM((TS, DN_W), F32),
                        pltpu.VMEM((TS, 128), F32), pltpu.VMEM((TS, 128), F32),
                        pltpu.VMEM((TS // CHUNK, 128, CHUNK), F32),
                        pltpu.VMEM((TS, DN_W), F32), pltpu.VMEM((TS, DN_W), BF16),
                        pltpu.VMEM((TS, DN_W), BF16), pltpu.VMEM((TS, DN_W), BF16),
                        pltpu.VMEM((TS, DN_HEADS * CHUNK), BF16),
                        pltpu.VMEM((TS, DN_W), F32)],
        compiler_params=_params(("arbitrary",)),
        name="deltanet_bwd" if reverse else "deltanet_fwd",
    )(*args)


def _gla_kernel(reverse, has_prev, q_ref, k_ref, v_ref, sm_ref, w2_ref, b2_ref, s0_ref, *rest):
    if has_prev:
        oprev_ref, r_ref, nw_ref, o_ref, st_ref, s_scr, o_scr = rest
    else:
        o_ref, st_ref, s_scr, o_scr = rest
    t = _scan_tile(reverse)
    is_p, first, last, _, _ = _tile_info(t)

    @pl.when(last if reverse else first)
    def _():
        s_scr[...] = jnp.where(is_p, 0.0, s0_ref[...])

    logits = _dot(sm_ref[...].astype(BF16), w2_ref[...]) + b2_ref[...]
    gk = -_softplus(-logits) / GLA_NORMALIZER

    ti = lax.broadcasted_iota(jnp.int32, (TS, TS), 0)
    tj = lax.broadcasted_iota(jnp.int32, (TS, TS), 1)
    same = (ti // CHUNK) == (tj // CHUNK)
    tri = jnp.where(jnp.logical_and(same, (tj >= ti) if reverse else (tj <= ti)), 1.0, 0.0).astype(BF16)
    g1, g2, g3 = _split3(gk)
    gc = _dot(tri, g1) + _dot(tri, g2) + _dot(tri, g3)

    incl, _ = _tri_masks(reverse)
    nchunk = TS // CHUNK
    e_last = 0 if reverse else CHUNK - 1
    eg = jnp.exp(gc)
    q_all = q_ref[...] * (GLA_DK ** -0.5)
    k_all = k_ref[...]
    qg_all = (q_all * eg).astype(BF16)
    kg_all = (k_all * jnp.exp(-gc)).astype(BF16)
    v_all = v_ref[...].astype(BF16)

    probs = [(c, h) for c in range(nchunk) for h in range(GLA_HEADS)]
    rws = {c: slice(c * CHUNK, (c + 1) * CHUNK) for c in range(nchunk)}
    kls = {h: slice(h * GLA_DK, (h + 1) * GLA_DK) for h in range(GLA_HEADS)}
    vls = {h: slice(h * GLA_DV, (h + 1) * GLA_DV) for h in range(GLA_HEADS)}
    glast = {c: gc[c * CHUNK + e_last:c * CHUNK + e_last + 1, :] for c in range(nchunk)}
    kd = {c: (k_all[rws[c], :] * jnp.exp(glast[c] - gc[rws[c], :])).astype(BF16) for c in range(nchunk)}
    a = {p: jnp.where(incl, _dot_tb(qg_all[rws[p[0]], kls[p[1]]], kg_all[rws[p[0]], kls[p[1]]]), 0.0).astype(BF16)
         for p in probs}
    av = {p: _dot(a[p], v_all[rws[p[0]], vls[p[1]]]) for p in probs}
    up = {p: _dot_ta(v_all[rws[p[0]], vls[p[1]]], kd[p[0]][:, kls[p[1]]]) for p in probs}
    order = list(reversed(range(nchunk))) if reverse else list(range(nchunk))
    s_in = {}
    for h in range(GLA_HEADS):
        s = s_scr[h]
        for c in order:
            s_in[(c, h)] = s.astype(BF16)
            s = jnp.exp(glast[c][:, kls[h]]) * s + up[(c, h)]
        s_scr[h] = s
    for p in probs:
        c, h = p
        o_scr[rws[c], vls[h]] = _dot_tb(qg_all[rws[c], kls[h]], s_in[p]) + av[p]

    if has_prev:
        o_ref[...] = _gated_rms_out(o_scr[...] + oprev_ref[...], r_ref[...], nw_ref[...]).astype(o_ref.dtype)
    else:
        o_ref[...] = o_scr[...]

    @pl.when(is_p)
    def _():
        st_ref[...] = s_scr[...]


def _gla_call(pm, ps, reverse, w2pad, b2, s0t, prev=None, norm_w=None):
    tmap = (lambda i: NT - 1 - i) if reverse else (lambda i: i)
    qkw = GLA_HEADS * GLA_DK
    vw = GLA_HEADS * GLA_DV
    in_specs = [
        pl.BlockSpec((TS, qkw), lambda i: (tmap(i), C_GQ // qkw)),
        pl.BlockSpec((TS, qkw), lambda i: (tmap(i), C_GK // qkw)),
        pl.BlockSpec((TS, vw), lambda i: (tmap(i), C_GV // vw)),
        pl.BlockSpec((TS, 128), lambda i: (tmap(i), 0)),
        pl.BlockSpec((128, qkw), lambda i: (0, 0)),
        pl.BlockSpec((1, qkw), lambda i: (0, 0)),
        pl.BlockSpec((None, GLA_HEADS, GLA_DV, GLA_DK), lambda i: (_seq_of_tile(tmap(i)), 0, 0, 0)),
    ]
    args = [pm, pm, pm, ps, w2pad, b2, s0t]
    has_prev = prev is not None
    if has_prev:
        in_specs += [pl.BlockSpec((TS, vw), lambda i: (tmap(i), 0)),
                     pl.BlockSpec((TS, vw), lambda i: (tmap(i), C_GR // vw)),
                     pl.BlockSpec((1, 128), lambda i: (0, 0))]
        args += [prev, pm, norm_w.reshape(1, 128)]
    out_dtype = BF16 if has_prev else F32
    return pl.pallas_call(
        functools.partial(_gla_kernel, reverse, has_prev),
        grid=(NT,),
        in_specs=in_specs,
        out_specs=[pl.BlockSpec((TS, vw), lambda i: (tmap(i), 0)),
                   pl.BlockSpec((None, GLA_HEADS, GLA_DV, GLA_DK), lambda i: (jnp.minimum(tmap(i), NT_P - 1), 0, 0, 0))],
        out_shape=[jax.ShapeDtypeStruct((M, vw), out_dtype),
                   jax.ShapeDtypeStruct((BATCH, GLA_HEADS, GLA_DV, GLA_DK), F32)],
        scratch_shapes=[pltpu.VMEM((GLA_HEADS, GLA_DV, GLA_DK), F32),
                        pltpu.VMEM((TS, vw), F32)],
        compiler_params=_params(("arbitrary",)),
        name="gla_bwd" if reverse else "gla_fwd",
    )(*args)


MOD_SH1, MOD_SC1, MOD_GT1, MOD_SH2, MOD_SC2, MOD_GT2 = range(N_MOD)


def kernel(x_prompt, x_sample, cache_k, cache_v, state_delta, state_gla, c, c_ctx, w_mod, b_mod, norm1, w_in, pool_w, pool_scale, attn_sink, dn_conv, dn_a_log, dn_dt_bias, dn_norm, gla_w2, gla_b2, gla_norm, w_br_pool, w_br_attn, w_br_delta, w_br_gla, w_out, norm2, w_gate, w_up, w_down, norm_f):
    x = jnp.concatenate([x_prompt.reshape(MP, D), x_sample.reshape(MS, D)], axis=0)
    cond8 = jnp.concatenate([c_ctx[None, :], c, jnp.zeros((8 - 1 - DEC_BATCH, D), F32)], axis=0)
    mod = _modulation(cond8, w_mod, b_mod).reshape(DEPTH, 8, N_MOD, 1, D)
    cos_t, sin_t = _rope_tables()
    s_gla_t = jnp.swapaxes(state_gla, -1, -2)
    w_out_bf16 = w_out.astype(BF16)
    w_brs = (w_br_pool, w_br_attn, w_br_delta, w_br_gla)
    w_in_t = jnp.swapaxes(w_in, 1, 2)
    gates_t = _gate_weights(w_in_t)

    new_k = jnp.zeros((BATCH, DEPTH, SEQ, KV_HEADS, HD), F32)
    new_v = jnp.zeros((BATCH, DEPTH, SEQ, KV_HEADS, HD), F32)
    sds, sgs = [], []
    for l in range(DEPTH):
        h, ps = _prenorm(x, norm1, mod, w_in_t, l, MOD_SC1, MOD_SH1)
        pm = _mix_proj(h, w_in_t, l)

        ya = _pool(pm, pool_w[l], pool_scale[l])
        yb_ctx, new_k, new_v = _ctx_attn(pm, attn_sink[l], new_k, new_v, l)
        yb = _lat_attn(pm, attn_sink[l], cos_t, sin_t, cache_k, cache_v, l, yb_ctx)

        alog = jnp.zeros((1, 128), F32).at[0, SM_A:SM_A + 8].set(dn_a_log[l].reshape(8))
        dtb = jnp.zeros((1, 128), F32).at[0, SM_A:SM_A + 8].set(dn_dt_bias[l].reshape(8))
        o_b, sd_b, dn_qkv = _dn_call(pm, ps, True, l, dn_conv[l], alog, dtb, state_delta)
        yc, sd_f = _dn_call(pm, ps, False, l, dn_conv[l], alog, dtb, state_delta, prev=(o_b, dn_qkv), norm_w=dn_norm[l])

        w2pad = [jnp.zeros((128, GLA_HEADS * GLA_DK), F32).at[SM_LR + o * GLA_RANK:SM_LR + (o + 1) * GLA_RANK].set(gla_w2[l, o]).astype(BF16)
                 for o in range(2)]
        g_b, sg_b = _gla_call(pm, ps, True, w2pad[1], gla_b2[l, 1].reshape(1, -1), s_gla_t[:, l, 1])
        yd, sg_f = _gla_call(pm, ps, False, w2pad[0], gla_b2[l, 0].reshape(1, -1), s_gla_t[:, l, 0], prev=g_b, norm_w=gla_norm[l])

        merged = _merge(h, gates_t, (ya, yb, yc, yd), w_brs, l)
        x, h2 = _out_proj(merged, w_out_bf16, x, mod, norm2, l, MOD_GT1, MOD_SC2, MOD_SH2)
        act = _ffn_up(h2, w_gate, w_up, l)
        x = _ffn_down(act, w_down, x, mod, l, MOD_GT2)

        sds.append(jnp.stack([sd_f, sd_b], axis=1))
        sgs.append(jnp.swapaxes(jnp.stack([sg_f, sg_b], axis=1), -1, -2))

    y = _finalnorm(x, norm_f)
    return (y[:MP].reshape(BATCH, SEQ, D), y[MP:].reshape(DEC_BATCH, DEC_SEQ, D),
            new_k, new_v, jnp.stack(sds, axis=1), jnp.stack(sgs, axis=1))
```

```python
import functools

import jax
import jax.numpy as jnp
from jax import lax
from jax.experimental import pallas as pl
from jax.experimental.pallas import tpu as pltpu

F32 = jnp.float32
BF16 = jnp.bfloat16

D = 2048
BATCH = 16
SEQ = 256
DEPTH = 4
DEC_BATCH = 4
DEC_SEQ = 2048
PAST = 256
GRID_W = 64
HD = 128
ROPE_THETA = 10000.0
EPS = 1e-6
N_MOD = 6
POOL_WINDOWS = (2, 4, 8, 16)
POOL_W = 512
Q_HEADS = 8
KV_HEADS = 2
GROUP = 4
ATTN_WINDOW = 128
DN_HEADS = 4
DN_DK = 128
DN_W = 512
DN_CONV = 4
GLA_HEADS = 4
GLA_DK = 64
GLA_DV = 128
GLA_RANK = 16
GLA_NORMALIZER = 16.0
FFN = 5632
CHUNK = 64
PREP_CHUNKS = 4

MP = BATCH * SEQ
MS = DEC_BATCH * DEC_SEQ
M = MP + MS
TS = 256
NT_P = MP // TS
TPS = DEC_SEQ // TS
NT_S = MS // TS
NT = NT_P + NT_S
HALO = 8
assert max(POOL_WINDOWS) // 2 <= HALO and DN_CONV // 2 <= HALO
assert ATTN_WINDOW <= TS // 2

C_Q, C_POOL, C_DNQKV, C_K, C_V, C_DNZ = 0, 1024, 1536, 3072, 3328, 3584
C_GQ, C_GK, C_GV, C_GR = 4096, 4352, 4608, 5120
MIXW = 5632
MIX_TN = 512
MIX_SRC = (1, 2, 0, 4, 5, 6, 3, 7, 8, 9, 10)
MIX_ALIGNED = 8
MIX_SHIFT = 16
SMALL_COL_A = 4096
SMALL_COL_LR = 5648
GATE_COL = 5680
IN_WIDTH = GATE_COL + 4 * D
SM_A, SM_B, SM_LR = 0, 8, 16

VMEM_LIMIT = 56 * 1024 * 1024


def _params(sem):
    return pltpu.CompilerParams(dimension_semantics=sem, vmem_limit_bytes=VMEM_LIMIT)


def _cond_row(row0):
    return jnp.where(row0 < MP, 0, 1 + (row0 - MP) // DEC_SEQ)


def _tile_info(t):
    is_p = t < NT_P
    s = jnp.maximum(t - NT_P, 0)
    j = s % TPS
    first = jnp.logical_or(is_p, j == 0)
    last = jnp.logical_or(is_p, j == TPS - 1)
    base = jnp.where(is_p, 0, j * TS)
    seq_len = jnp.where(is_p, SEQ, DEC_SEQ)
    return is_p, first, last, base, seq_len


def _silu(x):
    return x * jax.nn.sigmoid(x)


def _softplus(x):
    return jnp.maximum(x, 0.0) + jnp.log(1.0 + jnp.exp(-jnp.abs(x)))


def _dot(a, b):
    return jnp.dot(a, b, preferred_element_type=F32)


def _dot_tb(a, b):
    return lax.dot_general(a, b, (((1,), (1,)), ((), ())), preferred_element_type=F32)


def _dot_ta(a, b):
    return lax.dot_general(a, b, (((0,), (0,)), ((), ())), preferred_element_type=F32)


def _split2(x):
    hi = x.astype(BF16)
    lo = (x - hi.astype(F32)).astype(BF16)
    return hi, lo


def _split3(x):
    p1 = x.astype(BF16)
    r = x - p1.astype(F32)
    p2 = r.astype(BF16)
    p3 = (r - p2.astype(F32)).astype(BF16)
    return p1, p2, p3


def _mod_kernel(c_ref, w_ref, b_ref, o_ref):
    a = _silu(c_ref[...]).astype(BF16)
    o_ref[...] = _dot(a, w_ref[...].astype(BF16)) + b_ref[...]


def _modulation(cond8, w_mod, b_mod):
    tn = 1024
    n = N_MOD * D
    return pl.pallas_call(
        _mod_kernel,
        grid=(DEPTH, n // tn),
        in_specs=[pl.BlockSpec((8, D), lambda l, j: (0, 0)),
                  pl.BlockSpec((None, D, tn), lambda l, j: (l, 0, j)),
                  pl.BlockSpec((None, 1, tn), lambda l, j: (l, 0, j))],
        out_specs=pl.BlockSpec((None, 8, tn), lambda l, j: (l, 0, j)),
        out_shape=jax.ShapeDtypeStruct((DEPTH, 8, n), F32),
        compiler_params=_params(("parallel", "parallel")),
        name="modulation",
    )(cond8, w_mod, b_mod.reshape(DEPTH, 1, n))


def _rms_rows(x, w):
    ms = jnp.mean(x * x, axis=-1, keepdims=True)
    return x * lax.rsqrt(ms + EPS) * w


def _prenorm_kernel(x_ref, nw_ref, sc_ref, sh_ref, wa_ref, wl_ref, h_ref, ps_ref, wb_scr):
    @pl.when(pl.program_id(0) == 0)
    def _():
        pad = jnp.zeros((128 - SM_LR - 2 * GLA_RANK, D), F32)
        wb_scr[...] = jnp.concatenate([wa_ref[0], wl_ref[0], pad], axis=0).astype(BF16)

    y = _rms_rows(x_ref[...], nw_ref[...])
    h = (y * (1.0 + sc_ref[...]) + sh_ref[...]).astype(h_ref.dtype)
    h_ref[...] = h
    ps_ref[...] = _dot_tb(h, wb_scr[...])


def _mod_spec(layer, chunk, tm, tn=D, row_axis=0, col_axis=None):
    def index(*g):
        col = 0 if col_axis is None else g[col_axis]
        return (layer, _cond_row(g[row_axis] * tm), chunk, 0, col)
    return pl.BlockSpec((None, None, None, 1, tn), index)


def _prenorm(x, norm_w, mod, w_in_t, layer, sc_chunk, sh_chunk):
    tm = 512
    return pl.pallas_call(
        _prenorm_kernel,
        grid=(M // tm,),
        in_specs=[pl.BlockSpec((tm, D), lambda i: (i, 0)),
                  pl.BlockSpec((None, 1, D), lambda i: (layer, 0, 0)),
                  _mod_spec(layer, sc_chunk, tm), _mod_spec(layer, sh_chunk, tm),
                  _wt_rows(layer, SM_LR, lambda i: SMALL_COL_A),
                  _wt_rows(layer, 2 * GLA_RANK, lambda i: SMALL_COL_LR)],
        out_specs=[pl.BlockSpec((tm, D), lambda i: (i, 0)), pl.BlockSpec((tm, 128), lambda i: (i, 0))],
        out_shape=[jax.ShapeDtypeStruct((M, D), BF16), jax.ShapeDtypeStruct((M, 128), F32)],
        scratch_shapes=[pltpu.VMEM((128, D), BF16)],
        compiler_params=_params(("arbitrary",)),
        name="prenorm",
    )(x, norm_w.reshape(DEPTH, 1, D), mod, mod, w_in_t, w_in_t)


def _finalnorm_kernel(tm, x_ref, nw_ref, yp_ref, ys_ref):
    y = _rms_rows(x_ref[...], nw_ref[...])
    is_ctx = pl.program_id(0) < MP // tm

    @pl.when(is_ctx)
    def _():
        yp_ref[...] = y

    @pl.when(jnp.logical_not(is_ctx))
    def _():
        ys_ref[...] = y


def _finalnorm(x, nw):
    tm = 512
    nctx = MP // tm
    return pl.pallas_call(
        functools.partial(_finalnorm_kernel, tm),
        grid=(M // tm,),
        in_specs=[pl.BlockSpec((tm, D), lambda i: (i, 0)),
                  pl.BlockSpec((1, D), lambda i: (0, 0))],
        out_specs=[pl.BlockSpec((tm, D), lambda i: (jnp.minimum(i, nctx - 1), 0)),
                   pl.BlockSpec((tm, D), lambda i: (jnp.maximum(i - nctx, 0), 0))],
        out_shape=[jax.ShapeDtypeStruct((MP, D), F32), jax.ShapeDtypeStruct((MS, D), F32)],
        compiler_params=_params(("arbitrary",)),
        name="finalnorm",
    )(x, nw.reshape(1, D))


def _first_row_block():
    return pl.program_id(1) == 0


def _wt_rows(layer, nrows, row_fn):
    def index(*g):
        row = row_fn(*g)
        return (layer, row if isinstance(row, int) else pl.multiple_of(row, 8), 0)
    return pl.BlockSpec((pl.Element(1), pl.Element(nrows), pl.Element(D)), index)


def _mix_kernel(h_ref, w_ref, o_ref, wb_scr):
    @pl.when(_first_row_block())
    def _():
        wb_scr[...] = w_ref[0].astype(BF16)

    o_ref[...] = _dot_tb(h_ref[...], wb_scr[...])


def _mix_src_row(j):
    src = j
    for out_tile, src_tile in enumerate(MIX_SRC):
        src = jnp.where(j == out_tile, src_tile, src)
    return src * MIX_TN + jnp.where(src >= MIX_ALIGNED, MIX_SHIFT, 0)


def _mix_proj(h, w_in_t, layer):
    tm, tn = 2048, MIX_TN
    return pl.pallas_call(
        _mix_kernel,
        grid=(MIXW // tn, M // tm),
        in_specs=[pl.BlockSpec((tm, D), lambda j, i: (i, 0)),
                  _wt_rows(layer, tn, lambda j, i: _mix_src_row(j))],
        out_specs=pl.BlockSpec((tm, tn), lambda j, i: (i, j)),
        out_shape=jax.ShapeDtypeStruct((M, MIXW), F32),
        scratch_shapes=[pltpu.VMEM((tn, D), BF16)],
        compiler_params=_params(("arbitrary", "arbitrary")),
        name="mix_proj",
    )(h, w_in_t)


def _ffn_up_kernel(h_ref, wg_ref, wu_ref, o_ref, wg_scr, wu_scr):
    @pl.when(_first_row_block())
    def _():
        wg_scr[...] = wg_ref[...].astype(BF16)
        wu_scr[...] = wu_ref[...].astype(BF16)

    h = h_ref[...]
    o_ref[...] = (_silu(_dot(h, wg_scr[...])) * _dot(h, wu_scr[...])).astype(o_ref.dtype)


def _ffn_up(h, w_gate, w_up, layer):
    tm, tn = 2048, 512
    wspec = pl.BlockSpec((None, D, tn), lambda j, i: (layer, 0, j))
    return pl.pallas_call(
        _ffn_up_kernel,
        grid=(FFN // tn, M // tm),
        in_specs=[pl.BlockSpec((tm, D), lambda j, i: (i, 0)), wspec, wspec],
        out_specs=pl.BlockSpec((tm, tn), lambda j, i: (i, j)),
        out_shape=jax.ShapeDtypeStruct((M, FFN), BF16),
        scratch_shapes=[pltpu.VMEM((D, tn), BF16), pltpu.VMEM((D, tn), BF16)],
        compiler_params=_params(("arbitrary", "arbitrary")),
        name="ffn_up",
    )(h, w_gate, w_up)


def _ffn_down_kernel(a_ref, w_ref, x_ref, gt_ref, o_ref, wb_scr):
    @pl.when(_first_row_block())
    def _():
        wb_scr[...] = w_ref[...].astype(BF16)

    o_ref[...] = x_ref[...] + gt_ref[...] * _dot(a_ref[...], wb_scr[...])


def _ffn_down(act, w_down, x, mod, layer, gt_chunk):
    tm, tn = 1024, 512
    return pl.pallas_call(
        _ffn_down_kernel,
        grid=(D // tn, M // tm),
        in_specs=[pl.BlockSpec((tm, FFN), lambda j, i: (i, 0)),
                  pl.BlockSpec((None, FFN, tn), lambda j, i: (layer, 0, j), pipeline_mode=pl.Buffered(1)),
                  pl.BlockSpec((tm, tn), lambda j, i: (i, j)),
                  _mod_spec(layer, gt_chunk, tm, tn, row_axis=1, col_axis=0)],
        out_specs=pl.BlockSpec((tm, tn), lambda j, i: (i, j)),
        out_shape=jax.ShapeDtypeStruct((M, D), F32),
        scratch_shapes=[pltpu.VMEM((FFN, tn), BF16)],
        compiler_params=_params(("arbitrary", "arbitrary")),
        name="ffn_down",
    )(act, w_down, x, mod)


def _gate_cast_kernel(w_ref, o_ref):
    o_ref[...] = w_ref[0].astype(o_ref.dtype)


def _gate_weights(w_in_t):
    tr = 1024
    return pl.pallas_call(
        _gate_cast_kernel,
        grid=(DEPTH, 4 * D // tr),
        in_specs=[pl.BlockSpec((pl.Element(1), pl.Element(tr), pl.Element(D)),
                               lambda l, j: (l, pl.multiple_of(GATE_COL + j * tr, 8), 0))],
        out_specs=pl.BlockSpec((None, tr, D), lambda l, j: (l, j, 0)),
        out_shape=jax.ShapeDtypeStruct((DEPTH, 4 * D, D), BF16),
        compiler_params=_params(("parallel", "parallel")),
        name="gate_cast",
    )(w_in_t)


BR_WIDTHS = (POOL_W, Q_HEADS * HD, DN_W, GLA_HEADS * GLA_DV)


def _merge_kernel(h_ref, *refs):
    g_refs, y_refs, w_refs, o_ref = refs[0:4], refs[4:8], refs[8:12], refs[12]
    h = h_ref[...]
    acc = None
    for b in range(4):
        term = jax.nn.sigmoid(_dot_tb(h, g_refs[b][...])) * _dot(y_refs[b][...], w_refs[b][...].astype(BF16))
        acc = term if acc is None else acc + term
    o_ref[...] = acc.astype(o_ref.dtype)


def _merge(h, gates_t, ys, w_brs, layer):
    tm, tn = 1024, 512
    nj = D // tn
    gate_specs = [pl.BlockSpec((None, tn, D), functools.partial(lambda i, j, b: (layer, b * nj + j, 0), b=b))
                  for b in range(4)]
    y_specs = [pl.BlockSpec((tm, w), lambda i, j: (i, 0)) for w in BR_WIDTHS]
    w_specs = [pl.BlockSpec((None, w, tn), lambda i, j: (layer, 0, j)) for w in BR_WIDTHS]
    return pl.pallas_call(
        _merge_kernel,
        grid=(M // tm, nj),
        in_specs=[pl.BlockSpec((tm, D), lambda i, j: (i, 0))] + gate_specs + y_specs + w_specs,
        out_specs=pl.BlockSpec((tm, tn), lambda i, j: (i, j)),
        out_shape=jax.ShapeDtypeStruct((M, D), BF16),
        compiler_params=_params(("parallel", "arbitrary")),
        name="merge",
    )(h, *([gates_t] * 4), *ys, *w_brs)


def _out_proj_kernel(m_ref, w_ref, x_ref, gt_ref, nw_ref, sc_ref, sh_ref, xo_ref, h_ref):
    xn = x_ref[...] + gt_ref[...] * _dot(m_ref[...], w_ref[...])
    xo_ref[...] = xn
    h_ref[...] = (_rms_rows(xn, nw_ref[...]) * (1.0 + sc_ref[...]) + sh_ref[...]).astype(h_ref.dtype)


def _out_proj(merged, w_out_bf16, x, mod, norm_w, layer, gt_chunk, sc_chunk, sh_chunk):
    tm = 512
    row = pl.BlockSpec((tm, D), lambda i: (i, 0))
    return pl.pallas_call(
        _out_proj_kernel,
        grid=(M // tm,),
        in_specs=[row,
                  pl.BlockSpec((None, D, D), lambda i: (layer, 0, 0), pipeline_mode=pl.Buffered(1)),
                  row, _mod_spec(layer, gt_chunk, tm),
                  pl.BlockSpec((None, 1, D), lambda i: (layer, 0, 0)),
                  _mod_spec(layer, sc_chunk, tm), _mod_spec(layer, sh_chunk, tm)],
        out_specs=[row, row],
        out_shape=[jax.ShapeDtypeStruct((M, D), F32), jax.ShapeDtypeStruct((M, D), BF16)],
        compiler_params=_params(("parallel",)),
        name="out_proj",
    )(merged, w_out_bf16, x, mod, norm_w.reshape(DEPTH, 1, D), mod, mod)


def _pool_kernel(u_ref, up_ref, un_ref, w_ref, sc_ref, o_ref):
    t = pl.program_id(0)
    _, first, last, base, seq_len = _tile_info(t)
    u = u_ref[...]
    prev = jnp.where(first, 0.0, up_ref[...])
    nxt = jnp.where(last, 0.0, un_ref[...])
    ext = jnp.concatenate([prev, u, nxt], axis=0)
    pos = base + lax.broadcasted_iota(jnp.int32, (TS, 1), 0)
    outs = []
    for gi, win in enumerate(POOL_WINDOWS):
        lanes = slice(gi * 128, (gi + 1) * 128)
        p = ext[:, lanes]
        s = 1
        while s < win:
            p = p + pltpu.roll(p, s, 0)
            s *= 2
        d = HALO + win // 2 - 1
        ssum = p[d:d + TS]
        cnt = (jnp.minimum(pos + win // 2, seq_len) - jnp.maximum(pos - win // 2, 0)).astype(F32)
        pg = ssum / cnt - u[:, lanes]
        outs.append(_dot(pg.astype(BF16), w_ref[gi].astype(BF16)))
    o_ref[...] = (jnp.concatenate(outs, axis=1) * sc_ref[...]).astype(o_ref.dtype)


def _halo_specs(width, colblk):
    r = TS // HALO
    nblk = M // HALO
    prev = pl.BlockSpec((HALO, width), lambda t: (jnp.maximum(t * r - 1, 0), colblk))
    nxt = pl.BlockSpec((HALO, width), lambda t: (jnp.minimum((t + 1) * r, nblk - 1), colblk))
    return prev, nxt


def _pool(pm, pool_w, pool_scale):
    cb = C_POOL // POOL_W
    prev, nxt = _halo_specs(POOL_W, cb)
    return pl.pallas_call(
        _pool_kernel,
        grid=(NT,),
        in_specs=[pl.BlockSpec((TS, POOL_W), lambda t: (t, cb)), prev, nxt,
                  pl.BlockSpec((4, 128, 128), lambda t: (0, 0, 0)),
                  pl.BlockSpec((1, POOL_W), lambda t: (0, 0))],
        out_specs=pl.BlockSpec((TS, POOL_W), lambda t: (t, 0)),
        out_shape=jax.ShapeDtypeStruct((M, POOL_W), BF16),
        compiler_params=_params(("parallel",)),
        name="pool",
    )(pm, pm, pm, pool_w, pool_scale.reshape(1, POOL_W))


def _ctx_attn_kernel(sink_ref, q_ref, k_ref, v_ref, nk_in, nv_in, o_ref, nk_ref, nv_ref):
    del nk_in, nv_in
    q = q_ref[...] * (HD ** -0.5)
    k32 = k_ref[...]
    v32 = v_ref[...]
    k = k32.astype(BF16)
    v = v32.astype(BF16)
    for n in range(KV_HEADS):
        nk_ref[:, n, :] = k32[:, n * HD:(n + 1) * HD]
        nv_ref[:, n, :] = v32[:, n * HD:(n + 1) * HD]
        kn = k[:, n * HD:(n + 1) * HD]
        vn = v[:, n * HD:(n + 1) * HD]
        for g in range(GROUP):
            h = n * GROUP + g
            qh = q[:, h * HD:(h + 1) * HD].astype(BF16)
            s = _dot_tb(qh, kn)
            snk = sink_ref[h]
            m = jnp.maximum(jnp.max(s, axis=-1, keepdims=True), snk)
            e = jnp.exp(s - m)
            den = jnp.sum(e, axis=-1, keepdims=True) + jnp.exp(snk - m)
            o = _dot(e.astype(BF16), vn) / den
            o_ref[:, h * HD:(h + 1) * HD] = o.astype(o_ref.dtype)


def _ctx_attn(pm, sink, new_k, new_v, layer):
    qw, kw = Q_HEADS * HD, KV_HEADS * HD
    cache_blk = pl.BlockSpec((None, None, SEQ, KV_HEADS, HD), lambda t: (t, layer, 0, 0, 0))
    anyspace = pl.BlockSpec(memory_space=pl.ANY)
    return pl.pallas_call(
        _ctx_attn_kernel,
        grid=(NT_P,),
        in_specs=[pl.BlockSpec(memory_space=pltpu.SMEM),
                  pl.BlockSpec((TS, qw), lambda t: (t, C_Q // qw)),
                  pl.BlockSpec((TS, kw), lambda t: (t, C_K // kw)),
                  pl.BlockSpec((TS, kw), lambda t: (t, C_V // kw)),
                  anyspace, anyspace],
        out_specs=[pl.BlockSpec((TS, qw), lambda t: (t, 0)), cache_blk, cache_blk],
        out_shape=[jax.ShapeDtypeStruct((M, qw), BF16),
                   jax.ShapeDtypeStruct(new_k.shape, new_k.dtype), jax.ShapeDtypeStruct(new_v.shape, new_v.dtype)],
        input_output_aliases={4: 1, 5: 2},
        compiler_params=_params(("parallel",)),
        name="ctx_attn",
    )(sink, pm, pm, pm, new_k, new_v)


def _rope(x, cs, sn):
    lane = lax.broadcasted_iota(jnp.int32, x.shape, 1)
    swapped = jnp.where((lane % 64) < 32, pltpu.roll(x, 96, 1), pltpu.roll(x, 32, 1))
    return x * cs + swapped * sn


def _lat_attn_kernel(sink_ref, q_ref, km_ref, kp_ref, kn_ref, vm_ref, vp_ref, vn_ref,
                     cm_ref, cp_ref, cn_ref, sm_ref, sp_ref, sn_ref, kc_ref, vc_ref, yb_ref, o_ref):
    del yb_ref
    ts = pl.program_id(0)
    base = (ts % TPS) * TS
    half = TS // 2
    nk = TS + 2 * half
    q = q_ref[...] * (HD ** -0.5)
    cq, sq = cm_ref[...], sm_ref[...]
    kl = jnp.concatenate([kp_ref[...], km_ref[...], kn_ref[...]], axis=0)
    vl = jnp.concatenate([vp_ref[...], vm_ref[...], vn_ref[...]], axis=0).astype(BF16)
    ck = jnp.concatenate([cp_ref[...], cq, cn_ref[...]], axis=0)
    sk = jnp.concatenate([sp_ref[...], sq, sn_ref[...]], axis=0)
    qpos = base + lax.broadcasted_iota(jnp.int32, (TS, 1), 0)
    kpos = base - half + lax.broadcasted_iota(jnp.int32, (1, nk), 1)
    dist = qpos - kpos
    ok = (jnp.abs(dist) <= ATTN_WINDOW).astype(jnp.int32) * (kpos >= 0).astype(jnp.int32) * (kpos < DEC_SEQ).astype(jnp.int32)
    bias = jnp.where(ok > 0, 0.0, -1e30)
    for n in range(KV_HEADS):
        lanes = slice(n * HD, (n + 1) * HD)
        kr = _rope(kl[:, lanes], ck, sk).astype(BF16)
        vln = vl[:, lanes]
        kcn = kc_ref[:, n, :].astype(BF16)
        vcn = vc_ref[:, n, :].astype(BF16)
        for g in range(GROUP):
            h = n * GROUP + g
            qh = _rope(q[:, h * HD:(h + 1) * HD], cq, sq).astype(BF16)
            s_c = _dot_tb(qh, kcn)
            s_l = _dot_tb(qh, kr) + bias
            snk = sink_ref[h]
            m = jnp.maximum(jnp.maximum(jnp.max(s_c, axis=-1, keepdims=True),
                                        jnp.max(s_l, axis=-1, keepdims=True)), snk)
            e_c = jnp.exp(s_c - m)
            e_l = jnp.exp(s_l - m)
            den = jnp.sum(e_c, axis=-1, keepdims=True) + jnp.sum(e_l, axis=-1, keepdims=True) + jnp.exp(snk - m)
            o = (_dot(e_c.astype(BF16), vcn) + _dot(e_l.astype(BF16), vln)) / den
            o_ref[:, h * HD:(h + 1) * HD] = o.astype(o_ref.dtype)


def _lat_attn(pm, sink, cos_t, sin_t, cache_k, cache_v, layer, yb):
    qw, kw = Q_HEADS * HD, KV_HEADS * HD
    half = TS // 2
    nhb = M // half
    nhs = DEC_SEQ // half

    def main(w, cb):
        return pl.BlockSpec((TS, w), lambda ts: (NT_P + ts, cb))

    def prev(cb):
        return pl.BlockSpec((half, kw), lambda ts: (2 * (NT_P + ts) - 1, cb))

    def nxt(cb):
        return pl.BlockSpec((half, kw), lambda ts: (jnp.minimum(2 * (NT_P + ts) + 2, nhb - 1), cb))

    tab_m = pl.BlockSpec((TS, HD), lambda ts: (ts % TPS, 0))
    tab_p = pl.BlockSpec((half, HD), lambda ts: (jnp.maximum(2 * (ts % TPS) - 1, 0), 0))
    tab_n = pl.BlockSpec((half, HD), lambda ts: (jnp.minimum(2 * (ts % TPS) + 2, nhs - 1), 0))
    ctx = pl.BlockSpec((None, None, PAST, KV_HEADS, HD), lambda ts: (ts // TPS, layer, 0, 0, 0))
    return pl.pallas_call(
        _lat_attn_kernel,
        grid=(NT_S,),
        in_specs=[pl.BlockSpec(memory_space=pltpu.SMEM),
                  main(qw, C_Q // qw),
                  main(kw, C_K // kw), prev(C_K // kw), nxt(C_K // kw),
                  main(kw, C_V // kw), prev(C_V // kw), nxt(C_V // kw),
                  tab_m, tab_p, tab_n, tab_m, tab_p, tab_n, ctx, ctx,
                  pl.BlockSpec(memory_space=pl.ANY)],
        out_specs=pl.BlockSpec((TS, qw), lambda ts: (NT_P + ts, 0)),
        out_shape=jax.ShapeDtypeStruct((M, qw), BF16),
        input_output_aliases={16: 0},
        compiler_params=_params(("parallel",)),
        name="lat_attn",
    )(sink, pm, pm, pm, pm, pm, pm, pm, cos_t, cos_t, cos_t, sin_t, sin_t, sin_t,
      cache_k, cache_v, yb)


def _rope_tables():
    quarter = HD // 4
    inv = ROPE_THETA ** (-jnp.arange(quarter, dtype=F32) / quarter)
    t = jnp.arange(DEC_SEQ)
    row = (t // GRID_W).astype(F32)
    col = (t % GRID_W).astype(F32)
    ar = row[:, None] * inv[None, :]
    ac = col[:, None] * inv[None, :]
    cos_t = jnp.concatenate([jnp.cos(ar), jnp.cos(ar), jnp.cos(ac), jnp.cos(ac)], axis=-1)
    sin_t = jnp.concatenate([-jnp.sin(ar), jnp.sin(ar), -jnp.sin(ac), jnp.sin(ac)], axis=-1)
    return cos_t, sin_t


def _scan_tile(reverse):
    i = pl.program_id(0)
    return (NT - 1 - i) if reverse else i


def _tri_masks(reverse):
    ii = lax.broadcasted_iota(jnp.int32, (CHUNK, CHUNK), 0)
    jj = lax.broadcasted_iota(jnp.int32, (CHUNK, CHUNK), 1)
    incl = (jj >= ii) if reverse else (jj <= ii)
    strict = (jj > ii) if reverse else (jj < ii)
    return incl, strict


def _gated_rms_out(o, gate, nw):
    outs = []
    for h in range(4):
        lanes = slice(h * 128, (h + 1) * 128)
        outs.append(_rms_rows(o[:, lanes], nw) * _silu(gate[:, lanes]))
    return jnp.concatenate(outs, axis=1)


def _dot_hp_all(xs, ys):
    hh = [_dot(x[0], y[0]) for x, y in zip(xs, ys)]
    hl = [_dot(x[0], y[1]) for x, y in zip(xs, ys)]
    lh = [_dot(x[1], y[0]) for x, y in zip(xs, ys)]
    return [a + (b + c) for a, b, c in zip(hh, hl, lh)]


def _inv_unit_tri_all(lms):
    ii = lax.broadcasted_iota(jnp.int32, (CHUNK, CHUNK), 0)
    jj = lax.broadcasted_iota(jnp.int32, (CHUNK, CHUNK), 1)
    eye = jnp.where(ii == jj, 1.0, 0.0)
    ps = [eye - lm for lm in lms]
    mbs = [lm.astype(BF16) for lm in lms]
    mks = [_dot(m, m) for m in mbs]
    span = 4
    while span <= CHUNK:
        mbs = [m.astype(BF16) for m in mks]
        incs = [_dot(p.astype(BF16), m) for p, m in zip(ps, mbs)]
        if span < CHUNK:
            mks = [_dot(m, m) for m in mbs]
        ps = [p + inc for p, inc in zip(ps, incs)]
        span *= 2
    lts = _dot_hp_all([_split2(lm) for lm in lms], [_split2(t) for t in ps])
    rs = [(eye - t) - lt for t, lt in zip(ps, lts)]
    return [t + _dot(t.astype(BF16), r.astype(BF16)) for t, r in zip(ps, rs)]


def _dn_kernel(reverse, has_prev, *refs):
    if has_prev:
        qkv_ref, sm_ref, alog_ref, dtb_ref, s0_ref, oprev_ref, z_ref, nw_ref, o_ref, st_ref = refs[:10]
        scratch = refs[10:]
    else:
        x_ref, xp_ref, xn_ref, sm_ref, cw_ref, alog_ref, dtb_ref, s0_ref, o_ref, st_ref, qkv_out_ref = refs[:11]
        scratch = refs[11:]
    (s_scr, q_scr, k_scr, v_scr, b_scr, gcc_scr, gcr_scr, u_scr, w_scr, qe_scr, kd_scr, a_scr, o_scr) = scratch
    t = _scan_tile(reverse)
    is_p, first, last, _, _ = _tile_info(t)
    d = 1 if reverse else 0
    nchunk = TS // CHUNK
    e_last = 0 if reverse else CHUNK - 1

    @pl.when(last if reverse else first)
    def _():
        s_scr[...] = jnp.where(is_p, 0.0, s0_ref[...])

    if has_prev:
        q_scr[...] = qkv_ref[:, 0:DN_W]
        k_scr[...] = qkv_ref[:, DN_W:2 * DN_W]
        v_scr[...] = qkv_ref[:, 2 * DN_W:3 * DN_W]
    else:
        x = x_ref[...]
        ext = jnp.concatenate([jnp.where(first, 0.0, xp_ref[...]), x, jnp.where(last, 0.0, xn_ref[...])], axis=0)
        cw = cw_ref[...]
        ext_rows = TS + 2 * HALO
        conv = None
        for j in range(DN_CONV):
            off = j - DN_CONV // 2
            rows = ext if off == 0 else pltpu.roll(ext, (ext_rows - off) % ext_rows, 0)
            term = rows[HALO:HALO + TS] * cw[j:j + 1, :]
            conv = term if conv is None else conv + term
        act = _silu(conv)
        for h in range(DN_HEADS):
            lanes = slice(h * 128, (h + 1) * 128)
            qh = act[:, h * 128:(h + 1) * 128]
            kh = act[:, DN_W + h * 128:DN_W + (h + 1) * 128]
            q_scr[:, lanes] = qh * lax.rsqrt(jnp.sum(qh * qh, axis=-1, keepdims=True) + EPS) * (DN_DK ** -0.5)
            k_scr[:, lanes] = kh * lax.rsqrt(jnp.sum(kh * kh, axis=-1, keepdims=True) + EPS)
        v_scr[...] = act[:, 2 * DN_W:3 * DN_W]
        qkv_out_ref[:, 0:DN_W] = q_scr[...]
        qkv_out_ref[:, DN_W:2 * DN_W] = k_scr[...]
        qkv_out_ref[:, 2 * DN_W:3 * DN_W] = v_scr[...]
    sm = sm_ref[...]
    b_scr[...] = jax.nn.sigmoid(sm)
    g = -jnp.exp(alog_ref[...]) * _softplus(sm + dtb_ref[...])

    ti = lax.broadcasted_iota(jnp.int32, (TS, TS), 0)
    tj = lax.broadcasted_iota(jnp.int32, (TS, TS), 1)
    same = (ti // CHUNK) == (tj // CHUNK)
    lower = jnp.where(jnp.logical_and(same, tj <= ti), 1.0, 0.0).astype(BF16)
    upper = jnp.where(jnp.logical_and(same, tj >= ti), 1.0, 0.0).astype(BF16)
    tri, tri_t = (upper, lower) if reverse else (lower, upper)
    g1, g2, g3 = _split3(g)
    gcc_scr[...] = _dot(tri, g1) + _dot(tri, g2) + _dot(tri, g3)
    gcr = _dot_ta(g1, tri_t) + _dot_ta(g2, tri_t) + _dot_ta(g3, tri_t)
    for c in range(nchunk):
        gcr_scr[c] = gcr[:, c * CHUNK:(c + 1) * CHUNK]

    incl, strict = _tri_masks(reverse)

    def prep_body(pi, carry):
        rows, gblk, gcr_c, bt = {}, {}, {}, {}
        for cc in range(PREP_CHUNKS):
            c = pi * PREP_CHUNKS + cc
            rows[cc] = pl.ds(c * CHUNK, CHUNK)
            gblk[cc] = gcc_scr[rows[cc], :]
            gcr_c[cc] = gcr_scr[c]
            bt[cc] = b_scr[rows[cc], :]
        info = []
        for cc in range(PREP_CHUNKS):
            for h in range(DN_HEADS):
                lanes = slice(h * 128, (h + 1) * 128)
                r = SM_A + d * DN_HEADS + h
                gcol = gblk[cc][:, r:r + 1]
                grow = gcr_c[cc][r:r + 1, :]
                glast = gblk[cc][e_last:e_last + 1, r:r + 1]
                beta = bt[cc][:, SM_B + d * DN_HEADS + h:SM_B + d * DN_HEADS + h + 1]
                qc, kc, vc = q_scr[rows[cc], lanes], k_scr[rows[cc], lanes], v_scr[rows[cc], lanes]
                decay = jnp.exp(jnp.where(incl, gcol - grow, -1e30))
                kb = kc * beta
                eg = jnp.exp(gcol)
                qe_scr[rows[cc], lanes] = (qc * eg).astype(BF16)
                kd_scr[rows[cc], lanes] = (kc * jnp.exp(glast - gcol)).astype(BF16)
                info.append(dict(cc=cc, h=h, lanes=lanes, decay=decay, kbb=kb.astype(BF16), kcb=kc.astype(BF16),
                                 qcb=qc.astype(BF16), vbb=(vc * beta).astype(BF16), kgb=(kb * eg).astype(BF16)))
        kk = [_dot_tb(p["kbb"], p["kcb"]) for p in info]
        qk = [_dot_tb(p["qcb"], p["kcb"]) for p in info]
        for p, m in zip(info, qk):
            a_scr[rows[p["cc"]], p["h"] * CHUNK:(p["h"] + 1) * CHUNK] = (m * p["decay"]).astype(BF16)
        lms = [jnp.where(strict, m * p["decay"], 0.0) for p, m in zip(info, kk)]
        tinvs = [tv.astype(BF16) for tv in _inv_unit_tri_all(lms)]
        us = [_dot(tv, p["vbb"]) for tv, p in zip(tinvs, info)]
        ws = [_dot(tv, p["kgb"]) for tv, p in zip(tinvs, info)]
        for p, u, w in zip(info, us, ws):
            u_scr[rows[p["cc"]], p["lanes"]] = u
            w_scr[rows[p["cc"]], p["lanes"]] = w.astype(BF16)
        return carry

    for pi in range(nchunk // PREP_CHUNKS):
        prep_body(pi, 0)

    def chunk_body(ci, carry):
        c = (nchunk - 1 - ci) if reverse else ci
        rows = pl.ds(c * CHUNK, CHUNK)
        glast_row = gcc_scr[pl.ds(c * CHUNK + e_last, 1), :]
        heads = range(DN_HEADS)
        lanes = [slice(h * 128, (h + 1) * 128) for h in heads]
        s_old = [s_scr[h] for h in heads]
        sb = [s.astype(BF16) for s in s_old]
        ws = [_dot(w_scr[rows, lanes[h]], sb[h]) for h in heads]
        qs = [_dot(qe_scr[rows, lanes[h]], sb[h]) for h in heads]
        vnb = [(u_scr[rows, lanes[h]] - ws[h]).astype(BF16) for h in heads]
        av = [_dot(a_scr[rows, h * CHUNK:(h + 1) * CHUNK], vnb[h]) for h in heads]
        up = [_dot_ta(kd_scr[rows, lanes[h]], vnb[h]) for h in heads]
        for h in heads:
            r = SM_A + d * DN_HEADS + h
            o_scr[rows, lanes[h]] = qs[h] + av[h]
            s_scr[h] = jnp.exp(glast_row[:, r:r + 1]) * s_old[h] + up[h]
        return carry

    for ci in range(nchunk):
        chunk_body(ci, 0)

    if has_prev:
        o_ref[...] = _gated_rms_out(o_scr[...] + oprev_ref[...], z_ref[...], nw_ref[...]).astype(o_ref.dtype)
    else:
        o_ref[...] = o_scr[...]

    @pl.when(is_p)
    def _():
        st_ref[...] = s_scr[...]


def _seq_of_tile(t):
    return jnp.maximum(t - NT_P, 0) // TPS


def _dn_call(pm, ps, reverse, layer, conv_w, alog, dtb, state_delta, prev=None, norm_w=None):
    d = 1 if reverse else 0
    tmap = (lambda i: NT - 1 - i) if reverse else (lambda i: i)
    width = 3 * DN_W
    cb = C_DNQKV // width
    r = TS // HALO
    nblk = M // HALO
    small = [pl.BlockSpec((TS, 128), lambda i: (tmap(i), 0))]
    params = [pl.BlockSpec((1, 128), lambda i: (0, 0)),
              pl.BlockSpec((1, 128), lambda i: (0, 0)),
              pl.BlockSpec((None, None, None, DN_HEADS, DN_DK, 128), lambda i: (_seq_of_tile(tmap(i)), layer, d, 0, 0, 0))]
    tile_w = lambda w: pl.BlockSpec((TS, w), lambda i: (tmap(i), 0))
    has_prev = prev is not None
    out_specs = [tile_w(DN_W),
                 pl.BlockSpec((None, DN_HEADS, DN_DK, 128), lambda i: (jnp.minimum(tmap(i), NT_P - 1), 0, 0, 0))]
    out_shape = [jax.ShapeDtypeStruct((M, DN_W), BF16 if has_prev else F32),
                 jax.ShapeDtypeStruct((BATCH, DN_HEADS, DN_DK, 128), F32)]
    if has_prev:
        o_first, qkv = prev
        in_specs = [tile_w(width)] + small + params + [
            tile_w(DN_W), pl.BlockSpec((TS, DN_W), lambda i: (tmap(i), C_DNZ // DN_W)), pl.BlockSpec((1, 128), lambda i: (0, 0))]
        args = [qkv, ps, alog, dtb, state_delta, o_first, pm, norm_w.reshape(1, 128)]
    else:
        in_specs = [pl.BlockSpec((TS, width), lambda i: (tmap(i), cb)),
                    pl.BlockSpec((HALO, width), lambda i: (jnp.maximum(tmap(i) * r - 1, 0), cb)),
                    pl.BlockSpec((HALO, width), lambda i: (jnp.minimum((tmap(i) + 1) * r, nblk - 1), cb))] + small + [
            pl.BlockSpec((DN_CONV, width), lambda i: (0, 0))] + params
        args = [pm, pm, pm, ps, conv_w, alog, dtb, state_delta]
        out_specs.append(tile_w(width))
        out_shape.append(jax.ShapeDtypeStruct((M, width), F32))
    return pl.pallas_call(
        functools.partial(_dn_kernel, reverse, has_prev),
        grid=(NT,),
        in_specs=in_specs,
        out_specs=out_specs,
        out_shape=out_shape,
        scratch_shapes=[pltpu.VMEM((DN_HEADS, DN_DK, 128), F32),
                        pltpu.VMEM((TS, DN_W), F32), pltpu.VMEM((TS, DN_W), F32), pltpu.VMEM((TS, DN_W), F32),
                        pltpu.VMEM((TS, 128), F32), pltpu.VMEM((TS, 128), F32),
                        pltpu.VMEM((TS // CHUNK, 128, CHUNK), F32),
                        pltpu.VMEM((TS, DN_W), F32), pltpu.VMEM((TS, DN_W), BF16),
                        pltpu.VMEM((TS, DN_W), BF16), pltpu.VMEM((TS, DN_W), BF16),
                        pltpu.VMEM((TS, DN_HEADS * CHUNK), BF16),
                        pltpu.VMEM((TS, DN_W), F32)],
        compiler_params=_params(("arbitrary",)),
        name="deltanet_bwd" if reverse else "deltanet_fwd",
    )(*args)


def _gla_kernel(reverse, has_prev, q_ref, k_ref, v_ref, sm_ref, w2_ref, b2_ref, s0_ref, *rest):
    if has_prev:
        oprev_ref, r_ref, nw_ref, o_ref, st_ref, s_scr, o_scr = rest
    else:
        o_ref, st_ref, s_scr, o_scr = rest
    t = _scan_tile(reverse)
    is_p, first, last, _, _ = _tile_info(t)

    @pl.when(last if reverse else first)
    def _():
        s_scr[...] = jnp.where(is_p, 0.0, s0_ref[...])

    logits = _dot(sm_ref[...].astype(BF16), w2_ref[...]) + b2_ref[...]
    gk = -_softplus(-logits) / GLA_NORMALIZER

    ti = lax.broadcasted_iota(jnp.int32, (TS, TS), 0)
    tj = lax.broadcasted_iota(jnp.int32, (TS, TS), 1)
    same = (ti // CHUNK) == (tj // CHUNK)
    tri = jnp.where(jnp.logical_and(same, (tj >= ti) if reverse else (tj <= ti)), 1.0, 0.0).astype(BF16)
    g1, g2, g3 = _split3(gk)
    gc = _dot(tri, g1) + _dot(tri, g2) + _dot(tri, g3)

    incl, _ = _tri_masks(reverse)
    nchunk = TS // CHUNK
    e_last = 0 if reverse else CHUNK - 1
    eg = jnp.exp(gc)
    q_all = q_ref[...] * (GLA_DK ** -0.5)
    k_all = k_ref[...]
    qg_all = (q_all * eg).astype(BF16)
    kg_all = (k_all * jnp.exp(-gc)).astype(BF16)
    v_all = v_ref[...].astype(BF16)

    probs = [(c, h) for c in range(nchunk) for h in range(GLA_HEADS)]
    rws = {c: slice(c * CHUNK, (c + 1) * CHUNK) for c in range(nchunk)}
    kls = {h: slice(h * GLA_DK, (h + 1) * GLA_DK) for h in range(GLA_HEADS)}
    vls = {h: slice(h * GLA_DV, (h + 1) * GLA_DV) for h in range(GLA_HEADS)}
    glast = {c: gc[c * CHUNK + e_last:c * CHUNK + e_last + 1, :] for c in range(nchunk)}
    kd = {c: (k_all[rws[c], :] * jnp.exp(glast[c] - gc[rws[c], :])).astype(BF16) for c in range(nchunk)}
    a = {p: jnp.where(incl, _dot_tb(qg_all[rws[p[0]], kls[p[1]]], kg_all[rws[p[0]], kls[p[1]]]), 0.0).astype(BF16)
         for p in probs}
    av = {p: _dot(a[p], v_all[rws[p[0]], vls[p[1]]]) for p in probs}
    up = {p: _dot_ta(v_all[rws[p[0]], vls[p[1]]], kd[p[0]][:, kls[p[1]]]) for p in probs}
    order = list(reversed(range(nchunk))) if reverse else list(range(nchunk))
    s_in = {}
    for h in range(GLA_HEADS):
        s = s_scr[h]
        for c in order:
            s_in[(c, h)] = s.astype(BF16)
            s = jnp.exp(glast[c][:, kls[h]]) * s + up[(c, h)]
        s_scr[h] = s
    for p in probs:
        c, h = p
        o_scr[rws[c], vls[h]] = _dot_tb(qg_all[rws[c], kls[h]], s_in[p]) + av[p]

    if has_prev:
        o_ref[...] = _gated_rms_out(o_scr[...] + oprev_ref[...], r_ref[...], nw_ref[...]).astype(o_ref.dtype)
    else:
        o_ref[...] = o_scr[...]

    @pl.when(is_p)
    def _():
        st_ref[...] = s_scr[...]


def _gla_call(pm, ps, reverse, w2pad, b2, s0t, prev=None, norm_w=None):
    tmap = (lambda i: NT - 1 - i) if reverse else (lambda i: i)
    qkw = GLA_HEADS * GLA_DK
    vw = GLA_HEADS * GLA_DV
    in_specs = [
        pl.BlockSpec((TS, qkw), lambda i: (tmap(i), C_GQ // qkw)),
        pl.BlockSpec((TS, qkw), lambda i: (tmap(i), C_GK // qkw)),
        pl.BlockSpec((TS, vw), lambda i: (tmap(i), C_GV // vw)),
        pl.BlockSpec((TS, 128), lambda i: (tmap(i), 0)),
        pl.BlockSpec((128, qkw), lambda i: (0, 0)),
        pl.BlockSpec((1, qkw), lambda i: (0, 0)),
        pl.BlockSpec((None, GLA_HEADS, GLA_DV, GLA_DK), lambda i: (_seq_of_tile(tmap(i)), 0, 0, 0)),
    ]
    args = [pm, pm, pm, ps, w2pad, b2, s0t]
    has_prev = prev is not None
    if has_prev:
        in_specs += [pl.BlockSpec((TS, vw), lambda i: (tmap(i), 0)),
                     pl.BlockSpec((TS, vw), lambda i: (tmap(i), C_GR // vw)),
                     pl.BlockSpec((1, 128), lambda i: (0, 0))]
        args += [prev, pm, norm_w.reshape(1, 128)]
    out_dtype = BF16 if has_prev else F32
    return pl.pallas_call(
        functools.partial(_gla_kernel, reverse, has_prev),
        grid=(NT,),
        in_specs=in_specs,
        out_specs=[pl.BlockSpec((TS, vw), lambda i: (tmap(i), 0)),
                   pl.BlockSpec((None, GLA_HEADS, GLA_DV, GLA_DK), lambda i: (jnp.minimum(tmap(i), NT_P - 1), 0, 0, 0))],
        out_shape=[jax.ShapeDtypeStruct((M, vw), out_dtype),
                   jax.ShapeDtypeStruct((BATCH, GLA_HEADS, GLA_DV, GLA_DK), F32)],
        scratch_shapes=[pltpu.VMEM((GLA_HEADS, GLA_DV, GLA_DK), F32),
                        pltpu.VMEM((TS, vw), F32)],
        compiler_params=_params(("arbitrary",)),
        name="gla_bwd" if reverse else "gla_fwd",
    )(*args)


MOD_SH1, MOD_SC1, MOD_GT1, MOD_SH2, MOD_SC2, MOD_GT2 = range(N_MOD)


def kernel(x_prompt, x_sample, cache_k, cache_v, state_delta, state_gla, c, c_ctx, w_mod, b_mod, norm1, w_in, pool_w, pool_scale, attn_sink, dn_conv, dn_a_log, dn_dt_bias, dn_norm, gla_w2, gla_b2, gla_norm, w_br_pool, w_br_attn, w_br_delta, w_br_gla, w_out, norm2, w_gate, w_up, w_down, norm_f):
    x = jnp.concatenate([x_prompt.reshape(MP, D), x_sample.reshape(MS, D)], axis=0)
    cond8 = jnp.concatenate([c_ctx[None, :], c, jnp.zeros((8 - 1 - DEC_BATCH, D), F32)], axis=0)
    mod = _modulation(cond8, w_mod, b_mod).reshape(DEPTH, 8, N_MOD, 1, D)
    cos_t, sin_t = _rope_tables()
    s_gla_t = jnp.swapaxes(state_gla, -1, -2)
    w_out_bf16 = w_out.astype(BF16)
    w_brs = (w_br_pool, w_br_attn, w_br_delta, w_br_gla)
    w_in_t = jnp.swapaxes(w_in, 1, 2)
    gates_t = _gate_weights(w_in_t)

    new_k = jnp.zeros((BATCH, DEPTH, SEQ, KV_HEADS, HD), F32)
    new_v = jnp.zeros((BATCH, DEPTH, SEQ, KV_HEADS, HD), F32)
    sds, sgs = [], []
    for l in range(DEPTH):
        h, ps = _prenorm(x, norm1, mod, w_in_t, l, MOD_SC1, MOD_SH1)
        pm = _mix_proj(h, w_in_t, l)

        ya = _pool(pm, pool_w[l], pool_scale[l])
        yb_ctx, new_k, new_v = _ctx_attn(pm, attn_sink[l], new_k, new_v, l)
        yb = _lat_attn(pm, attn_sink[l], cos_t, sin_t, cache_k, cache_v, l, yb_ctx)

        alog = jnp.zeros((1, 128), F32).at[0, SM_A:SM_A + 8].set(dn_a_log[l].reshape(8))
        dtb = jnp.zeros((1, 128), F32).at[0, SM_A:SM_A + 8].set(dn_dt_bias[l].reshape(8))
        o_b, sd_b, dn_qkv = _dn_call(pm, ps, True, l, dn_conv[l], alog, dtb, state_delta)
        yc, sd_f = _dn_call(pm, ps, False, l, dn_conv[l], alog, dtb, state_delta, prev=(o_b, dn_qkv), norm_w=dn_norm[l])

        w2pad = [jnp.zeros((128, GLA_HEADS * GLA_DK), F32).at[SM_LR + o * GLA_RANK:SM_LR + (o + 1) * GLA_RANK].set(gla_w2[l, o]).astype(BF16)
                 for o in range(2)]
        g_b, sg_b = _gla_call(pm, ps, True, w2pad[1], gla_b2[l, 1].reshape(1, -1), s_gla_t[:, l, 1])
        yd, sg_f = _gla_call(pm, ps, False, w2pad[0], gla_b2[l, 0].reshape(1, -1), s_gla_t[:, l, 0], prev=g_b, norm_w=gla_norm[l])

        merged = _merge(h, gates_t, (ya, yb, yc, yd), w_brs, l)
        x, h2 = _out_proj(merged, w_out_bf16, x, mod, norm2, l, MOD_GT1, MOD_SC2, MOD_SH2)
        act = _ffn_up(h2, w_gate, w_up, l)
        x = _ffn_down(act, w_down, x, mod, l, MOD_GT2)

        sds.append(jnp.stack([sd_f, sd_b], axis=1))
        sgs.append(jnp.swapaxes(jnp.stack([sg_f, sg_b], axis=1), -1, -2))

    y_ctx, y_lat = _finalnorm(x, norm_f)
    return (y_ctx.reshape(BATCH, SEQ, D), y_lat.reshape(DEC_BATCH, DEC_SEQ, D),
            new_k, new_v, jnp.stack(sds, axis=1), jnp.stack(sgs, axis=1))
```

```python
import functools

import jax
import jax.numpy as jnp
from jax import lax
from jax.experimental import pallas as pl
from jax.experimental.pallas import tpu as pltpu

F32 = jnp.float32
BF16 = jnp.bfloat16

D = 2048
BATCH = 16
SEQ = 256
DEPTH = 4
DEC_BATCH = 4
DEC_SEQ = 2048
PAST = 256
GRID_W = 64
HD = 128
ROPE_THETA = 10000.0
EPS = 1e-6
N_MOD = 6
POOL_WINDOWS = (2, 4, 8, 16)
POOL_W = 512
Q_HEADS = 8
KV_HEADS = 2
GROUP = 4
ATTN_WINDOW = 128
DN_HEADS = 4
DN_DK = 128
DN_W = 512
DN_CONV = 4
GLA_HEADS = 4
GLA_DK = 64
GLA_DV = 128
GLA_RANK = 16
GLA_NORMALIZER = 16.0
FFN = 5632
CHUNK = 64
PREP_CHUNKS = 4

MP = BATCH * SEQ
MS = DEC_BATCH * DEC_SEQ
M = MP + MS
TS = 256
NT_P = MP // TS
TPS = DEC_SEQ // TS
NT_S = MS // TS
NT = NT_P + NT_S
HALO = 8
assert max(POOL_WINDOWS) // 2 <= HALO and DN_CONV // 2 <= HALO
assert ATTN_WINDOW <= TS // 2

C_Q, C_POOL, C_DNQKV, C_K, C_V, C_DNZ = 0, 1024, 1536, 3072, 3328, 3584
C_GQ, C_GK, C_GV, C_GR = 4096, 4352, 4608, 5120
MIXW = 5632
MIX_TN = 512
MIX_SRC = (1, 2, 0, 4, 5, 6, 3, 7, 8, 9, 10)
MIX_ALIGNED = 8
MIX_SHIFT = 16
SMALL_COL_A = 4096
SMALL_COL_LR = 5648
GATE_COL = 5680
IN_WIDTH = GATE_COL + 4 * D
SM_A, SM_B, SM_LR = 0, 8, 16

VMEM_LIMIT = 56 * 1024 * 1024


def _params(sem):
    return pltpu.CompilerParams(dimension_semantics=sem, vmem_limit_bytes=VMEM_LIMIT)


def _cond_row(row0):
    return jnp.where(row0 < MP, 0, 1 + (row0 - MP) // DEC_SEQ)


def _tile_info(t):
    is_p = t < NT_P
    s = jnp.maximum(t - NT_P, 0)
    j = s % TPS
    first = jnp.logical_or(is_p, j == 0)
    last = jnp.logical_or(is_p, j == TPS - 1)
    base = jnp.where(is_p, 0, j * TS)
    seq_len = jnp.where(is_p, SEQ, DEC_SEQ)
    return is_p, first, last, base, seq_len


def _silu(x):
    return x * jax.nn.sigmoid(x)


def _softplus(x):
    return jnp.maximum(x, 0.0) + jnp.log(1.0 + jnp.exp(-jnp.abs(x)))


def _dot(a, b):
    return jnp.dot(a, b, preferred_element_type=F32)


def _dot_tb(a, b):
    return lax.dot_general(a, b, (((1,), (1,)), ((), ())), preferred_element_type=F32)


def _dot_ta(a, b):
    return lax.dot_general(a, b, (((0,), (0,)), ((), ())), preferred_element_type=F32)


def _split2(x):
    hi = x.astype(BF16)
    lo = (x - hi.astype(F32)).astype(BF16)
    return hi, lo


def _split3(x):
    p1 = x.astype(BF16)
    r = x - p1.astype(F32)
    p2 = r.astype(BF16)
    p3 = (r - p2.astype(F32)).astype(BF16)
    return p1, p2, p3


def _mod_kernel(c_ref, w_ref, b_ref, o_ref):
    a = _silu(c_ref[...]).astype(BF16)
    o_ref[...] = _dot(a, w_ref[...].astype(BF16)) + b_ref[...]


def _modulation(cond8, w_mod, b_mod):
    tn = 1024
    n = N_MOD * D
    return pl.pallas_call(
        _mod_kernel,
        grid=(DEPTH, n // tn),
        in_specs=[pl.BlockSpec((8, D), lambda l, j: (0, 0)),
                  pl.BlockSpec((None, D, tn), lambda l, j: (l, 0, j)),
                  pl.BlockSpec((None, 1, tn), lambda l, j: (l, 0, j))],
        out_specs=pl.BlockSpec((None, 8, tn), lambda l, j: (l, 0, j)),
        out_shape=jax.ShapeDtypeStruct((DEPTH, 8, n), F32),
        compiler_params=_params(("parallel", "parallel")),
        name="modulation",
    )(cond8, w_mod, b_mod.reshape(DEPTH, 1, n))


def _rms_rows(x, w):
    ms = jnp.mean(x * x, axis=-1, keepdims=True)
    return x * lax.rsqrt(ms + EPS) * w


def _prenorm_kernel(x_ref, nw_ref, sc_ref, sh_ref, wa_ref, wl_ref, h_ref, ps_ref, wb_scr):
    @pl.when(pl.program_id(0) == 0)
    def _():
        pad = jnp.zeros((128 - SM_LR - 2 * GLA_RANK, D), F32)
        wb_scr[...] = jnp.concatenate([wa_ref[0], wl_ref[0], pad], axis=0).astype(BF16)

    y = _rms_rows(x_ref[...], nw_ref[...])
    h = (y * (1.0 + sc_ref[...]) + sh_ref[...]).astype(h_ref.dtype)
    h_ref[...] = h
    ps_ref[...] = _dot_tb(h, wb_scr[...])


def _mod_spec(layer, chunk, tm, tn=D, row_axis=0, col_axis=None):
    def index(*g):
        col = 0 if col_axis is None else g[col_axis]
        return (layer, _cond_row(g[row_axis] * tm), chunk, 0, col)
    return pl.BlockSpec((None, None, None, 1, tn), index)


def _prenorm(x, norm_w, mod, w_in_t, layer, sc_chunk, sh_chunk):
    tm = 512
    return pl.pallas_call(
        _prenorm_kernel,
        grid=(M // tm,),
        in_specs=[pl.BlockSpec((tm, D), lambda i: (i, 0)),
                  pl.BlockSpec((None, 1, D), lambda i: (layer, 0, 0)),
                  _mod_spec(layer, sc_chunk, tm), _mod_spec(layer, sh_chunk, tm),
                  _wt_rows(layer, SM_LR, lambda i: SMALL_COL_A),
                  _wt_rows(layer, 2 * GLA_RANK, lambda i: SMALL_COL_LR)],
        out_specs=[pl.BlockSpec((tm, D), lambda i: (i, 0)), pl.BlockSpec((tm, 128), lambda i: (i, 0))],
        out_shape=[jax.ShapeDtypeStruct((M, D), BF16), jax.ShapeDtypeStruct((M, 128), F32)],
        scratch_shapes=[pltpu.VMEM((128, D), BF16)],
        compiler_params=_params(("arbitrary",)),
        name="prenorm",
    )(x, norm_w.reshape(DEPTH, 1, D), mod, mod, w_in_t, w_in_t)


def _finalnorm_kernel(tm, x_ref, nw_ref, yp_ref, ys_ref):
    y = _rms_rows(x_ref[...], nw_ref[...])
    is_ctx = pl.program_id(0) < MP // tm

    @pl.when(is_ctx)
    def _():
        yp_ref[...] = y

    @pl.when(jnp.logical_not(is_ctx))
    def _():
        ys_ref[...] = y


def _finalnorm(x, nw):
    tm = 512
    nctx = MP // tm
    return pl.pallas_call(
        functools.partial(_finalnorm_kernel, tm),
        grid=(M // tm,),
        in_specs=[pl.BlockSpec((tm, D), lambda i: (i, 0)),
                  pl.BlockSpec((1, D), lambda i: (0, 0))],
        out_specs=[pl.BlockSpec((tm, D), lambda i: (jnp.minimum(i, nctx - 1), 0)),
                   pl.BlockSpec((tm, D), lambda i: (jnp.maximum(i - nctx, 0), 0))],
        out_shape=[jax.ShapeDtypeStruct((MP, D), F32), jax.ShapeDtypeStruct((MS, D), F32)],
        compiler_params=_params(("arbitrary",)),
        name="finalnorm",
    )(x, nw.reshape(1, D))


def _first_row_block():
    return pl.program_id(1) == 0


def _wt_rows(layer, nrows, row_fn):
    def index(*g):
        row = row_fn(*g)
        return (layer, row if isinstance(row, int) else pl.multiple_of(row, 8), 0)
    return pl.BlockSpec((pl.Element(1), pl.Element(nrows), pl.Element(D)), index)


def _mix_kernel(h_ref, w_ref, o_ref, wb_scr):
    @pl.when(_first_row_block())
    def _():
        wb_scr[...] = w_ref[0].astype(BF16)

    o_ref[...] = _dot_tb(h_ref[...], wb_scr[...])


def _mix_src_row(j):
    src = j
    for out_tile, src_tile in enumerate(MIX_SRC):
        src = jnp.where(j == out_tile, src_tile, src)
    return src * MIX_TN + jnp.where(src >= MIX_ALIGNED, MIX_SHIFT, 0)


def _mix_proj(h, w_in_t, layer):
    tm, tn = 2048, MIX_TN
    return pl.pallas_call(
        _mix_kernel,
        grid=(MIXW // tn, M // tm),
        in_specs=[pl.BlockSpec((tm, D), lambda j, i: (i, 0)),
                  _wt_rows(layer, tn, lambda j, i: _mix_src_row(j))],
        out_specs=pl.BlockSpec((tm, tn), lambda j, i: (i, j)),
        out_shape=jax.ShapeDtypeStruct((M, MIXW), F32),
        scratch_shapes=[pltpu.VMEM((tn, D), BF16)],
        compiler_params=_params(("arbitrary", "arbitrary")),
        name="mix_proj",
    )(h, w_in_t)


def _ffn_up_kernel(h_ref, wg_ref, wu_ref, o_ref, wg_scr, wu_scr):
    @pl.when(_first_row_block())
    def _():
        wg_scr[...] = wg_ref[...].astype(BF16)
        wu_scr[...] = wu_ref[...].astype(BF16)

    h = h_ref[...]
    o_ref[...] = (_silu(_dot(h, wg_scr[...])) * _dot(h, wu_scr[...])).astype(o_ref.dtype)


def _ffn_up(h, w_gate, w_up, layer):
    tm, tn = 2048, 512
    wspec = pl.BlockSpec((None, D, tn), lambda j, i: (layer, 0, j))
    return pl.pallas_call(
        _ffn_up_kernel,
        grid=(FFN // tn, M // tm),
        in_specs=[pl.BlockSpec((tm, D), lambda j, i: (i, 0)), wspec, wspec],
        out_specs=pl.BlockSpec((tm, tn), lambda j, i: (i, j)),
        out_shape=jax.ShapeDtypeStruct((M, FFN), BF16),
        scratch_shapes=[pltpu.VMEM((D, tn), BF16), pltpu.VMEM((D, tn), BF16)],
        compiler_params=_params(("arbitrary", "arbitrary")),
        name="ffn_up",
    )(h, w_gate, w_up)


def _ffn_down_kernel(a_ref, w_ref, x_ref, gt_ref, o_ref, wb_scr):
    @pl.when(_first_row_block())
    def _():
        wb_scr[...] = w_ref[...].astype(BF16)

    o_ref[...] = x_ref[...] + gt_ref[...] * _dot(a_ref[...], wb_scr[...])


def _ffn_down(act, w_down, x, mod, layer, gt_chunk):
    tm, tn = 1024, 512
    return pl.pallas_call(
        _ffn_down_kernel,
        grid=(D // tn, M // tm),
        in_specs=[pl.BlockSpec((tm, FFN), lambda j, i: (i, 0)),
                  pl.BlockSpec((None, FFN, tn), lambda j, i: (layer, 0, j), pipeline_mode=pl.Buffered(1)),
                  pl.BlockSpec((tm, tn), lambda j, i: (i, j)),
                  _mod_spec(layer, gt_chunk, tm, tn, row_axis=1, col_axis=0)],
        out_specs=pl.BlockSpec((tm, tn), lambda j, i: (i, j)),
        out_shape=jax.ShapeDtypeStruct((M, D), F32),
        scratch_shapes=[pltpu.VMEM((FFN, tn), BF16)],
        compiler_params=_params(("arbitrary", "arbitrary")),
        name="ffn_down",
    )(act, w_down, x, mod)


def _gate_cast_kernel(w_ref, o_ref):
    o_ref[...] = w_ref[0].astype(o_ref.dtype)


def _gate_weights(w_in_t):
    tr = 1024
    return pl.pallas_call(
        _gate_cast_kernel,
        grid=(DEPTH, 4 * D // tr),
        in_specs=[pl.BlockSpec((pl.Element(1), pl.Element(tr), pl.Element(D)),
                               lambda l, j: (l, pl.multiple_of(GATE_COL + j * tr, 8), 0))],
        out_specs=pl.BlockSpec((None, tr, D), lambda l, j: (l, j, 0)),
        out_shape=jax.ShapeDtypeStruct((DEPTH, 4 * D, D), BF16),
        compiler_params=_params(("parallel", "parallel")),
        name="gate_cast",
    )(w_in_t)


BR_WIDTHS = (POOL_W, Q_HEADS * HD, DN_W, GLA_HEADS * GLA_DV)


def _merge_kernel(h_ref, *refs):
    g_refs, y_refs, w_refs, o_ref = refs[0:4], refs[4:8], refs[8:12], refs[12]
    h = h_ref[...]
    acc = None
    for b in range(4):
        term = jax.nn.sigmoid(_dot_tb(h, g_refs[b][...])) * _dot(y_refs[b][...], w_refs[b][...].astype(BF16))
        acc = term if acc is None else acc + term
    o_ref[...] = acc.astype(o_ref.dtype)


def _merge(h, gates_t, ys, w_brs, layer):
    tm, tn = 1024, 512
    nj = D // tn
    gate_specs = [pl.BlockSpec((None, tn, D), functools.partial(lambda i, j, b: (layer, b * nj + j, 0), b=b))
                  for b in range(4)]
    y_specs = [pl.BlockSpec((tm, w), lambda i, j: (i, 0)) for w in BR_WIDTHS]
    w_specs = [pl.BlockSpec((None, w, tn), lambda i, j: (layer, 0, j)) for w in BR_WIDTHS]
    return pl.pallas_call(
        _merge_kernel,
        grid=(M // tm, nj),
        in_specs=[pl.BlockSpec((tm, D), lambda i, j: (i, 0))] + gate_specs + y_specs + w_specs,
        out_specs=pl.BlockSpec((tm, tn), lambda i, j: (i, j)),
        out_shape=jax.ShapeDtypeStruct((M, D), BF16),
        compiler_params=_params(("parallel", "arbitrary")),
        name="merge",
    )(h, *([gates_t] * 4), *ys, *w_brs)


def _out_proj_kernel(m_ref, w_ref, x_ref, gt_ref, nw_ref, sc_ref, sh_ref, xo_ref, h_ref):
    xn = x_ref[...] + gt_ref[...] * _dot(m_ref[...], w_ref[...])
    xo_ref[...] = xn
    h_ref[...] = (_rms_rows(xn, nw_ref[...]) * (1.0 + sc_ref[...]) + sh_ref[...]).astype(h_ref.dtype)


def _out_proj(merged, w_out_bf16, x, mod, norm_w, layer, gt_chunk, sc_chunk, sh_chunk):
    tm = 512
    row = pl.BlockSpec((tm, D), lambda i: (i, 0))
    return pl.pallas_call(
        _out_proj_kernel,
        grid=(M // tm,),
        in_specs=[row,
                  pl.BlockSpec((None, D, D), lambda i: (layer, 0, 0), pipeline_mode=pl.Buffered(1)),
                  row, _mod_spec(layer, gt_chunk, tm),
                  pl.BlockSpec((None, 1, D), lambda i: (layer, 0, 0)),
                  _mod_spec(layer, sc_chunk, tm), _mod_spec(layer, sh_chunk, tm)],
        out_specs=[row, row],
        out_shape=[jax.ShapeDtypeStruct((M, D), F32), jax.ShapeDtypeStruct((M, D), BF16)],
        compiler_params=_params(("parallel",)),
        name="out_proj",
    )(merged, w_out_bf16, x, mod, norm_w.reshape(DEPTH, 1, D), mod, mod)


def _pool_kernel(u_ref, up_ref, un_ref, w_ref, sc_ref, o_ref):
    t = pl.program_id(0)
    _, first, last, base, seq_len = _tile_info(t)
    u = u_ref[...]
    prev = jnp.where(first, 0.0, up_ref[...])
    nxt = jnp.where(last, 0.0, un_ref[...])
    ext = jnp.concatenate([prev, u, nxt], axis=0)
    pos = base + lax.broadcasted_iota(jnp.int32, (TS, 1), 0)
    outs = []
    for gi, win in enumerate(POOL_WINDOWS):
        lanes = slice(gi * 128, (gi + 1) * 128)
        p = ext[:, lanes]
        s = 1
        while s < win:
            p = p + pltpu.roll(p, s, 0)
            s *= 2
        d = HALO + win // 2 - 1
        ssum = p[d:d + TS]
        cnt = (jnp.minimum(pos + win // 2, seq_len) - jnp.maximum(pos - win // 2, 0)).astype(F32)
        pg = ssum / cnt - u[:, lanes]
        outs.append(_dot(pg.astype(BF16), w_ref[gi].astype(BF16)))
    o_ref[...] = (jnp.concatenate(outs, axis=1) * sc_ref[...]).astype(o_ref.dtype)


def _halo_specs(width, colblk):
    r = TS // HALO
    nblk = M // HALO
    prev = pl.BlockSpec((HALO, width), lambda t: (jnp.maximum(t * r - 1, 0), colblk))
    nxt = pl.BlockSpec((HALO, width), lambda t: (jnp.minimum((t + 1) * r, nblk - 1), colblk))
    return prev, nxt


def _pool(pm, pool_w, pool_scale):
    cb = C_POOL // POOL_W
    prev, nxt = _halo_specs(POOL_W, cb)
    return pl.pallas_call(
        _pool_kernel,
        grid=(NT,),
        in_specs=[pl.BlockSpec((TS, POOL_W), lambda t: (t, cb)), prev, nxt,
                  pl.BlockSpec((4, 128, 128), lambda t: (0, 0, 0)),
                  pl.BlockSpec((1, POOL_W), lambda t: (0, 0))],
        out_specs=pl.BlockSpec((TS, POOL_W), lambda t: (t, 0)),
        out_shape=jax.ShapeDtypeStruct((M, POOL_W), BF16),
        compiler_params=_params(("parallel",)),
        name="pool",
    )(pm, pm, pm, pool_w, pool_scale.reshape(1, POOL_W))


def _ctx_attn_kernel(sink_ref, q_ref, k_ref, v_ref, nk_in, nv_in, yb_in, o_ref, nk_ref, nv_ref):
    del nk_in, nv_in, yb_in
    q = q_ref[...] * (HD ** -0.5)
    k32 = k_ref[...]
    v32 = v_ref[...]
    k = k32.astype(BF16)
    v = v32.astype(BF16)
    for n in range(KV_HEADS):
        nk_ref[:, n, :] = k32[:, n * HD:(n + 1) * HD]
        nv_ref[:, n, :] = v32[:, n * HD:(n + 1) * HD]
        kn = k[:, n * HD:(n + 1) * HD]
        vn = v[:, n * HD:(n + 1) * HD]
        for g in range(GROUP):
            h = n * GROUP + g
            qh = q[:, h * HD:(h + 1) * HD].astype(BF16)
            s = _dot_tb(qh, kn)
            snk = sink_ref[h]
            m = jnp.maximum(jnp.max(s, axis=-1, keepdims=True), snk)
            e = jnp.exp(s - m)
            den = jnp.sum(e, axis=-1, keepdims=True) + jnp.exp(snk - m)
            o = _dot(e.astype(BF16), vn) / den
            o_ref[:, h * HD:(h + 1) * HD] = o.astype(o_ref.dtype)


def _ctx_attn(pm, sink, new_k, new_v, layer, yb_buf):
    qw, kw = Q_HEADS * HD, KV_HEADS * HD
    cache_blk = pl.BlockSpec((None, None, SEQ, KV_HEADS, HD), lambda t: (t, layer, 0, 0, 0))
    anyspace = pl.BlockSpec(memory_space=pl.ANY)
    return pl.pallas_call(
        _ctx_attn_kernel,
        grid=(NT_P,),
        in_specs=[pl.BlockSpec(memory_space=pltpu.SMEM),
                  pl.BlockSpec((TS, qw), lambda t: (t, C_Q // qw)),
                  pl.BlockSpec((TS, kw), lambda t: (t, C_K // kw)),
                  pl.BlockSpec((TS, kw), lambda t: (t, C_V // kw)),
                  anyspace, anyspace, anyspace],
        out_specs=[pl.BlockSpec((TS, qw), lambda t: (t, 0)), cache_blk, cache_blk],
        out_shape=[jax.ShapeDtypeStruct(yb_buf.shape, yb_buf.dtype),
                   jax.ShapeDtypeStruct(new_k.shape, new_k.dtype), jax.ShapeDtypeStruct(new_v.shape, new_v.dtype)],
        input_output_aliases={4: 1, 5: 2, 6: 0},
        compiler_params=_params(("parallel",)),
        name="ctx_attn",
    )(sink, pm, pm, pm, new_k, new_v, yb_buf)


def _rope(x, cs, sn):
    lane = lax.broadcasted_iota(jnp.int32, x.shape, 1)
    swapped = jnp.where((lane % 64) < 32, pltpu.roll(x, 96, 1), pltpu.roll(x, 32, 1))
    return x * cs + swapped * sn


def _lat_attn_kernel(sink_ref, q_ref, km_ref, kp_ref, kn_ref, vm_ref, vp_ref, vn_ref,
                     cm_ref, cp_ref, cn_ref, sm_ref, sp_ref, sn_ref, kc_ref, vc_ref, yb_ref, o_ref):
    del yb_ref
    ts = pl.program_id(0)
    base = (ts % TPS) * TS
    half = TS // 2
    nk = TS + 2 * half
    q = q_ref[...] * (HD ** -0.5)
    cq, sq = cm_ref[...], sm_ref[...]
    kl = jnp.concatenate([kp_ref[...], km_ref[...], kn_ref[...]], axis=0)
    vl = jnp.concatenate([vp_ref[...], vm_ref[...], vn_ref[...]], axis=0).astype(BF16)
    ck = jnp.concatenate([cp_ref[...], cq, cn_ref[...]], axis=0)
    sk = jnp.concatenate([sp_ref[...], sq, sn_ref[...]], axis=0)
    qpos = base + lax.broadcasted_iota(jnp.int32, (TS, 1), 0)
    kpos = base - half + lax.broadcasted_iota(jnp.int32, (1, nk), 1)
    dist = qpos - kpos
    ok = (jnp.abs(dist) <= ATTN_WINDOW).astype(jnp.int32) * (kpos >= 0).astype(jnp.int32) * (kpos < DEC_SEQ).astype(jnp.int32)
    bias = jnp.where(ok > 0, 0.0, -1e30)
    for n in range(KV_HEADS):
        lanes = slice(n * HD, (n + 1) * HD)
        kr = _rope(kl[:, lanes], ck, sk).astype(BF16)
        vln = vl[:, lanes]
        kcn = kc_ref[:, n, :].astype(BF16)
        vcn = vc_ref[:, n, :].astype(BF16)
        for g in range(GROUP):
            h = n * GROUP + g
            qh = _rope(q[:, h * HD:(h + 1) * HD], cq, sq).astype(BF16)
            s_c = _dot_tb(qh, kcn)
            s_l = _dot_tb(qh, kr) + bias
            snk = sink_ref[h]
            m = jnp.maximum(jnp.maximum(jnp.max(s_c, axis=-1, keepdims=True),
                                        jnp.max(s_l, axis=-1, keepdims=True)), snk)
            e_c = jnp.exp(s_c - m)
            e_l = jnp.exp(s_l - m)
            den = jnp.sum(e_c, axis=-1, keepdims=True) + jnp.sum(e_l, axis=-1, keepdims=True) + jnp.exp(snk - m)
            o = (_dot(e_c.astype(BF16), vcn) + _dot(e_l.astype(BF16), vln)) / den
            o_ref[:, h * HD:(h + 1) * HD] = o.astype(o_ref.dtype)


def _lat_attn(pm, sink, cos_t, sin_t, cache_k, cache_v, layer, yb):
    qw, kw = Q_HEADS * HD, KV_HEADS * HD
    half = TS // 2
    nhb = M // half
    nhs = DEC_SEQ // half

    def main(w, cb):
        return pl.BlockSpec((TS, w), lambda ts: (NT_P + ts, cb))

    def prev(cb):
        return pl.BlockSpec((half, kw), lambda ts: (2 * (NT_P + ts) - 1, cb))

    def nxt(cb):
        return pl.BlockSpec((half, kw), lambda ts: (jnp.minimum(2 * (NT_P + ts) + 2, nhb - 1), cb))

    tab_m = pl.BlockSpec((TS, HD), lambda ts: (ts % TPS, 0))
    tab_p = pl.BlockSpec((half, HD), lambda ts: (jnp.maximum(2 * (ts % TPS) - 1, 0), 0))
    tab_n = pl.BlockSpec((half, HD), lambda ts: (jnp.minimum(2 * (ts % TPS) + 2, nhs - 1), 0))
    ctx = pl.BlockSpec((None, None, PAST, KV_HEADS, HD), lambda ts: (ts // TPS, layer, 0, 0, 0))
    return pl.pallas_call(
        _lat_attn_kernel,
        grid=(NT_S,),
        in_specs=[pl.BlockSpec(memory_space=pltpu.SMEM),
                  main(qw, C_Q // qw),
                  main(kw, C_K // kw), prev(C_K // kw), nxt(C_K // kw),
                  main(kw, C_V // kw), prev(C_V // kw), nxt(C_V // kw),
                  tab_m, tab_p, tab_n, tab_m, tab_p, tab_n, ctx, ctx,
                  pl.BlockSpec(memory_space=pl.ANY)],
        out_specs=pl.BlockSpec((TS, qw), lambda ts: (NT_P + ts, 0)),
        out_shape=jax.ShapeDtypeStruct((M, qw), BF16),
        input_output_aliases={16: 0},
        compiler_params=_params(("parallel",)),
        name="lat_attn",
    )(sink, pm, pm, pm, pm, pm, pm, pm, cos_t, cos_t, cos_t, sin_t, sin_t, sin_t,
      cache_k, cache_v, yb)


def _rope_tables():
    quarter = HD // 4
    inv = ROPE_THETA ** (-jnp.arange(quarter, dtype=F32) / quarter)
    t = jnp.arange(DEC_SEQ)
    row = (t // GRID_W).astype(F32)
    col = (t % GRID_W).astype(F32)
    ar = row[:, None] * inv[None, :]
    ac = col[:, None] * inv[None, :]
    cos_t = jnp.concatenate([jnp.cos(ar), jnp.cos(ar), jnp.cos(ac), jnp.cos(ac)], axis=-1)
    sin_t = jnp.concatenate([-jnp.sin(ar), jnp.sin(ar), -jnp.sin(ac), jnp.sin(ac)], axis=-1)
    return cos_t, sin_t


def _scan_tile(reverse):
    i = pl.program_id(0)
    return (NT - 1 - i) if reverse else i


def _tri_masks(reverse):
    ii = lax.broadcasted_iota(jnp.int32, (CHUNK, CHUNK), 0)
    jj = lax.broadcasted_iota(jnp.int32, (CHUNK, CHUNK), 1)
    incl = (jj >= ii) if reverse else (jj <= ii)
    strict = (jj > ii) if reverse else (jj < ii)
    return incl, strict


def _gated_rms_out(o, gate, nw):
    outs = []
    for h in range(4):
        lanes = slice(h * 128, (h + 1) * 128)
        outs.append(_rms_rows(o[:, lanes], nw) * _silu(gate[:, lanes]))
    return jnp.concatenate(outs, axis=1)


def _dot_hp_all(xs, ys):
    hh = [_dot(x[0], y[0]) for x, y in zip(xs, ys)]
    hl = [_dot(x[0], y[1]) for x, y in zip(xs, ys)]
    lh = [_dot(x[1], y[0]) for x, y in zip(xs, ys)]
    return [a + (b + c) for a, b, c in zip(hh, hl, lh)]


def _inv_unit_tri_all(lms):
    ii = lax.broadcasted_iota(jnp.int32, (CHUNK, CHUNK), 0)
    jj = lax.broadcasted_iota(jnp.int32, (CHUNK, CHUNK), 1)
    eye = jnp.where(ii == jj, 1.0, 0.0)
    ps = [eye - lm for lm in lms]
    mbs = [lm.astype(BF16) for lm in lms]
    mks = [_dot(m, m) for m in mbs]
    span = 4
    while span <= CHUNK:
        mbs = [m.astype(BF16) for m in mks]
        incs = [_dot(p.astype(BF16), m) for p, m in zip(ps, mbs)]
        if span < CHUNK:
            mks = [_dot(m, m) for m in mbs]
        ps = [p + inc for p, inc in zip(ps, incs)]
        span *= 2
    lts = _dot_hp_all([_split2(lm) for lm in lms], [_split2(t) for t in ps])
    rs = [(eye - t) - lt for t, lt in zip(ps, lts)]
    return [t + _dot(t.astype(BF16), r.astype(BF16)) for t, r in zip(ps, rs)]


def _dn_kernel(reverse, has_prev, *refs):
    if has_prev:
        qkv_ref, sm_ref, alog_ref, dtb_ref, s0_ref, oprev_ref, z_ref, nw_ref, o_ref, st_ref = refs[:10]
        scratch = refs[10:]
    else:
        x_ref, xp_ref, xn_ref, sm_ref, cw_ref, alog_ref, dtb_ref, s0_ref, o_ref, st_ref, qkv_out_ref = refs[:11]
        scratch = refs[11:]
    (s_scr, q_scr, k_scr, v_scr, b_scr, gcc_scr, gcr_scr, u_scr, w_scr, qe_scr, kd_scr, a_scr, o_scr) = scratch
    t = _scan_tile(reverse)
    is_p, first, last, _, _ = _tile_info(t)
    d = 1 if reverse else 0
    nchunk = TS // CHUNK
    e_last = 0 if reverse else CHUNK - 1

    @pl.when(last if reverse else first)
    def _():
        s_scr[...] = jnp.where(is_p, 0.0, s0_ref[...])

    if has_prev:
        q_scr[...] = qkv_ref[:, 0:DN_W]
        k_scr[...] = qkv_ref[:, DN_W:2 * DN_W]
        v_scr[...] = qkv_ref[:, 2 * DN_W:3 * DN_W]
    else:
        x = x_ref[...]
        ext = jnp.concatenate([jnp.where(first, 0.0, xp_ref[...]), x, jnp.where(last, 0.0, xn_ref[...])], axis=0)
        cw = cw_ref[...]
        ext_rows = TS + 2 * HALO
        conv = None
        for j in range(DN_CONV):
            off = j - DN_CONV // 2
            rows = ext if off == 0 else pltpu.roll(ext, (ext_rows - off) % ext_rows, 0)
            term = rows[HALO:HALO + TS] * cw[j:j + 1, :]
            conv = term if conv is None else conv + term
        act = _silu(conv)
        for h in range(DN_HEADS):
            lanes = slice(h * 128, (h + 1) * 128)
            qh = act[:, h * 128:(h + 1) * 128]
            kh = act[:, DN_W + h * 128:DN_W + (h + 1) * 128]
            q_scr[:, lanes] = qh * lax.rsqrt(jnp.sum(qh * qh, axis=-1, keepdims=True) + EPS) * (DN_DK ** -0.5)
            k_scr[:, lanes] = kh * lax.rsqrt(jnp.sum(kh * kh, axis=-1, keepdims=True) + EPS)
        v_scr[...] = act[:, 2 * DN_W:3 * DN_W]
        qkv_out_ref[:, 0:DN_W] = q_scr[...]
        qkv_out_ref[:, DN_W:2 * DN_W] = k_scr[...]
        qkv_out_ref[:, 2 * DN_W:3 * DN_W] = v_scr[...]
    sm = sm_ref[...]
    b_scr[...] = jax.nn.sigmoid(sm)
    g = -jnp.exp(alog_ref[...]) * _softplus(sm + dtb_ref[...])

    ti = lax.broadcasted_iota(jnp.int32, (TS, TS), 0)
    tj = lax.broadcasted_iota(jnp.int32, (TS, TS), 1)
    same = (ti // CHUNK) == (tj // CHUNK)
    lower = jnp.where(jnp.logical_and(same, tj <= ti), 1.0, 0.0).astype(BF16)
    upper = jnp.where(jnp.logical_and(same, tj >= ti), 1.0, 0.0).astype(BF16)
    tri, tri_t = (upper, lower) if reverse else (lower, upper)
    g1, g2, g3 = _split3(g)
    gcc_scr[...] = _dot(tri, g1) + _dot(tri, g2) + _dot(tri, g3)
    gcr = _dot_ta(g1, tri_t) + _dot_ta(g2, tri_t) + _dot_ta(g3, tri_t)
    for c in range(nchunk):
        gcr_scr[c] = gcr[:, c * CHUNK:(c + 1) * CHUNK]

    incl, strict = _tri_masks(reverse)

    def prep_body(pi, carry):
        rows, gblk, gcr_c, bt = {}, {}, {}, {}
        for cc in range(PREP_CHUNKS):
            c = pi * PREP_CHUNKS + cc
            rows[cc] = pl.ds(c * CHUNK, CHUNK)
            gblk[cc] = gcc_scr[rows[cc], :]
            gcr_c[cc] = gcr_scr[c]
            bt[cc] = b_scr[rows[cc], :]
        info = []
        for cc in range(PREP_CHUNKS):
            for h in range(DN_HEADS):
                lanes = slice(h * 128, (h + 1) * 128)
                r = SM_A + d * DN_HEADS + h
                gcol = gblk[cc][:, r:r + 1]
                grow = gcr_c[cc][r:r + 1, :]
                glast = gblk[cc][e_last:e_last + 1, r:r + 1]
                beta = bt[cc][:, SM_B + d * DN_HEADS + h:SM_B + d * DN_HEADS + h + 1]
                qc, kc, vc = q_scr[rows[cc], lanes], k_scr[rows[cc], lanes], v_scr[rows[cc], lanes]
                decay = jnp.exp(jnp.where(incl, gcol - grow, -1e30))
                kb = kc * beta
                eg = jnp.exp(gcol)
                qe_scr[rows[cc], lanes] = (qc * eg).astype(BF16)
                kd_scr[rows[cc], lanes] = (kc * jnp.exp(glast - gcol)).astype(BF16)
                info.append(dict(cc=cc, h=h, lanes=lanes, decay=decay, kbb=kb.astype(BF16), kcb=kc.astype(BF16),
                                 qcb=qc.astype(BF16), vbb=(vc * beta).astype(BF16), kgb=(kb * eg).astype(BF16)))
        kk = [_dot_tb(p["kbb"], p["kcb"]) for p in info]
        qk = [_dot_tb(p["qcb"], p["kcb"]) for p in info]
        for p, m in zip(info, qk):
            a_scr[rows[p["cc"]], p["h"] * CHUNK:(p["h"] + 1) * CHUNK] = (m * p["decay"]).astype(BF16)
        lms = [jnp.where(strict, m * p["decay"], 0.0) for p, m in zip(info, kk)]
        tinvs = [tv.astype(BF16) for tv in _inv_unit_tri_all(lms)]
        us = [_dot(tv, p["vbb"]) for tv, p in zip(tinvs, info)]
        ws = [_dot(tv, p["kgb"]) for tv, p in zip(tinvs, info)]
        for p, u, w in zip(info, us, ws):
            u_scr[rows[p["cc"]], p["lanes"]] = u
            w_scr[rows[p["cc"]], p["lanes"]] = w.astype(BF16)
        return carry

    for pi in range(nchunk // PREP_CHUNKS):
        prep_body(pi, 0)

    def chunk_body(ci, carry):
        c = (nchunk - 1 - ci) if reverse else ci
        rows = pl.ds(c * CHUNK, CHUNK)
        glast_row = gcc_scr[pl.ds(c * CHUNK + e_last, 1), :]
        heads = range(DN_HEADS)
        lanes = [slice(h * 128, (h + 1) * 128) for h in heads]
        s_old = [s_scr[h] for h in heads]
        sb = [s.astype(BF16) for s in s_old]
        ws = [_dot(w_scr[rows, lanes[h]], sb[h]) for h in heads]
        qs = [_dot(qe_scr[rows, lanes[h]], sb[h]) for h in heads]
        vnb = [(u_scr[rows, lanes[h]] - ws[h]).astype(BF16) for h in heads]
        av = [_dot(a_scr[rows, h * CHUNK:(h + 1) * CHUNK], vnb[h]) for h in heads]
        up = [_dot_ta(kd_scr[rows, lanes[h]], vnb[h]) for h in heads]
        for h in heads:
            r = SM_A + d * DN_HEADS + h
            o_scr[rows, lanes[h]] = qs[h] + av[h]
            s_scr[h] = jnp.exp(glast_row[:, r:r + 1]) * s_old[h] + up[h]
        return carry

    for ci in range(nchunk):
        chunk_body(ci, 0)

    if has_prev:
        o_ref[...] = _gated_rms_out(o_scr[...] + oprev_ref[...], z_ref[...], nw_ref[...]).astype(o_ref.dtype)
    else:
        o_ref[...] = o_scr[...]

    @pl.when(is_p)
    def _():
        st_ref[...] = s_scr[...]


def _seq_of_tile(t):
    return jnp.maximum(t - NT_P, 0) // TPS


def _dn_call(pm, ps, reverse, layer, conv_w, alog, dtb, state_delta, prev=None, norm_w=None):
    d = 1 if reverse else 0
    tmap = (lambda i: NT - 1 - i) if reverse else (lambda i: i)
    width = 3 * DN_W
    cb = C_DNQKV // width
    r = TS // HALO
    nblk = M // HALO
    small = [pl.BlockSpec((TS, 128), lambda i: (tmap(i), 0))]
    params = [pl.BlockSpec((1, 128), lambda i: (0, 0)),
              pl.BlockSpec((1, 128), lambda i: (0, 0)),
              pl.BlockSpec((None, None, None, DN_HEADS, DN_DK, 128), lambda i: (_seq_of_tile(tmap(i)), layer, d, 0, 0, 0))]
    tile_w = lambda w: pl.BlockSpec((TS, w), lambda i: (tmap(i), 0))
    has_prev = prev is not None
    out_specs = [tile_w(DN_W),
                 pl.BlockSpec((None, DN_HEADS, DN_DK, 128), lambda i: (jnp.minimum(tmap(i), NT_P - 1), 0, 0, 0))]
    out_shape = [jax.ShapeDtypeStruct((M, DN_W), BF16 if has_prev else F32),
                 jax.ShapeDtypeStruct((BATCH, DN_HEADS, DN_DK, 128), F32)]
    if has_prev:
        o_first, qkv = prev
        in_specs = [tile_w(width)] + small + params + [
            tile_w(DN_W), pl.BlockSpec((TS, DN_W), lambda i: (tmap(i), C_DNZ // DN_W)), pl.BlockSpec((1, 128), lambda i: (0, 0))]
        args = [qkv, ps, alog, dtb, state_delta, o_first, pm, norm_w.reshape(1, 128)]
    else:
        in_specs = [pl.BlockSpec((TS, width), lambda i: (tmap(i), cb)),
                    pl.BlockSpec((HALO, width), lambda i: (jnp.maximum(tmap(i) * r - 1, 0), cb)),
                    pl.BlockSpec((HALO, width), lambda i: (jnp.minimum((tmap(i) + 1) * r, nblk - 1), cb))] + small + [
            pl.BlockSpec((DN_CONV, width), lambda i: (0, 0))] + params
        args = [pm, pm, pm, ps, conv_w, alog, dtb, state_delta]
        out_specs.append(tile_w(width))
        out_shape.append(jax.ShapeDtypeStruct((M, width), F32))
    return pl.pallas_call(
        functools.partial(_dn_kernel, reverse, has_prev),
        grid=(NT,),
        in_specs=in_specs,
        out_specs=out_specs,
        out_shape=out_shape,
        scratch_shapes=[pltpu.VMEM((DN_HEADS, DN_DK, 128), F32),
                        pltpu.VMEM((TS, DN_W), F32), pltpu.VMEM((TS, DN_W), F32), pltpu.VMEM((TS, DN_W), F32),
                        pltpu.VMEM((TS, 128), F32), pltpu.VMEM((TS, 128), F32),
                        pltpu.VMEM((TS // CHUNK, 128, CHUNK), F32),
                        pltpu.VMEM((TS, DN_W), F32), pltpu.VMEM((TS, DN_W), BF16),
                        pltpu.VMEM((TS, DN_W), BF16), pltpu.VMEM((TS, DN_W), BF16),
                        pltpu.VMEM((TS, DN_HEADS * CHUNK), BF16),
                        pltpu.VMEM((TS, DN_W), F32)],
        compiler_params=_params(("arbitrary",)),
        name="deltanet_bwd" if reverse else "deltanet_fwd",
    )(*args)


def _gla_kernel(reverse, has_prev, q_ref, k_ref, v_ref, sm_ref, w2_ref, b2_ref, s0_ref, *rest):
    if has_prev:
        oprev_ref, r_ref, nw_ref, o_ref, st_ref, s_scr, o_scr = rest
    else:
        o_ref, st_ref, s_scr, o_scr = rest
    t = _scan_tile(reverse)
    is_p, first, last, _, _ = _tile_info(t)

    @pl.when(last if reverse else first)
    def _():
        s_scr[...] = jnp.where(is_p, 0.0, s0_ref[...])

    logits = _dot(sm_ref[...].astype(BF16), w2_ref[...]) + b2_ref[...]
    gk = -_softplus(-logits) / GLA_NORMALIZER

    ti = lax.broadcasted_iota(jnp.int32, (TS, TS), 0)
    tj = lax.broadcasted_iota(jnp.int32, (TS, TS), 1)
    same = (ti // CHUNK) == (tj // CHUNK)
    tri = jnp.where(jnp.logical_and(same, (tj >= ti) if reverse else (tj <= ti)), 1.0, 0.0).astype(BF16)
    g1, g2, g3 = _split3(gk)
    gc = _dot(tri, g1) + _dot(tri, g2) + _dot(tri, g3)

    incl, _ = _tri_masks(reverse)
    nchunk = TS // CHUNK
    e_last = 0 if reverse else CHUNK - 1
    eg = jnp.exp(gc)
    q_all = q_ref[...] * (GLA_DK ** -0.5)
    k_all = k_ref[...]
    qg_all = (q_all * eg).astype(BF16)
    kg_all = (k_all * jnp.exp(-gc)).astype(BF16)
    v_all = v_ref[...].astype(BF16)

    probs = [(c, h) for c in range(nchunk) for h in range(GLA_HEADS)]
    rws = {c: slice(c * CHUNK, (c + 1) * CHUNK) for c in range(nchunk)}
    kls = {h: slice(h * GLA_DK, (h + 1) * GLA_DK) for h in range(GLA_HEADS)}
    vls = {h: slice(h * GLA_DV, (h + 1) * GLA_DV) for h in range(GLA_HEADS)}
    glast = {c: gc[c * CHUNK + e_last:c * CHUNK + e_last + 1, :] for c in range(nchunk)}
    kd = {c: (k_all[rws[c], :] * jnp.exp(glast[c] - gc[rws[c], :])).astype(BF16) for c in range(nchunk)}
    a = {p: jnp.where(incl, _dot_tb(qg_all[rws[p[0]], kls[p[1]]], kg_all[rws[p[0]], kls[p[1]]]), 0.0).astype(BF16)
         for p in probs}
    av = {p: _dot(a[p], v_all[rws[p[0]], vls[p[1]]]) for p in probs}
    up = {p: _dot_ta(v_all[rws[p[0]], vls[p[1]]], kd[p[0]][:, kls[p[1]]]) for p in probs}
    order = list(reversed(range(nchunk))) if reverse else list(range(nchunk))
    s_in = {}
    for h in range(GLA_HEADS):
        s = s_scr[h]
        for c in order:
            s_in[(c, h)] = s.astype(BF16)
            s = jnp.exp(glast[c][:, kls[h]]) * s + up[(c, h)]
        s_scr[h] = s
    for p in probs:
        c, h = p
        o_scr[rws[c], vls[h]] = _dot_tb(qg_all[rws[c], kls[h]], s_in[p]) + av[p]

    if has_prev:
        o_ref[...] = _gated_rms_out(o_scr[...] + oprev_ref[...], r_ref[...], nw_ref[...]).astype(o_ref.dtype)
    else:
        o_ref[...] = o_scr[...]

    @pl.when(is_p)
    def _():
        st_ref[...] = s_scr[...]


def _gla_call(pm, ps, reverse, w2pad, b2, s0t, prev=None, norm_w=None):
    tmap = (lambda i: NT - 1 - i) if reverse else (lambda i: i)
    qkw = GLA_HEADS * GLA_DK
    vw = GLA_HEADS * GLA_DV
    in_specs = [
        pl.BlockSpec((TS, qkw), lambda i: (tmap(i), C_GQ // qkw)),
        pl.BlockSpec((TS, qkw), lambda i: (tmap(i), C_GK // qkw)),
        pl.BlockSpec((TS, vw), lambda i: (tmap(i), C_GV // vw)),
        pl.BlockSpec((TS, 128), lambda i: (tmap(i), 0)),
        pl.BlockSpec((128, qkw), lambda i: (0, 0)),
        pl.BlockSpec((1, qkw), lambda i: (0, 0)),
        pl.BlockSpec((None, GLA_HEADS, GLA_DV, GLA_DK), lambda i: (_seq_of_tile(tmap(i)), 0, 0, 0)),
    ]
    args = [pm, pm, pm, ps, w2pad, b2, s0t]
    has_prev = prev is not None
    if has_prev:
        in_specs += [pl.BlockSpec((TS, vw), lambda i: (tmap(i), 0)),
                     pl.BlockSpec((TS, vw), lambda i: (tmap(i), C_GR // vw)),
                     pl.BlockSpec((1, 128), lambda i: (0, 0))]
        args += [prev, pm, norm_w.reshape(1, 128)]
    out_dtype = BF16 if has_prev else F32
    return pl.pallas_call(
        functools.partial(_gla_kernel, reverse, has_prev),
        grid=(NT,),
        in_specs=in_specs,
        out_specs=[pl.BlockSpec((TS, vw), lambda i: (tmap(i), 0)),
                   pl.BlockSpec((None, GLA_HEADS, GLA_DV, GLA_DK), lambda i: (jnp.minimum(tmap(i), NT_P - 1), 0, 0, 0))],
        out_shape=[jax.ShapeDtypeStruct((M, vw), out_dtype),
                   jax.ShapeDtypeStruct((BATCH, GLA_HEADS, GLA_DV, GLA_DK), F32)],
        scratch_shapes=[pltpu.VMEM((GLA_HEADS, GLA_DV, GLA_DK), F32),
                        pltpu.VMEM((TS, vw), F32)],
        compiler_params=_params(("arbitrary",)),
        name="gla_bwd" if reverse else "gla_fwd",
    )(*args)


MOD_SH1, MOD_SC1, MOD_GT1, MOD_SH2, MOD_SC2, MOD_GT2 = range(N_MOD)


def kernel(x_prompt, x_sample, cache_k, cache_v, state_delta, state_gla, c, c_ctx, w_mod, b_mod, norm1, w_in, pool_w, pool_scale, attn_sink, dn_conv, dn_a_log, dn_dt_bias, dn_norm, gla_w2, gla_b2, gla_norm, w_br_pool, w_br_attn, w_br_delta, w_br_gla, w_out, norm2, w_gate, w_up, w_down, norm_f):
    x = jnp.concatenate([x_prompt.reshape(MP, D), x_sample.reshape(MS, D)], axis=0)
    cond8 = jnp.concatenate([c_ctx[None, :], c, jnp.zeros((8 - 1 - DEC_BATCH, D), F32)], axis=0)
    mod = _modulation(cond8, w_mod, b_mod).reshape(DEPTH, 8, N_MOD, 1, D)
    cos_t, sin_t = _rope_tables()
    s_gla_t = jnp.swapaxes(state_gla, -1, -2)
    w_out_bf16 = w_out.astype(BF16)
    w_brs = (w_br_pool, w_br_attn, w_br_delta, w_br_gla)
    w_in_t = jnp.swapaxes(w_in, 1, 2)
    gates_t = _gate_weights(w_in_t)

    new_k = jnp.zeros((BATCH, DEPTH, SEQ, KV_HEADS, HD), F32)
    new_v = jnp.zeros((BATCH, DEPTH, SEQ, KV_HEADS, HD), F32)
    yb = jnp.zeros((M, Q_HEADS * HD), BF16)
    sds, sgs = [], []
    for l in range(DEPTH):
        h, ps = _prenorm(x, norm1, mod, w_in_t, l, MOD_SC1, MOD_SH1)
        pm = _mix_proj(h, w_in_t, l)

        ya = _pool(pm, pool_w[l], pool_scale[l])
        yb, new_k, new_v = _ctx_attn(pm, attn_sink[l], new_k, new_v, l, yb)
        yb = _lat_attn(pm, attn_sink[l], cos_t, sin_t, cache_k, cache_v, l, yb)

        alog = jnp.zeros((1, 128), F32).at[0, SM_A:SM_A + 8].set(dn_a_log[l].reshape(8))
        dtb = jnp.zeros((1, 128), F32).at[0, SM_A:SM_A + 8].set(dn_dt_bias[l].reshape(8))
        o_b, sd_b, dn_qkv = _dn_call(pm, ps, True, l, dn_conv[l], alog, dtb, state_delta)
        yc, sd_f = _dn_call(pm, ps, False, l, dn_conv[l], alog, dtb, state_delta, prev=(o_b, dn_qkv), norm_w=dn_norm[l])

        w2pad = [jnp.zeros((128, GLA_HEADS * GLA_DK), F32).at[SM_LR + o * GLA_RANK:SM_LR + (o + 1) * GLA_RANK].set(gla_w2[l, o]).astype(BF16)
                 for o in range(2)]
        g_b, sg_b = _gla_call(pm, ps, True, w2pad[1], gla_b2[l, 1].reshape(1, -1), s_gla_t[:, l, 1])
        yd, sg_f = _gla_call(pm, ps, False, w2pad[0], gla_b2[l, 0].reshape(1, -1), s_gla_t[:, l, 0], prev=g_b, norm_w=gla_norm[l])

        merged = _merge(h, gates_t, (ya, yb, yc, yd), w_brs, l)
        x, h2 = _out_proj(merged, w_out_bf16, x, mod, norm2, l, MOD_GT1, MOD_SC2, MOD_SH2)
        act = _ffn_up(h2, w_gate, w_up, l)
        x = _ffn_down(act, w_down, x, mod, l, MOD_GT2)

        sds.append(jnp.stack([sd_f, sd_b], axis=1))
        sgs.append(jnp.swapaxes(jnp.stack([sg_f, sg_b], axis=1), -1, -2))

    y_ctx, y_lat = _finalnorm(x, norm_f)
    return (y_ctx.reshape(BATCH, SEQ, D), y_lat.reshape(DEC_BATCH, DEC_SEQ, D),
            new_k, new_v, jnp.stack(sds, axis=1), jnp.stack(sgs, axis=1))
```

```python
import functools

import jax
import jax.numpy as jnp
from jax import lax
from jax.experimental import pallas as pl
from jax.experimental.pallas import tpu as pltpu

F32 = jnp.float32
BF16 = jnp.bfloat16

D = 2048
BATCH = 16
SEQ = 256
DEPTH = 4
DEC_BATCH = 4
DEC_SEQ = 2048
PAST = 256
GRID_W = 64
HD = 128
ROPE_THETA = 10000.0
EPS = 1e-6
N_MOD = 6
POOL_WINDOWS = (2, 4, 8, 16)
POOL_W = 512
Q_HEADS = 8
KV_HEADS = 2
GROUP = 4
ATTN_WINDOW = 128
DN_HEADS = 4
DN_DK = 128
DN_W = 512
DN_CONV = 4
GLA_HEADS = 4
GLA_DK = 64
GLA_DV = 128
GLA_RANK = 16
GLA_NORMALIZER = 16.0
FFN = 5632
CHUNK = 64
PREP_CHUNKS = 4

MP = BATCH * SEQ
MS = DEC_BATCH * DEC_SEQ
M = MP + MS
TS = 256
NT_P = MP // TS
TPS = DEC_SEQ // TS
NT_S = MS // TS
NT = NT_P + NT_S
HALO = 8
assert max(POOL_WINDOWS) // 2 <= HALO and DN_CONV // 2 <= HALO
assert ATTN_WINDOW <= TS // 2

C_Q, C_POOL, C_DNQKV, C_K, C_V, C_DNZ = 0, 1024, 1536, 3072, 3328, 3584
C_GQ, C_GK, C_GV, C_GR = 4096, 4352, 4608, 5120
MIXW = 5632
MIX_TN = 512
MIX_SRC = (1, 2, 0, 4, 5, 6, 3, 7, 8, 9, 10)
MIX_ALIGNED = 8
MIX_SHIFT = 16
SMALL_COL_A = 4096
SMALL_COL_LR = 5648
GATE_COL = 5680
IN_WIDTH = GATE_COL + 4 * D
SM_A, SM_B, SM_LR = 0, 8, 16

VMEM_LIMIT = 56 * 1024 * 1024


def _params(sem):
    return pltpu.CompilerParams(dimension_semantics=sem, vmem_limit_bytes=VMEM_LIMIT)


def _cond_row(row0):
    return jnp.where(row0 < MP, 0, 1 + (row0 - MP) // DEC_SEQ)


def _tile_info(t):
    is_p = t < NT_P
    s = jnp.maximum(t - NT_P, 0)
    j = s % TPS
    first = jnp.logical_or(is_p, j == 0)
    last = jnp.logical_or(is_p, j == TPS - 1)
    base = jnp.where(is_p, 0, j * TS)
    seq_len = jnp.where(is_p, SEQ, DEC_SEQ)
    return is_p, first, last, base, seq_len


def _silu(x):
    return x * jax.nn.sigmoid(x)


def _softplus(x):
    return jnp.maximum(x, 0.0) + jnp.log(1.0 + jnp.exp(-jnp.abs(x)))


def _dot(a, b):
    return jnp.dot(a, b, preferred_element_type=F32)


def _dot_tb(a, b):
    return lax.dot_general(a, b, (((1,), (1,)), ((), ())), preferred_element_type=F32)


def _dot_ta(a, b):
    return lax.dot_general(a, b, (((0,), (0,)), ((), ())), preferred_element_type=F32)


def _split2(x):
    hi = x.astype(BF16)
    lo = (x - hi.astype(F32)).astype(BF16)
    return hi, lo


def _split3(x):
    p1 = x.astype(BF16)
    r = x - p1.astype(F32)
    p2 = r.astype(BF16)
    p3 = (r - p2.astype(F32)).astype(BF16)
    return p1, p2, p3


def _mod_kernel(c_ref, w_ref, b_ref, o_ref):
    a = _silu(c_ref[...]).astype(BF16)
    o_ref[...] = _dot(a, w_ref[...].astype(BF16)) + b_ref[...]


def _modulation(cond8, w_mod, b_mod):
    tn = 1024
    n = N_MOD * D
    return pl.pallas_call(
        _mod_kernel,
        grid=(DEPTH, n // tn),
        in_specs=[pl.BlockSpec((8, D), lambda l, j: (0, 0)),
                  pl.BlockSpec((None, D, tn), lambda l, j: (l, 0, j)),
                  pl.BlockSpec((None, 1, tn), lambda l, j: (l, 0, j))],
        out_specs=pl.BlockSpec((None, 8, tn), lambda l, j: (l, 0, j)),
        out_shape=jax.ShapeDtypeStruct((DEPTH, 8, n), F32),
        compiler_params=_params(("parallel", "parallel")),
        name="modulation",
    )(cond8, w_mod, b_mod.reshape(DEPTH, 1, n))


def _rms_rows(x, w):
    ms = jnp.mean(x * x, axis=-1, keepdims=True)
    return x * lax.rsqrt(ms + EPS) * w


def _prenorm_kernel(nctx, *refs):
    if nctx is None:
        x_ref, nw_ref, sc_ref, sh_ref, wa_ref, wl_ref, h_ref, ps_ref, wb_scr = refs
        x = x_ref[...]
    else:
        xp_ref, xs_ref, nw_ref, sc_ref, sh_ref, wa_ref, wl_ref, h_ref, ps_ref, x_out_ref, wb_scr = refs
        x = jnp.where(pl.program_id(0) < nctx, xp_ref[...], xs_ref[...])
        x_out_ref[...] = x

    @pl.when(pl.program_id(0) == 0)
    def _():
        pad = jnp.zeros((128 - SM_LR - 2 * GLA_RANK, D), F32)
        wb_scr[...] = jnp.concatenate([wa_ref[0], wl_ref[0], pad], axis=0).astype(BF16)

    y = _rms_rows(x, nw_ref[...])
    h = (y * (1.0 + sc_ref[...]) + sh_ref[...]).astype(h_ref.dtype)
    h_ref[...] = h
    ps_ref[...] = _dot_tb(h, wb_scr[...])


def _mod_spec(layer, chunk, tm, tn=D, row_axis=0, col_axis=None):
    def index(*g):
        col = 0 if col_axis is None else g[col_axis]
        return (layer, _cond_row(g[row_axis] * tm), chunk, 0, col)
    return pl.BlockSpec((None, None, None, 1, tn), index)


def _prenorm(x, norm_w, mod, w_in_t, layer, sc_chunk, sh_chunk):
    tm = 512
    row = pl.BlockSpec((tm, D), lambda i: (i, 0))
    split = isinstance(x, tuple)
    nctx = MP // tm
    if split:
        x_specs = [pl.BlockSpec((tm, D), lambda i: (jnp.minimum(i, nctx - 1), 0)),
                   pl.BlockSpec((tm, D), lambda i: (jnp.maximum(i - nctx, 0), 0))]
        xs = list(x)
    else:
        x_specs, xs = [row], [x]
    out_specs = [row, pl.BlockSpec((tm, 128), lambda i: (i, 0))]
    out_shape = [jax.ShapeDtypeStruct((M, D), BF16), jax.ShapeDtypeStruct((M, 128), F32)]
    if split:
        out_specs.append(row)
        out_shape.append(jax.ShapeDtypeStruct((M, D), F32))
    return pl.pallas_call(
        functools.partial(_prenorm_kernel, nctx if split else None),
        grid=(M // tm,),
        in_specs=x_specs + [pl.BlockSpec((None, 1, D), lambda i: (layer, 0, 0)),
                            _mod_spec(layer, sc_chunk, tm), _mod_spec(layer, sh_chunk, tm),
                            _wt_rows(layer, SM_LR, lambda i: SMALL_COL_A),
                            _wt_rows(layer, 2 * GLA_RANK, lambda i: SMALL_COL_LR)],
        out_specs=out_specs,
        out_shape=out_shape,
        scratch_shapes=[pltpu.VMEM((128, D), BF16)],
        compiler_params=_params(("arbitrary",)),
        name="prenorm",
    )(*xs, norm_w.reshape(DEPTH, 1, D), mod, mod, w_in_t, w_in_t)


def _finalnorm_kernel(tm, x_ref, nw_ref, yp_ref, ys_ref):
    y = _rms_rows(x_ref[...], nw_ref[...])
    is_ctx = pl.program_id(0) < MP // tm

    @pl.when(is_ctx)
    def _():
        yp_ref[...] = y

    @pl.when(jnp.logical_not(is_ctx))
    def _():
        ys_ref[...] = y


def _finalnorm(x, nw):
    tm = 512
    nctx = MP // tm
    return pl.pallas_call(
        functools.partial(_finalnorm_kernel, tm),
        grid=(M // tm,),
        in_specs=[pl.BlockSpec((tm, D), lambda i: (i, 0)),
                  pl.BlockSpec((1, D), lambda i: (0, 0))],
        out_specs=[pl.BlockSpec((tm, D), lambda i: (jnp.minimum(i, nctx - 1), 0)),
                   pl.BlockSpec((tm, D), lambda i: (jnp.maximum(i - nctx, 0), 0))],
        out_shape=[jax.ShapeDtypeStruct((MP, D), F32), jax.ShapeDtypeStruct((MS, D), F32)],
        compiler_params=_params(("arbitrary",)),
        name="finalnorm",
    )(x, nw.reshape(1, D))


def _first_row_block():
    return pl.program_id(1) == 0


def _wt_rows(layer, nrows, row_fn):
    def index(*g):
        row = row_fn(*g)
        return (layer, row if isinstance(row, int) else pl.multiple_of(row, 8), 0)
    return pl.BlockSpec((pl.Element(1), pl.Element(nrows), pl.Element(D)), index)


def _mix_kernel(h_ref, w_ref, o_ref, wb_scr):
    @pl.when(_first_row_block())
    def _():
        wb_scr[...] = w_ref[0].astype(BF16)

    o_ref[...] = _dot_tb(h_ref[...], wb_scr[...])


def _mix_src_row(j):
    src = j
    for out_tile, src_tile in enumerate(MIX_SRC):
        src = jnp.where(j == out_tile, src_tile, src)
    return src * MIX_TN + jnp.where(src >= MIX_ALIGNED, MIX_SHIFT, 0)


def _mix_proj(h, w_in_t, layer):
    tm, tn = 2048, MIX_TN
    return pl.pallas_call(
        _mix_kernel,
        grid=(MIXW // tn, M // tm),
        in_specs=[pl.BlockSpec((tm, D), lambda j, i: (i, 0)),
                  _wt_rows(layer, tn, lambda j, i: _mix_src_row(j))],
        out_specs=pl.BlockSpec((tm, tn), lambda j, i: (i, j)),
        out_shape=jax.ShapeDtypeStruct((M, MIXW), F32),
        scratch_shapes=[pltpu.VMEM((tn, D), BF16)],
        compiler_params=_params(("arbitrary", "arbitrary")),
        name="mix_proj",
    )(h, w_in_t)


def _ffn_up_kernel(h_ref, wg_ref, wu_ref, o_ref, wg_scr, wu_scr):
    @pl.when(_first_row_block())
    def _():
        wg_scr[...] = wg_ref[...].astype(BF16)
        wu_scr[...] = wu_ref[...].astype(BF16)

    h = h_ref[...]
    o_ref[...] = (_silu(_dot(h, wg_scr[...])) * _dot(h, wu_scr[...])).astype(o_ref.dtype)


def _ffn_up(h, w_gate, w_up, layer):
    tm, tn = 2048, 512
    wspec = pl.BlockSpec((None, D, tn), lambda j, i: (layer, 0, j))
    return pl.pallas_call(
        _ffn_up_kernel,
        grid=(FFN // tn, M // tm),
        in_specs=[pl.BlockSpec((tm, D), lambda j, i: (i, 0)), wspec, wspec],
        out_specs=pl.BlockSpec((tm, tn), lambda j, i: (i, j)),
        out_shape=jax.ShapeDtypeStruct((M, FFN), BF16),
        scratch_shapes=[pltpu.VMEM((D, tn), BF16), pltpu.VMEM((D, tn), BF16)],
        compiler_params=_params(("arbitrary", "arbitrary")),
        name="ffn_up",
    )(h, w_gate, w_up)


def _ffn_down_kernel(a_ref, w_ref, x_ref, gt_ref, o_ref, wb_scr):
    @pl.when(_first_row_block())
    def _():
        wb_scr[...] = w_ref[...].astype(BF16)

    o_ref[...] = x_ref[...] + gt_ref[...] * _dot(a_ref[...], wb_scr[...])


def _ffn_down(act, w_down, x, mod, layer, gt_chunk):
    tm, tn = 1024, 512
    return pl.pallas_call(
        _ffn_down_kernel,
        grid=(D // tn, M // tm),
        in_specs=[pl.BlockSpec((tm, FFN), lambda j, i: (i, 0)),
                  pl.BlockSpec((None, FFN, tn), lambda j, i: (layer, 0, j), pipeline_mode=pl.Buffered(1)),
                  pl.BlockSpec((tm, tn), lambda j, i: (i, j)),
                  _mod_spec(layer, gt_chunk, tm, tn, row_axis=1, col_axis=0)],
        out_specs=pl.BlockSpec((tm, tn), lambda j, i: (i, j)),
        out_shape=jax.ShapeDtypeStruct((M, D), F32),
        scratch_shapes=[pltpu.VMEM((FFN, tn), BF16)],
        compiler_params=_params(("arbitrary", "arbitrary")),
        name="ffn_down",
    )(act, w_down, x, mod)


def _gate_cast_kernel(w_ref, o_ref):
    o_ref[...] = w_ref[0].astype(o_ref.dtype)


def _gate_weights(w_in_t):
    tr = 1024
    return pl.pallas_call(
        _gate_cast_kernel,
        grid=(DEPTH, 4 * D // tr),
        in_specs=[pl.BlockSpec((pl.Element(1), pl.Element(tr), pl.Element(D)),
                               lambda l, j: (l, pl.multiple_of(GATE_COL + j * tr, 8), 0))],
        out_specs=pl.BlockSpec((None, tr, D), lambda l, j: (l, j, 0)),
        out_shape=jax.ShapeDtypeStruct((DEPTH, 4 * D, D), BF16),
        compiler_params=_params(("parallel", "parallel")),
        name="gate_cast",
    )(w_in_t)


BR_WIDTHS = (POOL_W, Q_HEADS * HD, DN_W, GLA_HEADS * GLA_DV)


def _merge_kernel(h_ref, *refs):
    g_refs, y_refs, w_refs, o_ref = refs[0:4], refs[4:8], refs[8:12], refs[12]
    h = h_ref[...]
    acc = None
    for b in range(4):
        term = jax.nn.sigmoid(_dot_tb(h, g_refs[b][...])) * _dot(y_refs[b][...], w_refs[b][...].astype(BF16))
        acc = term if acc is None else acc + term
    o_ref[...] = acc.astype(o_ref.dtype)


def _merge(h, gates_t, ys, w_brs, layer):
    tm, tn = 1024, 512
    nj = D // tn
    gate_specs = [pl.BlockSpec((None, tn, D), functools.partial(lambda i, j, b: (layer, b * nj + j, 0), b=b))
                  for b in range(4)]
    y_specs = [pl.BlockSpec((tm, w), lambda i, j: (i, 0)) for w in BR_WIDTHS]
    w_specs = [pl.BlockSpec((None, w, tn), lambda i, j: (layer, 0, j)) for w in BR_WIDTHS]
    return pl.pallas_call(
        _merge_kernel,
        grid=(M // tm, nj),
        in_specs=[pl.BlockSpec((tm, D), lambda i, j: (i, 0))] + gate_specs + y_specs + w_specs,
        out_specs=pl.BlockSpec((tm, tn), lambda i, j: (i, j)),
        out_shape=jax.ShapeDtypeStruct((M, D), BF16),
        compiler_params=_params(("parallel", "arbitrary")),
        name="merge",
    )(h, *([gates_t] * 4), *ys, *w_brs)


def _out_proj_kernel(m_ref, w_ref, x_ref, gt_ref, nw_ref, sc_ref, sh_ref, xo_ref, h_ref):
    xn = x_ref[...] + gt_ref[...] * _dot(m_ref[...], w_ref[...])
    xo_ref[...] = xn
    h_ref[...] = (_rms_rows(xn, nw_ref[...]) * (1.0 + sc_ref[...]) + sh_ref[...]).astype(h_ref.dtype)


def _out_proj(merged, w_out_bf16, x, mod, norm_w, layer, gt_chunk, sc_chunk, sh_chunk):
    tm = 512
    row = pl.BlockSpec((tm, D), lambda i: (i, 0))
    return pl.pallas_call(
        _out_proj_kernel,
        grid=(M // tm,),
        in_specs=[row,
                  pl.BlockSpec((None, D, D), lambda i: (layer, 0, 0), pipeline_mode=pl.Buffered(1)),
                  row, _mod_spec(layer, gt_chunk, tm),
                  pl.BlockSpec((None, 1, D), lambda i: (layer, 0, 0)),
                  _mod_spec(layer, sc_chunk, tm), _mod_spec(layer, sh_chunk, tm)],
        out_specs=[row, row],
        out_shape=[jax.ShapeDtypeStruct((M, D), F32), jax.ShapeDtypeStruct((M, D), BF16)],
        compiler_params=_params(("parallel",)),
        name="out_proj",
    )(merged, w_out_bf16, x, mod, norm_w.reshape(DEPTH, 1, D), mod, mod)


def _pool_kernel(u_ref, up_ref, un_ref, w_ref, sc_ref, o_ref):
    t = pl.program_id(0)
    _, first, last, base, seq_len = _tile_info(t)
    u = u_ref[...]
    prev = jnp.where(first, 0.0, up_ref[...])
    nxt = jnp.where(last, 0.0, un_ref[...])
    ext = jnp.concatenate([prev, u, nxt], axis=0)
    pos = base + lax.broadcasted_iota(jnp.int32, (TS, 1), 0)
    outs = []
    for gi, win in enumerate(POOL_WINDOWS):
        lanes = slice(gi * 128, (gi + 1) * 128)
        p = ext[:, lanes]
        s = 1
        while s < win:
            p = p + pltpu.roll(p, s, 0)
            s *= 2
        d = HALO + win // 2 - 1
        ssum = p[d:d + TS]
        cnt = (jnp.minimum(pos + win // 2, seq_len) - jnp.maximum(pos - win // 2, 0)).astype(F32)
        pg = ssum / cnt - u[:, lanes]
        outs.append(_dot(pg.astype(BF16), w_ref[gi].astype(BF16)))
    o_ref[...] = (jnp.concatenate(outs, axis=1) * sc_ref[...]).astype(o_ref.dtype)


def _halo_specs(width, colblk):
    r = TS // HALO
    nblk = M // HALO
    prev = pl.BlockSpec((HALO, width), lambda t: (jnp.maximum(t * r - 1, 0), colblk))
    nxt = pl.BlockSpec((HALO, width), lambda t: (jnp.minimum((t + 1) * r, nblk - 1), colblk))
    return prev, nxt


def _pool(pm, pool_w, pool_scale):
    cb = C_POOL // POOL_W
    prev, nxt = _halo_specs(POOL_W, cb)
    return pl.pallas_call(
        _pool_kernel,
        grid=(NT,),
        in_specs=[pl.BlockSpec((TS, POOL_W), lambda t: (t, cb)), prev, nxt,
                  pl.BlockSpec((4, 128, 128), lambda t: (0, 0, 0)),
                  pl.BlockSpec((1, POOL_W), lambda t: (0, 0))],
        out_specs=pl.BlockSpec((TS, POOL_W), lambda t: (t, 0)),
        out_shape=jax.ShapeDtypeStruct((M, POOL_W), BF16),
        compiler_params=_params(("parallel",)),
        name="pool",
    )(pm, pm, pm, pool_w, pool_scale.reshape(1, POOL_W))


def _ctx_attn_kernel(sink_ref, q_ref, k_ref, v_ref, nk_in, nv_in, yb_in, o_ref, nk_ref, nv_ref):
    del nk_in, nv_in, yb_in
    q = q_ref[...] * (HD ** -0.5)
    k32 = k_ref[...]
    v32 = v_ref[...]
    k = k32.astype(BF16)
    v = v32.astype(BF16)
    for n in range(KV_HEADS):
        nk_ref[:, n, :] = k32[:, n * HD:(n + 1) * HD]
        nv_ref[:, n, :] = v32[:, n * HD:(n + 1) * HD]
        kn = k[:, n * HD:(n + 1) * HD]
        vn = v[:, n * HD:(n + 1) * HD]
        for g in range(GROUP):
            h = n * GROUP + g
            qh = q[:, h * HD:(h + 1) * HD].astype(BF16)
            s = _dot_tb(qh, kn)
            snk = sink_ref[h]
            m = jnp.maximum(jnp.max(s, axis=-1, keepdims=True), snk)
            e = jnp.exp(s - m)
            den = jnp.sum(e, axis=-1, keepdims=True) + jnp.exp(snk - m)
            o = _dot(e.astype(BF16), vn) / den
            o_ref[:, h * HD:(h + 1) * HD] = o.astype(o_ref.dtype)


def _ctx_attn(pm, sink, new_k, new_v, layer, yb_buf):
    qw, kw = Q_HEADS * HD, KV_HEADS * HD
    cache_blk = pl.BlockSpec((None, None, SEQ, KV_HEADS, HD), lambda t: (t, layer, 0, 0, 0))
    anyspace = pl.BlockSpec(memory_space=pl.ANY)
    return pl.pallas_call(
        _ctx_attn_kernel,
        grid=(NT_P,),
        in_specs=[pl.BlockSpec(memory_space=pltpu.SMEM),
                  pl.BlockSpec((TS, qw), lambda t: (t, C_Q // qw)),
                  pl.BlockSpec((TS, kw), lambda t: (t, C_K // kw)),
                  pl.BlockSpec((TS, kw), lambda t: (t, C_V // kw)),
                  anyspace, anyspace, anyspace],
        out_specs=[pl.BlockSpec((TS, qw), lambda t: (t, 0)), cache_blk, cache_blk],
        out_shape=[jax.ShapeDtypeStruct(yb_buf.shape, yb_buf.dtype),
                   jax.ShapeDtypeStruct(new_k.shape, new_k.dtype), jax.ShapeDtypeStruct(new_v.shape, new_v.dtype)],
        input_output_aliases={4: 1, 5: 2, 6: 0},
        compiler_params=_params(("parallel",)),
        name="ctx_attn",
    )(sink, pm, pm, pm, new_k, new_v, yb_buf)


def _rope(x, cs, sn):
    lane = lax.broadcasted_iota(jnp.int32, x.shape, 1)
    swapped = jnp.where((lane % 64) < 32, pltpu.roll(x, 96, 1), pltpu.roll(x, 32, 1))
    return x * cs + swapped * sn


def _lat_attn_kernel(sink_ref, q_ref, km_ref, kp_ref, kn_ref, vm_ref, vp_ref, vn_ref,
                     cm_ref, cp_ref, cn_ref, sm_ref, sp_ref, sn_ref, kc_ref, vc_ref, yb_ref, o_ref):
    del yb_ref
    ts = pl.program_id(0)
    base = (ts % TPS) * TS
    half = TS // 2
    nk = TS + 2 * half
    q = q_ref[...] * (HD ** -0.5)
    cq, sq = cm_ref[...], sm_ref[...]
    kl = jnp.concatenate([kp_ref[...], km_ref[...], kn_ref[...]], axis=0)
    vl = jnp.concatenate([vp_ref[...], vm_ref[...], vn_ref[...]], axis=0).astype(BF16)
    ck = jnp.concatenate([cp_ref[...], cq, cn_ref[...]], axis=0)
    sk = jnp.concatenate([sp_ref[...], sq, sn_ref[...]], axis=0)
    qpos = base + lax.broadcasted_iota(jnp.int32, (TS, 1), 0)
    kpos = base - half + lax.broadcasted_iota(jnp.int32, (1, nk), 1)
    dist = qpos - kpos
    ok = (jnp.abs(dist) <= ATTN_WINDOW).astype(jnp.int32) * (kpos >= 0).astype(jnp.int32) * (kpos < DEC_SEQ).astype(jnp.int32)
    bias = jnp.where(ok > 0, 0.0, -1e30)
    for n in range(KV_HEADS):
        lanes = slice(n * HD, (n + 1) * HD)
        kr = _rope(kl[:, lanes], ck, sk).astype(BF16)
        vln = vl[:, lanes]
        kcn = kc_ref[:, n, :].astype(BF16)
        vcn = vc_ref[:, n, :].astype(BF16)
        for g in range(GROUP):
            h = n * GROUP + g
            qh = _rope(q[:, h * HD:(h + 1) * HD], cq, sq).astype(BF16)
            s_c = _dot_tb(qh, kcn)
            s_l = _dot_tb(qh, kr) + bias
            snk = sink_ref[h]
            m = jnp.maximum(jnp.maximum(jnp.max(s_c, axis=-1, keepdims=True),
                                        jnp.max(s_l, axis=-1, keepdims=True)), snk)
            e_c = jnp.exp(s_c - m)
            e_l = jnp.exp(s_l - m)
            den = jnp.sum(e_c, axis=-1, keepdims=True) + jnp.sum(e_l, axis=-1, keepdims=True) + jnp.exp(snk - m)
            o = (_dot(e_c.astype(BF16), vcn) + _dot(e_l.astype(BF16), vln)) / den
            o_ref[:, h * HD:(h + 1) * HD] = o.astype(o_ref.dtype)


def _lat_attn(pm, sink, cos_t, sin_t, cache_k, cache_v, layer, yb):
    qw, kw = Q_HEADS * HD, KV_HEADS * HD
    half = TS // 2
    nhb = M // half
    nhs = DEC_SEQ // half

    def main(w, cb):
        return pl.BlockSpec((TS, w), lambda ts: (NT_P + ts, cb))

    def prev(cb):
        return pl.BlockSpec((half, kw), lambda ts: (2 * (NT_P + ts) - 1, cb))

    def nxt(cb):
        return pl.BlockSpec((half, kw), lambda ts: (jnp.minimum(2 * (NT_P + ts) + 2, nhb - 1), cb))

    tab_m = pl.BlockSpec((TS, HD), lambda ts: (ts % TPS, 0))
    tab_p = pl.BlockSpec((half, HD), lambda ts: (jnp.maximum(2 * (ts % TPS) - 1, 0), 0))
    tab_n = pl.BlockSpec((half, HD), lambda ts: (jnp.minimum(2 * (ts % TPS) + 2, nhs - 1), 0))
    ctx = pl.BlockSpec((None, None, PAST, KV_HEADS, HD), lambda ts: (ts // TPS, layer, 0, 0, 0))
    return pl.pallas_call(
        _lat_attn_kernel,
        grid=(NT_S,),
        in_specs=[pl.BlockSpec(memory_space=pltpu.SMEM),
                  main(qw, C_Q // qw),
                  main(kw, C_K // kw), prev(C_K // kw), nxt(C_K // kw),
                  main(kw, C_V // kw), prev(C_V // kw), nxt(C_V // kw),
                  tab_m, tab_p, tab_n, tab_m, tab_p, tab_n, ctx, ctx,
                  pl.BlockSpec(memory_space=pl.ANY)],
        out_specs=pl.BlockSpec((TS, qw), lambda ts: (NT_P + ts, 0)),
        out_shape=jax.ShapeDtypeStruct((M, qw), BF16),
        input_output_aliases={16: 0},
        compiler_params=_params(("parallel",)),
        name="lat_attn",
    )(sink, pm, pm, pm, pm, pm, pm, pm, cos_t, cos_t, cos_t, sin_t, sin_t, sin_t,
      cache_k, cache_v, yb)


def _rope_tables():
    quarter = HD // 4
    inv = ROPE_THETA ** (-jnp.arange(quarter, dtype=F32) / quarter)
    t = jnp.arange(DEC_SEQ)
    row = (t // GRID_W).astype(F32)
    col = (t % GRID_W).astype(F32)
    ar = row[:, None] * inv[None, :]
    ac = col[:, None] * inv[None, :]
    cos_t = jnp.concatenate([jnp.cos(ar), jnp.cos(ar), jnp.cos(ac), jnp.cos(ac)], axis=-1)
    sin_t = jnp.concatenate([-jnp.sin(ar), jnp.sin(ar), -jnp.sin(ac), jnp.sin(ac)], axis=-1)
    return cos_t, sin_t


def _scan_tile(reverse):
    i = pl.program_id(0)
    return (NT - 1 - i) if reverse else i


def _tri_masks(reverse):
    ii = lax.broadcasted_iota(jnp.int32, (CHUNK, CHUNK), 0)
    jj = lax.broadcasted_iota(jnp.int32, (CHUNK, CHUNK), 1)
    incl = (jj >= ii) if reverse else (jj <= ii)
    strict = (jj > ii) if reverse else (jj < ii)
    return incl, strict


def _gated_rms_out(o, gate, nw):
    outs = []
    for h in range(4):
        lanes = slice(h * 128, (h + 1) * 128)
        outs.append(_rms_rows(o[:, lanes], nw) * _silu(gate[:, lanes]))
    return jnp.concatenate(outs, axis=1)


def _dot_hp_all(xs, ys):
    hh = [_dot(x[0], y[0]) for x, y in zip(xs, ys)]
    hl = [_dot(x[0], y[1]) for x, y in zip(xs, ys)]
    lh = [_dot(x[1], y[0]) for x, y in zip(xs, ys)]
    return [a + (b + c) for a, b, c in zip(hh, hl, lh)]


def _inv_unit_tri_all(lms):
    ii = lax.broadcasted_iota(jnp.int32, (CHUNK, CHUNK), 0)
    jj = lax.broadcasted_iota(jnp.int32, (CHUNK, CHUNK), 1)
    eye = jnp.where(ii == jj, 1.0, 0.0)
    ps = [eye - lm for lm in lms]
    mbs = [lm.astype(BF16) for lm in lms]
    mks = [_dot(m, m) for m in mbs]
    span = 4
    while span <= CHUNK:
        mbs = [m.astype(BF16) for m in mks]
        incs = [_dot(p.astype(BF16), m) for p, m in zip(ps, mbs)]
        if span < CHUNK:
            mks = [_dot(m, m) for m in mbs]
        ps = [p + inc for p, inc in zip(ps, incs)]
        span *= 2
    lts = _dot_hp_all([_split2(lm) for lm in lms], [_split2(t) for t in ps])
    rs = [(eye - t) - lt for t, lt in zip(ps, lts)]
    return [t + _dot(t.astype(BF16), r.astype(BF16)) for t, r in zip(ps, rs)]


def _dn_kernel(reverse, has_prev, *refs):
    if has_prev:
        qkv_ref, sm_ref, alog_ref, dtb_ref, s0_ref, oprev_ref, z_ref, nw_ref, o_ref, st_ref = refs[:10]
        scratch = refs[10:]
    else:
        x_ref, xp_ref, xn_ref, sm_ref, cw_ref, alog_ref, dtb_ref, s0_ref, o_ref, st_ref, qkv_out_ref = refs[:11]
        scratch = refs[11:]
    (s_scr, q_scr, k_scr, v_scr, b_scr, gcc_scr, gcr_scr, u_scr, w_scr, qe_scr, kd_scr, a_scr, o_scr) = scratch
    t = _scan_tile(reverse)
    is_p, first, last, _, _ = _tile_info(t)
    d = 1 if reverse else 0
    nchunk = TS // CHUNK
    e_last = 0 if reverse else CHUNK - 1

    @pl.when(last if reverse else first)
    def _():
        s_scr[...] = jnp.where(is_p, 0.0, s0_ref[...])

    if has_prev:
        q_scr[...] = qkv_ref[:, 0:DN_W]
        k_scr[...] = qkv_ref[:, DN_W:2 * DN_W]
        v_scr[...] = qkv_ref[:, 2 * DN_W:3 * DN_W]
    else:
        x = x_ref[...]
        ext = jnp.concatenate([jnp.where(first, 0.0, xp_ref[...]), x, jnp.where(last, 0.0, xn_ref[...])], axis=0)
        cw = cw_ref[...]
        ext_rows = TS + 2 * HALO
        conv = None
        for j in range(DN_CONV):
            off = j - DN_CONV // 2
            rows = ext if off == 0 else pltpu.roll(ext, (ext_rows - off) % ext_rows, 0)
            term = rows[HALO:HALO + TS] * cw[j:j + 1, :]
            conv = term if conv is None else conv + term
        act = _silu(conv)
        for h in range(DN_HEADS):
            lanes = slice(h * 128, (h + 1) * 128)
            qh = act[:, h * 128:(h + 1) * 128]
            kh = act[:, DN_W + h * 128:DN_W + (h + 1) * 128]
            q_scr[:, lanes] = qh * lax.rsqrt(jnp.sum(qh * qh, axis=-1, keepdims=True) + EPS) * (DN_DK ** -0.5)
            k_scr[:, lanes] = kh * lax.rsqrt(jnp.sum(kh * kh, axis=-1, keepdims=True) + EPS)
        v_scr[...] = act[:, 2 * DN_W:3 * DN_W]
        qkv_out_ref[:, 0:DN_W] = q_scr[...]
        qkv_out_ref[:, DN_W:2 * DN_W] = k_scr[...]
        qkv_out_ref[:, 2 * DN_W:3 * DN_W] = v_scr[...]
    sm = sm_ref[...]
    b_scr[...] = jax.nn.sigmoid(sm)
    g = -jnp.exp(alog_ref[...]) * _softplus(sm + dtb_ref[...])

    ti = lax.broadcasted_iota(jnp.int32, (TS, TS), 0)
    tj = lax.broadcasted_iota(jnp.int32, (TS, TS), 1)
    same = (ti // CHUNK) == (tj // CHUNK)
    lower = jnp.where(jnp.logical_and(same, tj <= ti), 1.0, 0.0).astype(BF16)
    upper = jnp.where(jnp.logical_and(same, tj >= ti), 1.0, 0.0).astype(BF16)
    tri, tri_t = (upper, lower) if reverse else (lower, upper)
    g1, g2, g3 = _split3(g)
    gcc_scr[...] = _dot(tri, g1) + _dot(tri, g2) + _dot(tri, g3)
    gcr = _dot_ta(g1, tri_t) + _dot_ta(g2, tri_t) + _dot_ta(g3, tri_t)
    for c in range(nchunk):
        gcr_scr[c] = gcr[:, c * CHUNK:(c + 1) * CHUNK]

    incl, strict = _tri_masks(reverse)

    def prep_body(pi, carry):
        rows, gblk, gcr_c, bt = {}, {}, {}, {}
        for cc in range(PREP_CHUNKS):
            c = pi * PREP_CHUNKS + cc
            rows[cc] = pl.ds(c * CHUNK, CHUNK)
            gblk[cc] = gcc_scr[rows[cc], :]
            gcr_c[cc] = gcr_scr[c]
            bt[cc] = b_scr[rows[cc], :]
        info = []
        for cc in range(PREP_CHUNKS):
            for h in range(DN_HEADS):
                lanes = slice(h * 128, (h + 1) * 128)
                r = SM_A + d * DN_HEADS + h
                gcol = gblk[cc][:, r:r + 1]
                grow = gcr_c[cc][r:r + 1, :]
                glast = gblk[cc][e_last:e_last + 1, r:r + 1]
                beta = bt[cc][:, SM_B + d * DN_HEADS + h:SM_B + d * DN_HEADS + h + 1]
                qc, kc, vc = q_scr[rows[cc], lanes], k_scr[rows[cc], lanes], v_scr[rows[cc], lanes]
                decay = jnp.exp(jnp.where(incl, gcol - grow, -1e30))
                kb = kc * beta
                eg = jnp.exp(gcol)
                qe_scr[rows[cc], lanes] = (qc * eg).astype(BF16)
                kd_scr[rows[cc], lanes] = (kc * jnp.exp(glast - gcol)).astype(BF16)
                info.append(dict(cc=cc, h=h, lanes=lanes, decay=decay, kbb=kb.astype(BF16), kcb=kc.astype(BF16),
                                 qcb=qc.astype(BF16), vbb=(vc * beta).astype(BF16), kgb=(kb * eg).astype(BF16)))
        kk = [_dot_tb(p["kbb"], p["kcb"]) for p in info]
        qk = [_dot_tb(p["qcb"], p["kcb"]) for p in info]
        for p, m in zip(info, qk):
            a_scr[rows[p["cc"]], p["h"] * CHUNK:(p["h"] + 1) * CHUNK] = (m * p["decay"]).astype(BF16)
        lms = [jnp.where(strict, m * p["decay"], 0.0) for p, m in zip(info, kk)]
        tinvs = [tv.astype(BF16) for tv in _inv_unit_tri_all(lms)]
        us = [_dot(tv, p["vbb"]) for tv, p in zip(tinvs, info)]
        ws = [_dot(tv, p["kgb"]) for tv, p in zip(tinvs, info)]
        for p, u, w in zip(info, us, ws):
            u_scr[rows[p["cc"]], p["lanes"]] = u
            w_scr[rows[p["cc"]], p["lanes"]] = w.astype(BF16)
        return carry

    for pi in range(nchunk // PREP_CHUNKS):
        prep_body(pi, 0)

    def chunk_body(ci, carry):
        c = (nchunk - 1 - ci) if reverse else ci
        rows = pl.ds(c * CHUNK, CHUNK)
        glast_row = gcc_scr[pl.ds(c * CHUNK + e_last, 1), :]
        heads = range(DN_HEADS)
        lanes = [slice(h * 128, (h + 1) * 128) for h in heads]
        s_old = [s_scr[h] for h in heads]
        sb = [s.astype(BF16) for s in s_old]
        ws = [_dot(w_scr[rows, lanes[h]], sb[h]) for h in heads]
        qs = [_dot(qe_scr[rows, lanes[h]], sb[h]) for h in heads]
        vnb = [(u_scr[rows, lanes[h]] - ws[h]).astype(BF16) for h in heads]
        av = [_dot(a_scr[rows, h * CHUNK:(h + 1) * CHUNK], vnb[h]) for h in heads]
        up = [_dot_ta(kd_scr[rows, lanes[h]], vnb[h]) for h in heads]
        for h in heads:
            r = SM_A + d * DN_HEADS + h
            o_scr[rows, lanes[h]] = qs[h] + av[h]
            s_scr[h] = jnp.exp(glast_row[:, r:r + 1]) * s_old[h] + up[h]
        return carry

    for ci in range(nchunk):
        chunk_body(ci, 0)

    if has_prev:
        o_ref[...] = _gated_rms_out(o_scr[...] + oprev_ref[...], z_ref[...], nw_ref[...]).astype(o_ref.dtype)
    else:
        o_ref[...] = o_scr[...]

    @pl.when(is_p)
    def _():
        st_ref[...] = s_scr[...]


def _seq_of_tile(t):
    return jnp.maximum(t - NT_P, 0) // TPS


def _dn_call(pm, ps, reverse, layer, conv_w, alog, dtb, state_delta, prev=None, norm_w=None):
    d = 1 if reverse else 0
    tmap = (lambda i: NT - 1 - i) if reverse else (lambda i: i)
    width = 3 * DN_W
    cb = C_DNQKV // width
    r = TS // HALO
    nblk = M // HALO
    small = [pl.BlockSpec((TS, 128), lambda i: (tmap(i), 0))]
    params = [pl.BlockSpec((1, 128), lambda i: (0, 0)),
              pl.BlockSpec((1, 128), lambda i: (0, 0)),
              pl.BlockSpec((None, None, None, DN_HEADS, DN_DK, 128), lambda i: (_seq_of_tile(tmap(i)), layer, d, 0, 0, 0))]
    tile_w = lambda w: pl.BlockSpec((TS, w), lambda i: (tmap(i), 0))
    has_prev = prev is not None
    out_specs = [tile_w(DN_W),
                 pl.BlockSpec((None, DN_HEADS, DN_DK, 128), lambda i: (jnp.minimum(tmap(i), NT_P - 1), 0, 0, 0))]
    out_shape = [jax.ShapeDtypeStruct((M, DN_W), BF16 if has_prev else F32),
                 jax.ShapeDtypeStruct((BATCH, DN_HEADS, DN_DK, 128), F32)]
    if has_prev:
        o_first, qkv = prev
        in_specs = [tile_w(width)] + small + params + [
            tile_w(DN_W), pl.BlockSpec((TS, DN_W), lambda i: (tmap(i), C_DNZ // DN_W)), pl.BlockSpec((1, 128), lambda i: (0, 0))]
        args = [qkv, ps, alog, dtb, state_delta, o_first, pm, norm_w.reshape(1, 128)]
    else:
        in_specs = [pl.BlockSpec((TS, width), lambda i: (tmap(i), cb)),
                    pl.BlockSpec((HALO, width), lambda i: (jnp.maximum(tmap(i) * r - 1, 0), cb)),
                    pl.BlockSpec((HALO, width), lambda i: (jnp.minimum((tmap(i) + 1) * r, nblk - 1), cb))] + small + [
            pl.BlockSpec((DN_CONV, width), lambda i: (0, 0))] + params
        args = [pm, pm, pm, ps, conv_w, alog, dtb, state_delta]
        out_specs.append(tile_w(width))
        out_shape.append(jax.ShapeDtypeStruct((M, width), F32))
    return pl.pallas_call(
        functools.partial(_dn_kernel, reverse, has_prev),
        grid=(NT,),
        in_specs=in_specs,
        out_specs=out_specs,
        out_shape=out_shape,
        scratch_shapes=[pltpu.VMEM((DN_HEADS, DN_DK, 128), F32),
                        pltpu.VMEM((TS, DN_W), F32), pltpu.VMEM((TS, DN_W), F32), pltpu.VMEM((TS, DN_W), F32),
                        pltpu.VMEM((TS, 128), F32), pltpu.VMEM((TS, 128), F32),
                        pltpu.VMEM((TS // CHUNK, 128, CHUNK), F32),
                        pltpu.VMEM((TS, DN_W), F32), pltpu.VMEM((TS, DN_W), BF16),
                        pltpu.VMEM((TS, DN_W), BF16), pltpu.VMEM((TS, DN_W), BF16),
                        pltpu.VMEM((TS, DN_HEADS * CHUNK), BF16),
                        pltpu.VMEM((TS, DN_W), F32)],
        compiler_params=_params(("arbitrary",)),
        name="deltanet_bwd" if reverse else "deltanet_fwd",
    )(*args)


def _gla_kernel(reverse, has_prev, q_ref, k_ref, v_ref, sm_ref, w2_ref, b2_ref, s0_ref, *rest):
    if has_prev:
        oprev_ref, r_ref, nw_ref, o_ref, st_ref, s_scr, o_scr = rest
    else:
        o_ref, st_ref, s_scr, o_scr = rest
    t = _scan_tile(reverse)
    is_p, first, last, _, _ = _tile_info(t)

    @pl.when(last if reverse else first)
    def _():
        s_scr[...] = jnp.where(is_p, 0.0, s0_ref[...])

    logits = _dot(sm_ref[...].astype(BF16), w2_ref[...]) + b2_ref[...]
    gk = -_softplus(-logits) / GLA_NORMALIZER

    ti = lax.broadcasted_iota(jnp.int32, (TS, TS), 0)
    tj = lax.broadcasted_iota(jnp.int32, (TS, TS), 1)
    same = (ti // CHUNK) == (tj // CHUNK)
    tri = jnp.where(jnp.logical_and(same, (tj >= ti) if reverse else (tj <= ti)), 1.0, 0.0).astype(BF16)
    g1, g2, g3 = _split3(gk)
    gc = _dot(tri, g1) + _dot(tri, g2) + _dot(tri, g3)

    incl, _ = _tri_masks(reverse)
    nchunk = TS // CHUNK
    e_last = 0 if reverse else CHUNK - 1
    eg = jnp.exp(gc)
    q_all = q_ref[...] * (GLA_DK ** -0.5)
    k_all = k_ref[...]
    qg_all = (q_all * eg).astype(BF16)
    kg_all = (k_all * jnp.exp(-gc)).astype(BF16)
    v_all = v_ref[...].astype(BF16)

    probs = [(c, h) for c in range(nchunk) for h in range(GLA_HEADS)]
    rws = {c: slice(c * CHUNK, (c + 1) * CHUNK) for c in range(nchunk)}
    kls = {h: slice(h * GLA_DK, (h + 1) * GLA_DK) for h in range(GLA_HEADS)}
    vls = {h: slice(h * GLA_DV, (h + 1) * GLA_DV) for h in range(GLA_HEADS)}
    glast = {c: gc[c * CHUNK + e_last:c * CHUNK + e_last + 1, :] for c in range(nchunk)}
    kd = {c: (k_all[rws[c], :] * jnp.exp(glast[c] - gc[rws[c], :])).astype(BF16) for c in range(nchunk)}
    a = {p: jnp.where(incl, _dot_tb(qg_all[rws[p[0]], kls[p[1]]], kg_all[rws[p[0]], kls[p[1]]]), 0.0).astype(BF16)
         for p in probs}
    av = {p: _dot(a[p], v_all[rws[p[0]], vls[p[1]]]) for p in probs}
    up = {p: _dot_ta(v_all[rws[p[0]], vls[p[1]]], kd[p[0]][:, kls[p[1]]]) for p in probs}
    order = list(reversed(range(nchunk))) if reverse else list(range(nchunk))
    s_in = {}
    for h in range(GLA_HEADS):
        s = s_scr[h]
        for c in order:
            s_in[(c, h)] = s.astype(BF16)
            s = jnp.exp(glast[c][:, kls[h]]) * s + up[(c, h)]
        s_scr[h] = s
    for p in probs:
        c, h = p
        o_scr[rws[c], vls[h]] = _dot_tb(qg_all[rws[c], kls[h]], s_in[p]) + av[p]

    if has_prev:
        o_ref[...] = _gated_rms_out(o_scr[...] + oprev_ref[...], r_ref[...], nw_ref[...]).astype(o_ref.dtype)
    else:
        o_ref[...] = o_scr[...]

    @pl.when(is_p)
    def _():
        st_ref[...] = s_scr[...]


def _gla_call(pm, ps, reverse, w2pad, b2, s0t, prev=None, norm_w=None):
    tmap = (lambda i: NT - 1 - i) if reverse else (lambda i: i)
    qkw = GLA_HEADS * GLA_DK
    vw = GLA_HEADS * GLA_DV
    in_specs = [
        pl.BlockSpec((TS, qkw), lambda i: (tmap(i), C_GQ // qkw)),
        pl.BlockSpec((TS, qkw), lambda i: (tmap(i), C_GK // qkw)),
        pl.BlockSpec((TS, vw), lambda i: (tmap(i), C_GV // vw)),
        pl.BlockSpec((TS, 128), lambda i: (tmap(i), 0)),
        pl.BlockSpec((128, qkw), lambda i: (0, 0)),
        pl.BlockSpec((1, qkw), lambda i: (0, 0)),
        pl.BlockSpec((None, GLA_HEADS, GLA_DV, GLA_DK), lambda i: (_seq_of_tile(tmap(i)), 0, 0, 0)),
    ]
    args = [pm, pm, pm, ps, w2pad, b2, s0t]
    has_prev = prev is not None
    if has_prev:
        in_specs += [pl.BlockSpec((TS, vw), lambda i: (tmap(i), 0)),
                     pl.BlockSpec((TS, vw), lambda i: (tmap(i), C_GR // vw)),
                     pl.BlockSpec((1, 128), lambda i: (0, 0))]
        args += [prev, pm, norm_w.reshape(1, 128)]
    out_dtype = BF16 if has_prev else F32
    return pl.pallas_call(
        functools.partial(_gla_kernel, reverse, has_prev),
        grid=(NT,),
        in_specs=in_specs,
        out_specs=[pl.BlockSpec((TS, vw), lambda i: (tmap(i), 0)),
                   pl.BlockSpec((None, GLA_HEADS, GLA_DV, GLA_DK), lambda i: (jnp.minimum(tmap(i), NT_P - 1), 0, 0, 0))],
        out_shape=[jax.ShapeDtypeStruct((M, vw), out_dtype),
                   jax.ShapeDtypeStruct((BATCH, GLA_HEADS, GLA_DV, GLA_DK), F32)],
        scratch_shapes=[pltpu.VMEM((GLA_HEADS, GLA_DV, GLA_DK), F32),
                        pltpu.VMEM((TS, vw), F32)],
        compiler_params=_params(("arbitrary",)),
        name="gla_bwd" if reverse else "gla_fwd",
    )(*args)


MOD_SH1, MOD_SC1, MOD_GT1, MOD_SH2, MOD_SC2, MOD_GT2 = range(N_MOD)


def kernel(x_prompt, x_sample, cache_k, cache_v, state_delta, state_gla, c, c_ctx, w_mod, b_mod, norm1, w_in, pool_w, pool_scale, attn_sink, dn_conv, dn_a_log, dn_dt_bias, dn_norm, gla_w2, gla_b2, gla_norm, w_br_pool, w_br_attn, w_br_delta, w_br_gla, w_out, norm2, w_gate, w_up, w_down, norm_f):
    x = (x_prompt.reshape(MP, D), x_sample.reshape(MS, D))
    cond8 = jnp.concatenate([c_ctx[None, :], c, jnp.zeros((8 - 1 - DEC_BATCH, D), F32)], axis=0)
    mod = _modulation(cond8, w_mod, b_mod).reshape(DEPTH, 8, N_MOD, 1, D)
    cos_t, sin_t = _rope_tables()
    s_gla_t = jnp.swapaxes(state_gla, -1, -2)
    w_out_bf16 = w_out.astype(BF16)
    w_brs = (w_br_pool, w_br_attn, w_br_delta, w_br_gla)
    w_in_t = jnp.swapaxes(w_in, 1, 2)
    gates_t = _gate_weights(w_in_t)

    new_k = jnp.zeros((BATCH, DEPTH, SEQ, KV_HEADS, HD), F32)
    new_v = jnp.zeros((BATCH, DEPTH, SEQ, KV_HEADS, HD), F32)
    yb = jnp.zeros((M, Q_HEADS * HD), BF16)
    sds, sgs = [], []
    for l in range(DEPTH):
        h, ps, *joined = _prenorm(x, norm1, mod, w_in_t, l, MOD_SC1, MOD_SH1)
        if joined:
            x = joined[0]
        pm = _mix_proj(h, w_in_t, l)

        ya = _pool(pm, pool_w[l], pool_scale[l])
        yb, new_k, new_v = _ctx_attn(pm, attn_sink[l], new_k, new_v, l, yb)
        yb = _lat_attn(pm, attn_sink[l], cos_t, sin_t, cache_k, cache_v, l, yb)

        alog = jnp.zeros((1, 128), F32).at[0, SM_A:SM_A + 8].set(dn_a_log[l].reshape(8))
        dtb = jnp.zeros((1, 128), F32).at[0, SM_A:SM_A + 8].set(dn_dt_bias[l].reshape(8))
        o_b, sd_b, dn_qkv = _dn_call(pm, ps, True, l, dn_conv[l], alog, dtb, state_delta)
        yc, sd_f = _dn_call(pm, ps, False, l, dn_conv[l], alog, dtb, state_delta, prev=(o_b, dn_qkv), norm_w=dn_norm[l])

        w2pad = [jnp.zeros((128, GLA_HEADS * GLA_DK), F32).at[SM_LR + o * GLA_RANK:SM_LR + (o + 1) * GLA_RANK].set(gla_w2[l, o]).astype(BF16)
                 for o in range(2)]
        g_b, sg_b = _gla_call(pm, ps, True, w2pad[1], gla_b2[l, 1].reshape(1, -1), s_gla_t[:, l, 1])
        yd, sg_f = _gla_call(pm, ps, False, w2pad[0], gla_b2[l, 0].reshape(1, -1), s_gla_t[:, l, 0], prev=g_b, norm_w=gla_norm[l])

        merged = _merge(h, gates_t, (ya, yb, yc, yd), w_brs, l)
        x, h2 = _out_proj(merged, w_out_bf16, x, mod, norm2, l, MOD_GT1, MOD_SC2, MOD_SH2)
        act = _ffn_up(h2, w_gate, w_up, l)
        x = _ffn_down(act, w_down, x, mod, l, MOD_GT2)

        sds.append(jnp.stack([sd_f, sd_b], axis=1))
        sgs.append(jnp.swapaxes(jnp.stack([sg_f, sg_b], axis=1), -1, -2))

    y_ctx, y_lat = _finalnorm(x, norm_f)
    return (y_ctx.reshape(BATCH, SEQ, D), y_lat.reshape(DEC_BATCH, DEC_SEQ, D),
            new_k, new_v, jnp.stack(sds, axis=1), jnp.stack(sgs, axis=1))
```

```python
import functools

import jax
import jax.numpy as jnp
from jax import lax
from jax.experimental import pallas as pl
from jax.experimental.pallas import tpu as pltpu

F32 = jnp.float32
BF16 = jnp.bfloat16

D = 2048
BATCH = 16
SEQ = 256
DEPTH = 4
DEC_BATCH = 4
DEC_SEQ = 2048
PAST = 256
GRID_W = 64
HD = 128
ROPE_THETA = 10000.0
EPS = 1e-6
N_MOD = 6
POOL_WINDOWS = (2, 4, 8, 16)
POOL_W = 512
Q_HEADS = 8
KV_HEADS = 2
GROUP = 4
ATTN_WINDOW = 128
DN_HEADS = 4
DN_DK = 128
DN_W = 512
DN_CONV = 4
GLA_HEADS = 4
GLA_DK = 64
GLA_DV = 128
GLA_RANK = 16
GLA_NORMALIZER = 16.0
FFN = 5632
CHUNK = 64
PREP_CHUNKS = 4

MP = BATCH * SEQ
MS = DEC_BATCH * DEC_SEQ
M = MP + MS
TS = 256
NT_P = MP // TS
TPS = DEC_SEQ // TS
NT_S = MS // TS
NT = NT_P + NT_S
HALO = 8
assert max(POOL_WINDOWS) // 2 <= HALO and DN_CONV // 2 <= HALO
assert ATTN_WINDOW <= TS // 2

C_Q, C_POOL, C_DNQKV, C_K, C_V, C_DNZ = 0, 1024, 1536, 3072, 3328, 3584
C_GQ, C_GK, C_GV, C_GR = 4096, 4352, 4608, 5120
MIXW = 5632
MIX_TN = 512
MIX_SRC = (1, 2, 0, 4, 5, 6, 3, 7, 8, 9, 10)
MIX_ALIGNED = 8
MIX_SHIFT = 16
SMALL_COL_A = 4096
SMALL_COL_LR = 5648
GATE_COL = 5680
IN_WIDTH = GATE_COL + 4 * D
SM_A, SM_B, SM_LR = 0, 8, 16

VMEM_LIMIT = 56 * 1024 * 1024


def _params(sem):
    return pltpu.CompilerParams(dimension_semantics=sem, vmem_limit_bytes=VMEM_LIMIT)


def _cond_row(row0):
    return jnp.where(row0 < MP, 0, 1 + (row0 - MP) // DEC_SEQ)


def _tile_info(t):
    is_p = t < NT_P
    s = jnp.maximum(t - NT_P, 0)
    j = s % TPS
    first = jnp.logical_or(is_p, j == 0)
    last = jnp.logical_or(is_p, j == TPS - 1)
    base = jnp.where(is_p, 0, j * TS)
    seq_len = jnp.where(is_p, SEQ, DEC_SEQ)
    return is_p, first, last, base, seq_len


def _silu(x):
    return x * jax.nn.sigmoid(x)


def _softplus(x):
    return jnp.maximum(x, 0.0) + jnp.log(1.0 + jnp.exp(-jnp.abs(x)))


def _dot(a, b):
    return jnp.dot(a, b, preferred_element_type=F32)


def _dot_tb(a, b):
    return lax.dot_general(a, b, (((1,), (1,)), ((), ())), preferred_element_type=F32)


def _dot_ta(a, b):
    return lax.dot_general(a, b, (((0,), (0,)), ((), ())), preferred_element_type=F32)


def _split2(x):
    hi = x.astype(BF16)
    lo = (x - hi.astype(F32)).astype(BF16)
    return hi, lo


def _split3(x):
    p1 = x.astype(BF16)
    r = x - p1.astype(F32)
    p2 = r.astype(BF16)
    p3 = (r - p2.astype(F32)).astype(BF16)
    return p1, p2, p3


def _mod_kernel(c_ref, w_ref, b_ref, o_ref):
    a = _silu(c_ref[...]).astype(BF16)
    o_ref[...] = _dot(a, w_ref[...].astype(BF16)) + b_ref[...]


def _modulation(cond8, w_mod, b_mod):
    tn = 1024
    n = N_MOD * D
    return pl.pallas_call(
        _mod_kernel,
        grid=(DEPTH, n // tn),
        in_specs=[pl.BlockSpec((8, D), lambda l, j: (0, 0)),
                  pl.BlockSpec((None, D, tn), lambda l, j: (l, 0, j)),
                  pl.BlockSpec((None, 1, tn), lambda l, j: (l, 0, j))],
        out_specs=pl.BlockSpec((None, 8, tn), lambda l, j: (l, 0, j)),
        out_shape=jax.ShapeDtypeStruct((DEPTH, 8, n), F32),
        compiler_params=_params(("parallel", "parallel")),
        name="modulation",
    )(cond8, w_mod, b_mod.reshape(DEPTH, 1, n))


def _rms_rows(x, w):
    ms = jnp.mean(x * x, axis=-1, keepdims=True)
    return x * lax.rsqrt(ms + EPS) * w


def _prenorm_kernel(nctx, *refs):
    if nctx is None:
        x_ref, nw_ref, sc_ref, sh_ref, wa_ref, wl_ref, h_ref, ps_ref, wb_scr = refs
        x = x_ref[...]
    else:
        xp_ref, xs_ref, nw_ref, sc_ref, sh_ref, wa_ref, wl_ref, h_ref, ps_ref, x_out_ref, wb_scr = refs
        x = jnp.where(pl.program_id(0) < nctx, xp_ref[...], xs_ref[...])
        x_out_ref[...] = x

    @pl.when(pl.program_id(0) == 0)
    def _():
        pad = jnp.zeros((128 - SM_LR - 2 * GLA_RANK, D), F32)
        wb_scr[...] = jnp.concatenate([wa_ref[0], wl_ref[0], pad], axis=0).astype(BF16)

    y = _rms_rows(x, nw_ref[...])
    h = (y * (1.0 + sc_ref[...]) + sh_ref[...]).astype(h_ref.dtype)
    h_ref[...] = h
    ps_ref[...] = _dot_tb(h, wb_scr[...])


def _mod_spec(layer, chunk, tm, tn=D, row_axis=0, col_axis=None):
    def index(*g):
        col = 0 if col_axis is None else g[col_axis]
        return (layer, _cond_row(g[row_axis] * tm), chunk, 0, col)
    return pl.BlockSpec((None, None, None, 1, tn), index)


def _prenorm(x, norm_w, mod, w_in_t, layer, sc_chunk, sh_chunk):
    tm = 512
    row = pl.BlockSpec((tm, D), lambda i: (i, 0))
    split = isinstance(x, tuple)
    nctx = MP // tm
    if split:
        x_specs = [pl.BlockSpec((tm, D), lambda i: (jnp.minimum(i, nctx - 1), 0)),
                   pl.BlockSpec((tm, D), lambda i: (jnp.maximum(i - nctx, 0), 0))]
        xs = list(x)
    else:
        x_specs, xs = [row], [x]
    out_specs = [row, pl.BlockSpec((tm, 128), lambda i: (i, 0))]
    out_shape = [jax.ShapeDtypeStruct((M, D), BF16), jax.ShapeDtypeStruct((M, 128), F32)]
    if split:
        out_specs.append(row)
        out_shape.append(jax.ShapeDtypeStruct((M, D), F32))
    return pl.pallas_call(
        functools.partial(_prenorm_kernel, nctx if split else None),
        grid=(M // tm,),
        in_specs=x_specs + [pl.BlockSpec((None, 1, D), lambda i: (layer, 0, 0)),
                            _mod_spec(layer, sc_chunk, tm), _mod_spec(layer, sh_chunk, tm),
                            _wt_rows(layer, SM_LR, lambda i: SMALL_COL_A),
                            _wt_rows(layer, 2 * GLA_RANK, lambda i: SMALL_COL_LR)],
        out_specs=out_specs,
        out_shape=out_shape,
        scratch_shapes=[pltpu.VMEM((128, D), BF16)],
        compiler_params=_params(("arbitrary",)),
        name="prenorm",
    )(*xs, norm_w.reshape(DEPTH, 1, D), mod, mod, w_in_t, w_in_t)


def _finalnorm_kernel(tm, x_ref, nw_ref, yp_ref, ys_ref):
    y = _rms_rows(x_ref[...], nw_ref[...])
    is_ctx = pl.program_id(0) < MP // tm

    @pl.when(is_ctx)
    def _():
        yp_ref[...] = y

    @pl.when(jnp.logical_not(is_ctx))
    def _():
        ys_ref[...] = y


def _finalnorm(x, nw):
    tm = 512
    nctx = MP // tm
    return pl.pallas_call(
        functools.partial(_finalnorm_kernel, tm),
        grid=(M // tm,),
        in_specs=[pl.BlockSpec((tm, D), lambda i: (i, 0)),
                  pl.BlockSpec((1, D), lambda i: (0, 0))],
        out_specs=[pl.BlockSpec((tm, D), lambda i: (jnp.minimum(i, nctx - 1), 0)),
                   pl.BlockSpec((tm, D), lambda i: (jnp.maximum(i - nctx, 0), 0))],
        out_shape=[jax.ShapeDtypeStruct((MP, D), F32), jax.ShapeDtypeStruct((MS, D), F32)],
        compiler_params=_params(("arbitrary",)),
        name="finalnorm",
    )(x, nw.reshape(1, D))


def _first_row_block():
    return pl.program_id(1) == 0


def _wt_rows(layer, nrows, row_fn):
    def index(*g):
        row = row_fn(*g)
        return (layer, row if isinstance(row, int) else pl.multiple_of(row, 8), 0)
    return pl.BlockSpec((pl.Element(1), pl.Element(nrows), pl.Element(D)), index)


def _mix_kernel(h_ref, w_ref, o_ref, wb_scr):
    @pl.when(_first_row_block())
    def _():
        wb_scr[...] = w_ref[0].astype(BF16)

    o_ref[...] = _dot_tb(h_ref[...], wb_scr[...])


def _mix_src_row(j):
    src = j
    for out_tile, src_tile in enumerate(MIX_SRC):
        src = jnp.where(j == out_tile, src_tile, src)
    return src * MIX_TN + jnp.where(src >= MIX_ALIGNED, MIX_SHIFT, 0)


def _mix_proj(h, w_in_t, layer):
    tm, tn = 2048, MIX_TN
    return pl.pallas_call(
        _mix_kernel,
        grid=(MIXW // tn, M // tm),
        in_specs=[pl.BlockSpec((tm, D), lambda j, i: (i, 0)),
                  _wt_rows(layer, tn, lambda j, i: _mix_src_row(j))],
        out_specs=pl.BlockSpec((tm, tn), lambda j, i: (i, j)),
        out_shape=jax.ShapeDtypeStruct((M, MIXW), F32),
        scratch_shapes=[pltpu.VMEM((tn, D), BF16)],
        compiler_params=_params(("arbitrary", "arbitrary")),
        name="mix_proj",
    )(h, w_in_t)


def _ffn_up_kernel(h_ref, wg_ref, wu_ref, o_ref, wg_scr, wu_scr):
    @pl.when(_first_row_block())
    def _():
        wg_scr[...] = wg_ref[...].astype(BF16)
        wu_scr[...] = wu_ref[...].astype(BF16)

    h = h_ref[...]
    o_ref[...] = (_silu(_dot(h, wg_scr[...])) * _dot(h, wu_scr[...])).astype(o_ref.dtype)


def _ffn_up(h, w_gate, w_up, layer):
    tm, tn = 2048, 512
    wspec = pl.BlockSpec((None, D, tn), lambda j, i: (layer, 0, j))
    return pl.pallas_call(
        _ffn_up_kernel,
        grid=(FFN // tn, M // tm),
        in_specs=[pl.BlockSpec((tm, D), lambda j, i: (i, 0)), wspec, wspec],
        out_specs=pl.BlockSpec((tm, tn), lambda j, i: (i, j)),
        out_shape=jax.ShapeDtypeStruct((M, FFN), BF16),
        scratch_shapes=[pltpu.VMEM((D, tn), BF16), pltpu.VMEM((D, tn), BF16)],
        compiler_params=_params(("arbitrary", "arbitrary")),
        name="ffn_up",
    )(h, w_gate, w_up)


def _ffn_down_kernel(a_ref, w_ref, x_ref, gt_ref, o_ref, wb_scr):
    @pl.when(_first_row_block())
    def _():
        wb_scr[...] = w_ref[...].astype(BF16)

    o_ref[...] = x_ref[...] + gt_ref[...] * _dot(a_ref[...], wb_scr[...])


def _ffn_down(act, w_down, x, mod, layer, gt_chunk):
    tm, tn = 1024, 512
    return pl.pallas_call(
        _ffn_down_kernel,
        grid=(D // tn, M // tm),
        in_specs=[pl.BlockSpec((tm, FFN), lambda j, i: (i, 0)),
                  pl.BlockSpec((None, FFN, tn), lambda j, i: (layer, 0, j), pipeline_mode=pl.Buffered(1)),
                  pl.BlockSpec((tm, tn), lambda j, i: (i, j)),
                  _mod_spec(layer, gt_chunk, tm, tn, row_axis=1, col_axis=0)],
        out_specs=pl.BlockSpec((tm, tn), lambda j, i: (i, j)),
        out_shape=jax.ShapeDtypeStruct((M, D), F32),
        scratch_shapes=[pltpu.VMEM((FFN, tn), BF16)],
        compiler_params=_params(("arbitrary", "arbitrary")),
        name="ffn_down",
    )(act, w_down, x, mod)


def _gate_cast_kernel(w_ref, o_ref):
    o_ref[...] = w_ref[0].astype(o_ref.dtype)


def _gate_weights(w_in_t):
    tr = 1024
    return pl.pallas_call(
        _gate_cast_kernel,
        grid=(DEPTH, 4 * D // tr),
        in_specs=[pl.BlockSpec((pl.Element(1), pl.Element(tr), pl.Element(D)),
                               lambda l, j: (l, pl.multiple_of(GATE_COL + j * tr, 8), 0))],
        out_specs=pl.BlockSpec((None, tr, D), lambda l, j: (l, j, 0)),
        out_shape=jax.ShapeDtypeStruct((DEPTH, 4 * D, D), BF16),
        compiler_params=_params(("parallel", "parallel")),
        name="gate_cast",
    )(w_in_t)


BR_WIDTHS = (POOL_W, Q_HEADS * HD, DN_W, GLA_HEADS * GLA_DV)


def _merge_kernel(h_ref, *refs):
    g_refs, y_refs, w_refs, o_ref = refs[0:4], refs[4:8], refs[8:12], refs[12]
    h = h_ref[...]
    acc = None
    for b in range(4):
        term = jax.nn.sigmoid(_dot_tb(h, g_refs[b][...])) * _dot(y_refs[b][...], w_refs[b][...].astype(BF16))
        acc = term if acc is None else acc + term
    o_ref[...] = acc.astype(o_ref.dtype)


def _merge(h, gates_t, ys, w_brs, layer):
    tm, tn = 1024, 512
    nj = D // tn
    gate_specs = [pl.BlockSpec((None, tn, D), functools.partial(lambda i, j, b: (layer, b * nj + j, 0), b=b))
                  for b in range(4)]
    y_specs = [pl.BlockSpec((tm, w), lambda i, j: (i, 0)) for w in BR_WIDTHS]
    w_specs = [pl.BlockSpec((None, w, tn), lambda i, j: (layer, 0, j)) for w in BR_WIDTHS]
    return pl.pallas_call(
        _merge_kernel,
        grid=(M // tm, nj),
        in_specs=[pl.BlockSpec((tm, D), lambda i, j: (i, 0))] + gate_specs + y_specs + w_specs,
        out_specs=pl.BlockSpec((tm, tn), lambda i, j: (i, j)),
        out_shape=jax.ShapeDtypeStruct((M, D), BF16),
        compiler_params=_params(("parallel", "arbitrary")),
        name="merge",
    )(h, *([gates_t] * 4), *ys, *w_brs)


def _out_proj_kernel(m_ref, w_ref, x_ref, gt_ref, nw_ref, sc_ref, sh_ref, xo_ref, h_ref):
    xn = x_ref[...] + gt_ref[...] * _dot(m_ref[...], w_ref[...])
    xo_ref[...] = xn
    h_ref[...] = (_rms_rows(xn, nw_ref[...]) * (1.0 + sc_ref[...]) + sh_ref[...]).astype(h_ref.dtype)


def _out_proj(merged, w_out_bf16, x, mod, norm_w, layer, gt_chunk, sc_chunk, sh_chunk):
    tm = 512
    row = pl.BlockSpec((tm, D), lambda i: (i, 0))
    return pl.pallas_call(
        _out_proj_kernel,
        grid=(M // tm,),
        in_specs=[row,
                  pl.BlockSpec((None, D, D), lambda i: (layer, 0, 0), pipeline_mode=pl.Buffered(1)),
                  row, _mod_spec(layer, gt_chunk, tm),
                  pl.BlockSpec((None, 1, D), lambda i: (layer, 0, 0)),
                  _mod_spec(layer, sc_chunk, tm), _mod_spec(layer, sh_chunk, tm)],
        out_specs=[row, row],
        out_shape=[jax.ShapeDtypeStruct((M, D), F32), jax.ShapeDtypeStruct((M, D), BF16)],
        compiler_params=_params(("parallel",)),
        name="out_proj",
    )(merged, w_out_bf16, x, mod, norm_w.reshape(DEPTH, 1, D), mod, mod)


def _pool_kernel(u_ref, up_ref, un_ref, w_ref, sc_ref, o_ref):
    t = pl.program_id(0)
    _, first, last, base, seq_len = _tile_info(t)
    u = u_ref[...]
    prev = jnp.where(first, 0.0, up_ref[...])
    nxt = jnp.where(last, 0.0, un_ref[...])
    ext = jnp.concatenate([prev, u, nxt], axis=0)
    pos = base + lax.broadcasted_iota(jnp.int32, (TS, 1), 0)
    outs = []
    for gi, win in enumerate(POOL_WINDOWS):
        lanes = slice(gi * 128, (gi + 1) * 128)
        p = ext[:, lanes]
        s = 1
        while s < win:
            p = p + pltpu.roll(p, s, 0)
            s *= 2
        d = HALO + win // 2 - 1
        ssum = p[d:d + TS]
        cnt = (jnp.minimum(pos + win // 2, seq_len) - jnp.maximum(pos - win // 2, 0)).astype(F32)
        pg = ssum / cnt - u[:, lanes]
        outs.append(_dot(pg.astype(BF16), w_ref[gi].astype(BF16)))
    o_ref[...] = (jnp.concatenate(outs, axis=1) * sc_ref[...]).astype(o_ref.dtype)


def _halo_specs(width, colblk):
    r = TS // HALO
    nblk = M // HALO
    prev = pl.BlockSpec((HALO, width), lambda t: (jnp.maximum(t * r - 1, 0), colblk))
    nxt = pl.BlockSpec((HALO, width), lambda t: (jnp.minimum((t + 1) * r, nblk - 1), colblk))
    return prev, nxt


def _pool(pm, pool_w, pool_scale):
    cb = C_POOL // POOL_W
    prev, nxt = _halo_specs(POOL_W, cb)
    return pl.pallas_call(
        _pool_kernel,
        grid=(NT,),
        in_specs=[pl.BlockSpec((TS, POOL_W), lambda t: (t, cb)), prev, nxt,
                  pl.BlockSpec((4, 128, 128), lambda t: (0, 0, 0)),
                  pl.BlockSpec((1, POOL_W), lambda t: (0, 0))],
        out_specs=pl.BlockSpec((TS, POOL_W), lambda t: (t, 0)),
        out_shape=jax.ShapeDtypeStruct((M, POOL_W), BF16),
        compiler_params=_params(("parallel",)),
        name="pool",
    )(pm, pm, pm, pool_w, pool_scale.reshape(1, POOL_W))


def _ctx_attn_kernel(sink_ref, q_ref, k_ref, v_ref, nk_in, nv_in, yb_in, o_ref, nk_ref, nv_ref):
    del nk_in, nv_in, yb_in
    q = q_ref[...] * (HD ** -0.5)
    k32 = k_ref[...]
    v32 = v_ref[...]
    k = k32.astype(BF16)
    v = v32.astype(BF16)
    for n in range(KV_HEADS):
        nk_ref[:, n, :] = k32[:, n * HD:(n + 1) * HD]
        nv_ref[:, n, :] = v32[:, n * HD:(n + 1) * HD]
        kn = k[:, n * HD:(n + 1) * HD]
        vn = v[:, n * HD:(n + 1) * HD]
        for g in range(GROUP):
            h = n * GROUP + g
            qh = q[:, h * HD:(h + 1) * HD].astype(BF16)
            s = _dot_tb(qh, kn)
            snk = sink_ref[h]
            m = jnp.maximum(jnp.max(s, axis=-1, keepdims=True), snk)
            e = jnp.exp(s - m)
            den = jnp.sum(e, axis=-1, keepdims=True) + jnp.exp(snk - m)
            o = _dot(e.astype(BF16), vn) / den
            o_ref[:, h * HD:(h + 1) * HD] = o.astype(o_ref.dtype)


def _ctx_attn(pm, sink, new_k, new_v, layer, yb_buf):
    qw, kw = Q_HEADS * HD, KV_HEADS * HD
    cache_blk = pl.BlockSpec((None, None, SEQ, KV_HEADS, HD), lambda t: (t, layer, 0, 0, 0))
    anyspace = pl.BlockSpec(memory_space=pl.ANY)
    return pl.pallas_call(
        _ctx_attn_kernel,
        grid=(NT_P,),
        in_specs=[pl.BlockSpec(memory_space=pltpu.SMEM),
                  pl.BlockSpec((TS, qw), lambda t: (t, C_Q // qw)),
                  pl.BlockSpec((TS, kw), lambda t: (t, C_K // kw)),
                  pl.BlockSpec((TS, kw), lambda t: (t, C_V // kw)),
                  anyspace, anyspace, anyspace],
        out_specs=[pl.BlockSpec((TS, qw), lambda t: (t, 0)), cache_blk, cache_blk],
        out_shape=[jax.ShapeDtypeStruct(yb_buf.shape, yb_buf.dtype),
                   jax.ShapeDtypeStruct(new_k.shape, new_k.dtype), jax.ShapeDtypeStruct(new_v.shape, new_v.dtype)],
        input_output_aliases={4: 1, 5: 2, 6: 0},
        compiler_params=_params(("parallel",)),
        name="ctx_attn",
    )(sink, pm, pm, pm, new_k, new_v, yb_buf)


def _rope(x, cs, sn):
    lane = lax.broadcasted_iota(jnp.int32, x.shape, 1)
    swapped = jnp.where((lane % 64) < 32, pltpu.roll(x, 96, 1), pltpu.roll(x, 32, 1))
    return x * cs + swapped * sn


def _lat_attn_kernel(sink_ref, q_ref, km_ref, kp_ref, kn_ref, vm_ref, vp_ref, vn_ref,
                     cm_ref, cp_ref, cn_ref, sm_ref, sp_ref, sn_ref, kc_ref, vc_ref, yb_ref, o_ref):
    del yb_ref
    ts = pl.program_id(0)
    base = (ts % TPS) * TS
    half = TS // 2
    nk = TS + 2 * half
    q = q_ref[...] * (HD ** -0.5)
    cq, sq = cm_ref[...], sm_ref[...]
    kl = jnp.concatenate([kp_ref[...], km_ref[...], kn_ref[...]], axis=0)
    vl = jnp.concatenate([vp_ref[...], vm_ref[...], vn_ref[...]], axis=0).astype(BF16)
    ck = jnp.concatenate([cp_ref[...], cq, cn_ref[...]], axis=0)
    sk = jnp.concatenate([sp_ref[...], sq, sn_ref[...]], axis=0)
    qpos = base + lax.broadcasted_iota(jnp.int32, (TS, 1), 0)
    kpos = base - half + lax.broadcasted_iota(jnp.int32, (1, nk), 1)
    dist = qpos - kpos
    ok = (jnp.abs(dist) <= ATTN_WINDOW).astype(jnp.int32) * (kpos >= 0).astype(jnp.int32) * (kpos < DEC_SEQ).astype(jnp.int32)
    bias = jnp.where(ok > 0, 0.0, -1e30)
    for n in range(KV_HEADS):
        lanes = slice(n * HD, (n + 1) * HD)
        kr = _rope(kl[:, lanes], ck, sk).astype(BF16)
        vln = vl[:, lanes]
        kcn = kc_ref[:, n, :].astype(BF16)
        vcn = vc_ref[:, n, :].astype(BF16)
        for g in range(GROUP):
            h = n * GROUP + g
            qh = _rope(q[:, h * HD:(h + 1) * HD], cq, sq).astype(BF16)
            s_c = _dot_tb(qh, kcn)
            s_l = _dot_tb(qh, kr) + bias
            snk = sink_ref[h]
            m = jnp.maximum(jnp.maximum(jnp.max(s_c, axis=-1, keepdims=True),
                                        jnp.max(s_l, axis=-1, keepdims=True)), snk)
            e_c = jnp.exp(s_c - m)
            e_l = jnp.exp(s_l - m)
            den = jnp.sum(e_c, axis=-1, keepdims=True) + jnp.sum(e_l, axis=-1, keepdims=True) + jnp.exp(snk - m)
            o = (_dot(e_c.astype(BF16), vcn) + _dot(e_l.astype(BF16), vln)) / den
            o_ref[:, h * HD:(h + 1) * HD] = o.astype(o_ref.dtype)


def _lat_attn(pm, sink, cos_t, sin_t, cache_k, cache_v, layer, yb):
    qw, kw = Q_HEADS * HD, KV_HEADS * HD
    half = TS // 2
    nhb = M // half
    nhs = DEC_SEQ // half

    def main(w, cb):
        return pl.BlockSpec((TS, w), lambda ts: (NT_P + ts, cb))

    def prev(cb):
        return pl.BlockSpec((half, kw), lambda ts: (2 * (NT_P + ts) - 1, cb))

    def nxt(cb):
        return pl.BlockSpec((half, kw), lambda ts: (jnp.minimum(2 * (NT_P + ts) + 2, nhb - 1), cb))

    tab_m = pl.BlockSpec((TS, HD), lambda ts: (ts % TPS, 0))
    tab_p = pl.BlockSpec((half, HD), lambda ts: (jnp.maximum(2 * (ts % TPS) - 1, 0), 0))
    tab_n = pl.BlockSpec((half, HD), lambda ts: (jnp.minimum(2 * (ts % TPS) + 2, nhs - 1), 0))
    ctx = pl.BlockSpec((None, None, PAST, KV_HEADS, HD), lambda ts: (ts // TPS, layer, 0, 0, 0))
    return pl.pallas_call(
        _lat_attn_kernel,
        grid=(NT_S,),
        in_specs=[pl.BlockSpec(memory_space=pltpu.SMEM),
                  main(qw, C_Q // qw),
                  main(kw, C_K // kw), prev(C_K // kw), nxt(C_K // kw),
                  main(kw, C_V // kw), prev(C_V // kw), nxt(C_V // kw),
                  tab_m, tab_p, tab_n, tab_m, tab_p, tab_n, ctx, ctx,
                  pl.BlockSpec(memory_space=pl.ANY)],
        out_specs=pl.BlockSpec((TS, qw), lambda ts: (NT_P + ts, 0)),
        out_shape=jax.ShapeDtypeStruct((M, qw), BF16),
        input_output_aliases={16: 0},
        compiler_params=_params(("parallel",)),
        name="lat_attn",
    )(sink, pm, pm, pm, pm, pm, pm, pm, cos_t, cos_t, cos_t, sin_t, sin_t, sin_t,
      cache_k, cache_v, yb)


def _rope_tables():
    quarter = HD // 4
    inv = ROPE_THETA ** (-jnp.arange(quarter, dtype=F32) / quarter)
    t = jnp.arange(DEC_SEQ)
    row = (t // GRID_W).astype(F32)
    col = (t % GRID_W).astype(F32)
    ar = row[:, None] * inv[None, :]
    ac = col[:, None] * inv[None, :]
    cos_t = jnp.concatenate([jnp.cos(ar), jnp.cos(ar), jnp.cos(ac), jnp.cos(ac)], axis=-1)
    sin_t = jnp.concatenate([-jnp.sin(ar), jnp.sin(ar), -jnp.sin(ac), jnp.sin(ac)], axis=-1)
    return cos_t, sin_t


def _scan_tile(reverse):
    i = pl.program_id(0)
    return (NT - 1 - i) if reverse else i


def _tri_masks(reverse):
    ii = lax.broadcasted_iota(jnp.int32, (CHUNK, CHUNK), 0)
    jj = lax.broadcasted_iota(jnp.int32, (CHUNK, CHUNK), 1)
    incl = (jj >= ii) if reverse else (jj <= ii)
    strict = (jj > ii) if reverse else (jj < ii)
    return incl, strict


def _gated_rms_out(o, gate, nw):
    outs = []
    for h in range(4):
        lanes = slice(h * 128, (h + 1) * 128)
        outs.append(_rms_rows(o[:, lanes], nw) * _silu(gate[:, lanes]))
    return jnp.concatenate(outs, axis=1)


def _dot_hp_all(xs, ys):
    hh = [_dot(x[0], y[0]) for x, y in zip(xs, ys)]
    hl = [_dot(x[0], y[1]) for x, y in zip(xs, ys)]
    lh = [_dot(x[1], y[0]) for x, y in zip(xs, ys)]
    return [a + (b + c) for a, b, c in zip(hh, hl, lh)]


def _inv_unit_tri_all(lms):
    ii = lax.broadcasted_iota(jnp.int32, (CHUNK, CHUNK), 0)
    jj = lax.broadcasted_iota(jnp.int32, (CHUNK, CHUNK), 1)
    eye = jnp.where(ii == jj, 1.0, 0.0)
    ps = [eye - lm for lm in lms]
    mbs = [lm.astype(BF16) for lm in lms]
    mks = [_dot(m, m) for m in mbs]
    span = 4
    while span <= CHUNK:
        mbs = [m.astype(BF16) for m in mks]
        incs = [_dot(p.astype(BF16), m) for p, m in zip(ps, mbs)]
        if span < CHUNK:
            mks = [_dot(m, m) for m in mbs]
        ps = [p + inc for p, inc in zip(ps, incs)]
        span *= 2
    lts = _dot_hp_all([_split2(lm) for lm in lms], [_split2(t) for t in ps])
    rs = [(eye - t) - lt for t, lt in zip(ps, lts)]
    return [t + _dot(t.astype(BF16), r.astype(BF16)) for t, r in zip(ps, rs)]


def _dn_kernel(reverse, has_prev, *refs):
    if has_prev:
        qkv_ref, sm_ref, alog_ref, dtb_ref, s0_ref, oprev_ref, z_ref, nw_ref, o_ref, st_ref = refs[:10]
        scratch = refs[10:]
    else:
        x_ref, xp_ref, xn_ref, sm_ref, cw_ref, alog_ref, dtb_ref, s0_ref, o_ref, st_ref, qkv_out_ref = refs[:11]
        scratch = refs[11:]
    (s_scr, q_scr, k_scr, v_scr, b_scr, gcc_scr, gcr_scr, u_scr, w_scr, qe_scr, kd_scr, a_scr, o_scr) = scratch
    t = _scan_tile(reverse)
    is_p, first, last, _, _ = _tile_info(t)
    d = 1 if reverse else 0
    nchunk = TS // CHUNK
    e_last = 0 if reverse else CHUNK - 1

    @pl.when(last if reverse else first)
    def _():
        s_scr[...] = jnp.where(is_p, 0.0, s0_ref[...])

    if has_prev:
        q_scr[...] = qkv_ref[:, 0:DN_W]
        k_scr[...] = qkv_ref[:, DN_W:2 * DN_W]
        v_scr[...] = qkv_ref[:, 2 * DN_W:3 * DN_W]
    else:
        x = x_ref[...]
        ext = jnp.concatenate([jnp.where(first, 0.0, xp_ref[...]), x, jnp.where(last, 0.0, xn_ref[...])], axis=0)
        cw = cw_ref[...]
        ext_rows = TS + 2 * HALO
        conv = None
        for j in range(DN_CONV):
            off = j - DN_CONV // 2
            rows = ext if off == 0 else pltpu.roll(ext, (ext_rows - off) % ext_rows, 0)
            term = rows[HALO:HALO + TS] * cw[j:j + 1, :]
            conv = term if conv is None else conv + term
        act = _silu(conv)
        for h in range(DN_HEADS):
            lanes = slice(h * 128, (h + 1) * 128)
            qh = act[:, h * 128:(h + 1) * 128]
            kh = act[:, DN_W + h * 128:DN_W + (h + 1) * 128]
            q_scr[:, lanes] = qh * lax.rsqrt(jnp.sum(qh * qh, axis=-1, keepdims=True) + EPS) * (DN_DK ** -0.5)
            k_scr[:, lanes] = kh * lax.rsqrt(jnp.sum(kh * kh, axis=-1, keepdims=True) + EPS)
        v_scr[...] = act[:, 2 * DN_W:3 * DN_W]
        qkv_out_ref[:, 0:DN_W] = q_scr[...]
        qkv_out_ref[:, DN_W:2 * DN_W] = k_scr[...]
        qkv_out_ref[:, 2 * DN_W:3 * DN_W] = v_scr[...]
    sm = sm_ref[...]
    b_scr[...] = jax.nn.sigmoid(sm)
    g = -jnp.exp(alog_ref[...]) * _softplus(sm + dtb_ref[...])

    ti = lax.broadcasted_iota(jnp.int32, (TS, TS), 0)
    tj = lax.broadcasted_iota(jnp.int32, (TS, TS), 1)
    same = (ti // CHUNK) == (tj // CHUNK)
    lower = jnp.where(jnp.logical_and(same, tj <= ti), 1.0, 0.0).astype(BF16)
    upper = jnp.where(jnp.logical_and(same, tj >= ti), 1.0, 0.0).astype(BF16)
    tri, tri_t = (upper, lower) if reverse else (lower, upper)
    g1, g2, g3 = _split3(g)
    gcc_scr[...] = _dot(tri, g1) + _dot(tri, g2) + _dot(tri, g3)
    gcr = _dot_ta(g1, tri_t) + _dot_ta(g2, tri_t) + _dot_ta(g3, tri_t)
    for c in range(nchunk):
        gcr_scr[c] = gcr[:, c * CHUNK:(c + 1) * CHUNK]

    incl, strict = _tri_masks(reverse)

    step = {}

    def prep_body(pi, carry):
        rows, gblk, gcr_c, bt = {}, {}, {}, {}
        for cc in range(PREP_CHUNKS):
            c = pi * PREP_CHUNKS + cc
            rows[cc] = pl.ds(c * CHUNK, CHUNK)
            gblk[cc] = gcc_scr[rows[cc], :]
            gcr_c[cc] = gcr_scr[c]
            bt[cc] = b_scr[rows[cc], :]
        info = []
        for cc in range(PREP_CHUNKS):
            for h in range(DN_HEADS):
                lanes = slice(h * 128, (h + 1) * 128)
                r = SM_A + d * DN_HEADS + h
                gcol = gblk[cc][:, r:r + 1]
                grow = gcr_c[cc][r:r + 1, :]
                glast = gblk[cc][e_last:e_last + 1, r:r + 1]
                beta = bt[cc][:, SM_B + d * DN_HEADS + h:SM_B + d * DN_HEADS + h + 1]
                qc, kc, vc = q_scr[rows[cc], lanes], k_scr[rows[cc], lanes], v_scr[rows[cc], lanes]
                decay = jnp.exp(jnp.where(incl, gcol - grow, -1e30))
                kb = kc * beta
                eg = jnp.exp(gcol)
                qe_scr[rows[cc], lanes] = (qc * eg).astype(BF16)
                kd_scr[rows[cc], lanes] = (kc * jnp.exp(glast - gcol)).astype(BF16)
                info.append(dict(cc=cc, h=h, lanes=lanes, decay=decay, qe32=qc * eg, kbb=kb.astype(BF16), kcb=kc.astype(BF16),
                                 qcb=qc.astype(BF16), vbb=(vc * beta).astype(BF16), kgb=(kb * eg).astype(BF16)))
        kk = [_dot_tb(p["kbb"], p["kcb"]) for p in info]
        qk = [_dot_tb(p["qcb"], p["kcb"]) for p in info]
        for p, m in zip(info, qk):
            a_scr[rows[p["cc"]], p["h"] * CHUNK:(p["h"] + 1) * CHUNK] = (m * p["decay"]).astype(BF16)
        lms = [jnp.where(strict, m * p["decay"], 0.0) for p, m in zip(info, kk)]
        tinvs = [tv.astype(BF16) for tv in _inv_unit_tri_all(lms)]
        us = [_dot(tv, p["vbb"]) for tv, p in zip(tinvs, info)]
        ws = [_dot(tv, p["kgb"]) for tv, p in zip(tinvs, info)]
        for p, u, w in zip(info, us, ws):
            rw, ln = rows[p["cc"]], p["lanes"]
            ub, wb = u.astype(BF16), w.astype(BF16)
            ab = a_scr[rw, p["h"] * CHUNK:(p["h"] + 1) * CHUNK]
            kdb = kd_scr[rw, ln]
            u_scr[rw, ln] = _dot(ab, ub)
            qe_scr[rw, ln] = (p["qe32"] - _dot(ab, wb)).astype(BF16)
            step[(p["cc"], p["h"])] = (_dot_ta(kdb, wb).astype(BF16), _dot_ta(kdb, ub))
        return carry

    for pi in range(nchunk // PREP_CHUNKS):
        prep_body(pi, 0)

    def chunk_body(ci, carry):
        c = (nchunk - 1 - ci) if reverse else ci
        rows = pl.ds(c * CHUNK, CHUNK)
        glast_row = gcc_scr[pl.ds(c * CHUNK + e_last, 1), :]
        heads = range(DN_HEADS)
        lanes = [slice(h * 128, (h + 1) * 128) for h in heads]
        s_old = [s_scr[h] for h in heads]
        sb = [s.astype(BF16) for s in s_old]
        qs = [_dot(qe_scr[rows, lanes[h]], sb[h]) for h in heads]
        cs = [_dot(step[(c, h)][0], sb[h]) for h in heads]
        for h in heads:
            r = SM_A + d * DN_HEADS + h
            o_scr[rows, lanes[h]] = qs[h] + u_scr[rows, lanes[h]]
            s_scr[h] = jnp.exp(glast_row[:, r:r + 1]) * s_old[h] - cs[h] + step[(c, h)][1]
        return carry

    for ci in range(nchunk):
        chunk_body(ci, 0)

    if has_prev:
        o_ref[...] = _gated_rms_out(o_scr[...] + oprev_ref[...], z_ref[...], nw_ref[...]).astype(o_ref.dtype)
    else:
        o_ref[...] = o_scr[...]

    @pl.when(is_p)
    def _():
        st_ref[...] = s_scr[...]


def _seq_of_tile(t):
    return jnp.maximum(t - NT_P, 0) // TPS


def _dn_call(pm, ps, reverse, layer, conv_w, alog, dtb, state_delta, prev=None, norm_w=None):
    d = 1 if reverse else 0
    tmap = (lambda i: NT - 1 - i) if reverse else (lambda i: i)
    width = 3 * DN_W
    cb = C_DNQKV // width
    r = TS // HALO
    nblk = M // HALO
    small = [pl.BlockSpec((TS, 128), lambda i: (tmap(i), 0))]
    params = [pl.BlockSpec((1, 128), lambda i: (0, 0)),
              pl.BlockSpec((1, 128), lambda i: (0, 0)),
              pl.BlockSpec((None, None, None, DN_HEADS, DN_DK, 128), lambda i: (_seq_of_tile(tmap(i)), layer, d, 0, 0, 0))]
    tile_w = lambda w: pl.BlockSpec((TS, w), lambda i: (tmap(i), 0))
    has_prev = prev is not None
    out_specs = [tile_w(DN_W),
                 pl.BlockSpec((None, DN_HEADS, DN_DK, 128), lambda i: (jnp.minimum(tmap(i), NT_P - 1), 0, 0, 0))]
    out_shape = [jax.ShapeDtypeStruct((M, DN_W), BF16 if has_prev else F32),
                 jax.ShapeDtypeStruct((BATCH, DN_HEADS, DN_DK, 128), F32)]
    if has_prev:
        o_first, qkv = prev
        in_specs = [tile_w(width)] + small + params + [
            tile_w(DN_W), pl.BlockSpec((TS, DN_W), lambda i: (tmap(i), C_DNZ // DN_W)), pl.BlockSpec((1, 128), lambda i: (0, 0))]
        args = [qkv, ps, alog, dtb, state_delta, o_first, pm, norm_w.reshape(1, 128)]
    else:
        in_specs = [pl.BlockSpec((TS, width), lambda i: (tmap(i), cb)),
                    pl.BlockSpec((HALO, width), lambda i: (jnp.maximum(tmap(i) * r - 1, 0), cb)),
                    pl.BlockSpec((HALO, width), lambda i: (jnp.minimum((tmap(i) + 1) * r, nblk - 1), cb))] + small + [
            pl.BlockSpec((DN_CONV, width), lambda i: (0, 0))] + params
        args = [pm, pm, pm, ps, conv_w, alog, dtb, state_delta]
        out_specs.append(tile_w(width))
        out_shape.append(jax.ShapeDtypeStruct((M, width), F32))
    return pl.pallas_call(
        functools.partial(_dn_kernel, reverse, has_prev),
        grid=(NT,),
        in_specs=in_specs,
        out_specs=out_specs,
        out_shape=out_shape,
        scratch_shapes=[pltpu.VMEM((DN_HEADS, DN_DK, 128), F32),
                        pltpu.VMEM((TS, DN_W), F32), pltpu.VMEM((TS, DN_W), F32), pltpu.VMEM((TS, DN_W), F32),
                        pltpu.VMEM((TS, 128), F32), pltpu.VMEM((TS, 128), F32),
                        pltpu.VMEM((TS // CHUNK, 128, CHUNK), F32),
                        pltpu.VMEM((TS, DN_W), F32), pltpu.VMEM((TS, DN_W), BF16),
                        pltpu.VMEM((TS, DN_W), BF16), pltpu.VMEM((TS, DN_W), BF16),
                        pltpu.VMEM((TS, DN_HEADS * CHUNK), BF16),
                        pltpu.VMEM((TS, DN_W), F32)],
        compiler_params=_params(("arbitrary",)),
        name="deltanet_bwd" if reverse else "deltanet_fwd",
    )(*args)


def _gla_kernel(reverse, has_prev, q_ref, k_ref, v_ref, sm_ref, w2_ref, b2_ref, s0_ref, *rest):
    if has_prev:
        oprev_ref, r_ref, nw_ref, o_ref, st_ref, s_scr, o_scr = rest
    else:
        o_ref, st_ref, s_scr, o_scr = rest
    t = _scan_tile(reverse)
    is_p, first, last, _, _ = _tile_info(t)

    @pl.when(last if reverse else first)
    def _():
        s_scr[...] = jnp.where(is_p, 0.0, s0_ref[...])

    logits = _dot(sm_ref[...].astype(BF16), w2_ref[...]) + b2_ref[...]
    gk = -_softplus(-logits) / GLA_NORMALIZER

    ti = lax.broadcasted_iota(jnp.int32, (TS, TS), 0)
    tj = lax.broadcasted_iota(jnp.int32, (TS, TS), 1)
    same = (ti // CHUNK) == (tj // CHUNK)
    tri = jnp.where(jnp.logical_and(same, (tj >= ti) if reverse else (tj <= ti)), 1.0, 0.0).astype(BF16)
    g1, g2, g3 = _split3(gk)
    gc = _dot(tri, g1) + _dot(tri, g2) + _dot(tri, g3)

    incl, _ = _tri_masks(reverse)
    nchunk = TS // CHUNK
    e_last = 0 if reverse else CHUNK - 1
    eg = jnp.exp(gc)
    q_all = q_ref[...] * (GLA_DK ** -0.5)
    k_all = k_ref[...]
    qg_all = (q_all * eg).astype(BF16)
    kg_all = (k_all * jnp.exp(-gc)).astype(BF16)
    v_all = v_ref[...].astype(BF16)

    probs = [(c, h) for c in range(nchunk) for h in range(GLA_HEADS)]
    rws = {c: slice(c * CHUNK, (c + 1) * CHUNK) for c in range(nchunk)}
    kls = {h: slice(h * GLA_DK, (h + 1) * GLA_DK) for h in range(GLA_HEADS)}
    vls = {h: slice(h * GLA_DV, (h + 1) * GLA_DV) for h in range(GLA_HEADS)}
    glast = {c: gc[c * CHUNK + e_last:c * CHUNK + e_last + 1, :] for c in range(nchunk)}
    kd = {c: (k_all[rws[c], :] * jnp.exp(glast[c] - gc[rws[c], :])).astype(BF16) for c in range(nchunk)}
    a = {p: jnp.where(incl, _dot_tb(qg_all[rws[p[0]], kls[p[1]]], kg_all[rws[p[0]], kls[p[1]]]), 0.0).astype(BF16)
         for p in probs}
    av = {p: _dot(a[p], v_all[rws[p[0]], vls[p[1]]]) for p in probs}
    up = {p: _dot_ta(v_all[rws[p[0]], vls[p[1]]], kd[p[0]][:, kls[p[1]]]) for p in probs}
    order = list(reversed(range(nchunk))) if reverse else list(range(nchunk))
    s_in = {}
    for h in range(GLA_HEADS):
        s = s_scr[h]
        for c in order:
            s_in[(c, h)] = s.astype(BF16)
            s = jnp.exp(glast[c][:, kls[h]]) * s + up[(c, h)]
        s_scr[h] = s
    for p in probs:
        c, h = p
        o_scr[rws[c], vls[h]] = _dot_tb(qg_all[rws[c], kls[h]], s_in[p]) + av[p]

    if has_prev:
        o_ref[...] = _gated_rms_out(o_scr[...] + oprev_ref[...], r_ref[...], nw_ref[...]).astype(o_ref.dtype)
    else:
        o_ref[...] = o_scr[...]

    @pl.when(is_p)
    def _():
        st_ref[...] = s_scr[...]


def _gla_call(pm, ps, reverse, w2pad, b2, s0t, prev=None, norm_w=None):
    tmap = (lambda i: NT - 1 - i) if reverse else (lambda i: i)
    qkw = GLA_HEADS * GLA_DK
    vw = GLA_HEADS * GLA_DV
    in_specs = [
        pl.BlockSpec((TS, qkw), lambda i: (tmap(i), C_GQ // qkw)),
        pl.BlockSpec((TS, qkw), lambda i: (tmap(i), C_GK // qkw)),
        pl.BlockSpec((TS, vw), lambda i: (tmap(i), C_GV // vw)),
        pl.BlockSpec((TS, 128), lambda i: (tmap(i), 0)),
        pl.BlockSpec((128, qkw), lambda i: (0, 0)),
        pl.BlockSpec((1, qkw), lambda i: (0, 0)),
        pl.BlockSpec((None, GLA_HEADS, GLA_DV, GLA_DK), lambda i: (_seq_of_tile(tmap(i)), 0, 0, 0)),
    ]
    args = [pm, pm, pm, ps, w2pad, b2, s0t]
    has_prev = prev is not None
    if has_prev:
        in_specs += [pl.BlockSpec((TS, vw), lambda i: (tmap(i), 0)),
                     pl.BlockSpec((TS, vw), lambda i: (tmap(i), C_GR // vw)),
                     pl.BlockSpec((1, 128), lambda i: (0, 0))]
        args += [prev, pm, norm_w.reshape(1, 128)]
    out_dtype = BF16 if has_prev else F32
    return pl.pallas_call(
        functools.partial(_gla_kernel, reverse, has_prev),
        grid=(NT,),
        in_specs=in_specs,
        out_specs=[pl.BlockSpec((TS, vw), lambda i: (tmap(i), 0)),
                   pl.BlockSpec((None, GLA_HEADS, GLA_DV, GLA_DK), lambda i: (jnp.minimum(tmap(i), NT_P - 1), 0, 0, 0))],
        out_shape=[jax.ShapeDtypeStruct((M, vw), out_dtype),
                   jax.ShapeDtypeStruct((BATCH, GLA_HEADS, GLA_DV, GLA_DK), F32)],
        scratch_shapes=[pltpu.VMEM((GLA_HEADS, GLA_DV, GLA_DK), F32),
                        pltpu.VMEM((TS, vw), F32)],
        compiler_params=_params(("arbitrary",)),
        name="gla_bwd" if reverse else "gla_fwd",
    )(*args)


MOD_SH1, MOD_SC1, MOD_GT1, MOD_SH2, MOD_SC2, MOD_GT2 = range(N_MOD)


def kernel(x_prompt, x_sample, cache_k, cache_v, state_delta, state_gla, c, c_ctx, w_mod, b_mod, norm1, w_in, pool_w, pool_scale, attn_sink, dn_conv, dn_a_log, dn_dt_bias, dn_norm, gla_w2, gla_b2, gla_norm, w_br_pool, w_br_attn, w_br_delta, w_br_gla, w_out, norm2, w_gate, w_up, w_down, norm_f):
    x = (x_prompt.reshape(MP, D), x_sample.reshape(MS, D))
    cond8 = jnp.concatenate([c_ctx[None, :], c, jnp.zeros((8 - 1 - DEC_BATCH, D), F32)], axis=0)
    mod = _modulation(cond8, w_mod, b_mod).reshape(DEPTH, 8, N_MOD, 1, D)
    cos_t, sin_t = _rope_tables()
    s_gla_t = jnp.swapaxes(state_gla, -1, -2)
    w_out_bf16 = w_out.astype(BF16)
    w_brs = (w_br_pool, w_br_attn, w_br_delta, w_br_gla)
    w_in_t = jnp.swapaxes(w_in, 1, 2)
    gates_t = _gate_weights(w_in_t)

    new_k = jnp.zeros((BATCH, DEPTH, SEQ, KV_HEADS, HD), F32)
    new_v = jnp.zeros((BATCH, DEPTH, SEQ, KV_HEADS, HD), F32)
    yb = jnp.zeros((M, Q_HEADS * HD), BF16)
    sds, sgs = [], []
    for l in range(DEPTH):
        h, ps, *joined = _prenorm(x, norm1, mod, w_in_t, l, MOD_SC1, MOD_SH1)
        if joined:
            x = joined[0]
        pm = _mix_proj(h, w_in_t, l)

        ya = _pool(pm, pool_w[l], pool_scale[l])
        yb, new_k, new_v = _ctx_attn(pm, attn_sink[l], new_k, new_v, l, yb)
        yb = _lat_attn(pm, attn_sink[l], cos_t, sin_t, cache_k, cache_v, l, yb)

        alog = jnp.zeros((1, 128), F32).at[0, SM_A:SM_A + 8].set(dn_a_log[l].reshape(8))
        dtb = jnp.zeros((1, 128), F32).at[0, SM_A:SM_A + 8].set(dn_dt_bias[l].reshape(8))
        o_b, sd_b, dn_qkv = _dn_call(pm, ps, True, l, dn_conv[l], alog, dtb, state_delta)
        yc, sd_f = _dn_call(pm, ps, False, l, dn_conv[l], alog, dtb, state_delta, prev=(o_b, dn_qkv), norm_w=dn_norm[l])

        w2pad = [jnp.zeros((128, GLA_HEADS * GLA_DK), F32).at[SM_LR + o * GLA_RANK:SM_LR + (o + 1) * GLA_RANK].set(gla_w2[l, o]).astype(BF16)
                 for o in range(2)]
        g_b, sg_b = _gla_call(pm, ps, True, w2pad[1], gla_b2[l, 1].reshape(1, -1), s_gla_t[:, l, 1])
        yd, sg_f = _gla_call(pm, ps, False, w2pad[0], gla_b2[l, 0].reshape(1, -1), s_gla_t[:, l, 0], prev=g_b, norm_w=gla_norm[l])

        merged = _merge(h, gates_t, (ya, yb, yc, yd), w_brs, l)
        x, h2 = _out_proj(merged, w_out_bf16, x, mod, norm2, l, MOD_GT1, MOD_SC2, MOD_SH2)
        act = _ffn_up(h2, w_gate, w_up, l)
        x = _ffn_down(act, w_down, x, mod, l, MOD_GT2)

        sds.append(jnp.stack([sd_f, sd_b], axis=1))
        sgs.append(jnp.swapaxes(jnp.stack([sg_f, sg_b], axis=1), -1, -2))

    y_ctx, y_lat = _finalnorm(x, norm_f)
    return (y_ctx.reshape(BATCH, SEQ, D), y_lat.reshape(DEC_BATCH, DEC_SEQ, D),
            new_k, new_v, jnp.stack(sds, axis=1), jnp.stack(sgs, axis=1))
```
